```python
import math
import jax, jax.numpy as jnp
from jax import lax
import numpy as np

D_MODEL = 1024
BATCH = 1
SEQ = 16384
DEPTH = 1

MLA_HEADS = 8
QK_NOPE_DIM = 128
QK_ROPE_DIM = 64
V_HEAD_DIM = 128
Q_LORA_RANK = 384
KV_LORA_RANK = 256
MLA_WIDTH = MLA_HEADS * V_HEAD_DIM
QK_HEAD_DIM = QK_NOPE_DIM + QK_ROPE_DIM
ROPE_THETA = 10000.0
ATTN_BLOCK = 128

SSM_HEAD_DIM = 64
SSM_HEADS = 16
SSM_WIDTH = SSM_HEADS * SSM_HEAD_DIM
SSM_GROUPS = 2
SSM_STATE = 128
CONV_WIDTH = 4
CHUNK = 128
CONV_CH = SSM_WIDTH + 2 * SSM_GROUPS * SSM_STATE

MIX_WIDTH = MLA_WIDTH + SSM_WIDTH

IN_SPLITS = (Q_LORA_RANK, KV_LORA_RANK + QK_ROPE_DIM, MLA_WIDTH, CONV_CH, SSM_HEADS, SSM_WIDTH)
IN_WIDTH = sum(IN_SPLITS)

DEEPNORM_ALPHA = (2.0 * DEPTH) ** 0.25
DEEPNORM_BETA = (8.0 * DEPTH) ** -0.25
RMS_EPS = 1e-6
LN_EPS = 1e-5

kernel_name = "hybrid_mla_ssd_parallel_heads_deepnorm_adaln"


def split_last(x, sizes):
    out, start = [], 0
    for s in sizes:
        out.append(x[..., start:start + s])
        start += s
    return out


def rms_norm(x, g):
    xf = x.astype(jnp.float32)
    y = xf * lax.rsqrt(jnp.mean(xf * xf, axis=-1, keepdims=True) + RMS_EPS)
    return (y * g.astype(jnp.float32)).astype(x.dtype)


def layer_norm(x, g, b):
    xf = x.astype(jnp.float32)
    mu = jnp.mean(xf, axis=-1, keepdims=True)
    var = jnp.mean(jnp.square(xf - mu), axis=-1, keepdims=True)
    y = (xf - mu) * lax.rsqrt(var + LN_EPS)
    return (y * g.astype(jnp.float32) + b.astype(jnp.float32)).astype(x.dtype)


def apply_rope(x, positions):
    half = QK_ROPE_DIM // 2
    inv_freq = 1.0 / (ROPE_THETA ** (jnp.arange(half, dtype=jnp.float32) / half))
    ang = positions.astype(jnp.float32)[..., None] * inv_freq
    cos = jnp.cos(ang)[:, :, None, :]
    sin = jnp.sin(ang)[:, :, None, :]
    xf = x.astype(jnp.float32)
    x1, x2 = xf[..., :half], xf[..., half:]
    out = jnp.concatenate([x1 * cos - x2 * sin, x2 * cos + x1 * sin], axis=-1)
    return out.astype(x.dtype)


def causal_block_attention(q, k, v, scale):
    b, s, h, dq = q.shape
    nb = s // ATTN_BLOCK
    qb = q.reshape(b, nb, ATTN_BLOCK, h, dq).transpose(1, 0, 2, 3, 4)
    key_pos = jnp.arange(s)

    def one_block(args):
        q_blk, i = args
        sc = jnp.einsum('bqhd,bkhd->bhqk', q_blk, k,
                        preferred_element_type=jnp.float32) * scale
        q_pos = i * ATTN_BLOCK + jnp.arange(ATTN_BLOCK)
        mask = key_pos[None, :] <= q_pos[:, None]
        sc = jnp.where(mask[None, None], sc, -jnp.inf)
        p = jax.nn.softmax(sc, axis=-1).astype(v.dtype)
        return jnp.einsum('bhqk,bkhd->bqhd', p, v)

    out = lax.map(one_block, (qb, jnp.arange(nb)))
    return out.transpose(1, 0, 2, 3, 4).reshape(b, s, h, v.shape[-1])


def mla_branch(q_lat, kv_lat, positions, q_norm_g, w_qb, kv_norm_g, w_kvb):
    b, s, _ = q_lat.shape
    q = (rms_norm(q_lat, q_norm_g) @ w_qb).reshape(b, s, MLA_HEADS, QK_HEAD_DIM)
    q_nope, q_rope = q[..., :QK_NOPE_DIM], q[..., QK_NOPE_DIM:]
    c_kv, k_rope = kv_lat[..., :KV_LORA_RANK], kv_lat[..., KV_LORA_RANK:]
    kv = (rms_norm(c_kv, kv_norm_g) @ w_kvb).reshape(b, s, MLA_HEADS, QK_NOPE_DIM + V_HEAD_DIM)
    k_nope, v = kv[..., :QK_NOPE_DIM], kv[..., QK_NOPE_DIM:]
    q_rope = apply_rope(q_rope, positions)
    k_rope = apply_rope(k_rope[:, :, None, :], positions)
    q = jnp.concatenate([q_nope, q_rope], axis=-1)
    k = jnp.concatenate([k_nope, jnp.broadcast_to(k_rope, (b, s, MLA_HEADS, QK_ROPE_DIM))], axis=-1)
    o = causal_block_attention(q, k, v, QK_HEAD_DIM ** -0.5)
    return o.reshape(b, s, MLA_WIDTH)


def causal_depthwise_conv(x, w, bias):
    out = lax.conv_general_dilated(
        x, w[:, None, :], window_strides=(1,), padding=[(CONV_WIDTH - 1, 0)],
        dimension_numbers=('NWC', 'WIO', 'NWC'), feature_group_count=x.shape[-1])
    return out + bias


def ssd_chunked(x, da, bm, cm):
    out_dtype = x.dtype
    b, s, h, p = x.shape
    nc, r = s // CHUNK, h // SSM_GROUPS
    f32 = jnp.float32
    X = x.astype(f32).reshape(b, nc, CHUNK, SSM_GROUPS, r, p)
    A = da.astype(f32).reshape(b, nc, CHUNK, SSM_GROUPS, r)
    Bc = bm.astype(f32).reshape(b, nc, CHUNK, SSM_GROUPS, SSM_STATE)
    Cc = cm.astype(f32).reshape(b, nc, CHUNK, SSM_GROUPS, SSM_STATE)
    a_cum = jnp.cumsum(A, axis=2)
    seg = a_cum[:, :, :, None] - a_cum[:, :, None, :]
    tri = jnp.tril(jnp.ones((CHUNK, CHUNK), dtype=bool))[:, :, None, None]
    decay = jnp.exp(jnp.where(tri, seg, -jnp.inf))
    cb = jnp.einsum('bclgn,bcsgn->bclsg', Cc, Bc)
    y_diag = jnp.einsum('bclsgr,bcsgrp->bclgrp', cb[..., None] * decay, X)
    decay_to_end = jnp.exp(a_cum[:, :, -1:] - a_cum)
    states = jnp.einsum('bclgn,bclgr,bclgrp->bcgrpn', Bc, decay_to_end, X)
    chunk_decay = jnp.exp(a_cum[:, :, -1])

    def step(hc, inp):
        st, dc = inp
        return hc * dc[..., None, None] + st, hc

    h0 = jnp.zeros((b, SSM_GROUPS, r, p, SSM_STATE), f32)
    _, prev = lax.scan(step, h0, (states.transpose(1, 0, 2, 3, 4, 5),
                                   chunk_decay.transpose(1, 0, 2, 3)))
    prev = prev.transpose(1, 0, 2, 3, 4, 5)
    y_off = jnp.einsum('bclgn,bcgrpn,bclgr->bclgrp', Cc, prev, jnp.exp(a_cum))
    return (y_diag + y_off).reshape(b, s, h, p).astype(out_dtype)


def ssd_branch(xbc_raw, dt_raw, z, conv_w, conv_b, dt_bias, a_log, d_skip, ssm_norm_g):
    b, s, _ = xbc_raw.shape
    xbc = jax.nn.silu(causal_depthwise_conv(xbc_raw, conv_w, conv_b))
    xs, bm, cm = split_last(xbc, (SSM_WIDTH, SSM_GROUPS * SSM_STATE, SSM_GROUPS * SSM_STATE))
    xs = xs.reshape(b, s, SSM_HEADS, SSM_HEAD_DIM)
    bm = bm.reshape(b, s, SSM_GROUPS, SSM_STATE)
    cm = cm.reshape(b, s, SSM_GROUPS, SSM_STATE)
    dt = jax.nn.softplus(dt_raw.astype(jnp.float32) + dt_bias.astype(jnp.float32))
    a = -jnp.exp(a_log.astype(jnp.float32))
    y = ssd_chunked(xs * dt[..., None].astype(xs.dtype), dt * a, bm, cm)
    y = y + xs * d_skip[:, None]
    hf = (y.reshape(b, s, SSM_WIDTH).astype(jnp.float32)
          * jax.nn.silu(z.astype(jnp.float32))).reshape(b, s, SSM_GROUPS, -1)
    hf = hf * lax.rsqrt(jnp.mean(hf * hf, axis=-1, keepdims=True) + RMS_EPS)
    return (hf.reshape(b, s, SSM_WIDTH) * ssm_norm_g.astype(jnp.float32)).astype(xbc_raw.dtype)


def hybrid_layer(x, c, positions, w_ada, b_ada, w_in, q_norm_g, w_qb, kv_norm_g, w_kvb,
                 conv_w, conv_b, dt_bias, a_log, d_skip, ssm_norm_g, w_out, ln_g, ln_b):
    mod = c @ w_ada + b_ada
    shift, scale, gate = mod[:, :D_MODEL], mod[:, D_MODEL:2 * D_MODEL], mod[:, 2 * D_MODEL:]
    u = x * (1.0 + scale[:, None, :]) + shift[:, None, :]
    proj = u @ w_in
    q_lat, kv_lat, z_attn, xbc, dt_raw, z_ssm = split_last(proj, IN_SPLITS)
    o_attn = mla_branch(q_lat, kv_lat, positions, q_norm_g, w_qb, kv_norm_g, w_kvb) * jax.nn.silu(z_attn)
    o_ssm = ssd_branch(xbc, dt_raw, z_ssm, conv_w, conv_b, dt_bias, a_log, d_skip, ssm_norm_g)
    mixed = jnp.concatenate([o_attn, o_ssm], axis=-1) @ w_out
    return layer_norm(DEEPNORM_ALPHA * x + gate[:, None, :] * mixed, ln_g, ln_b)


def setup_inputs(seed: int = 0) -> dict:
    key = jax.random.key(seed)
    ks = jax.random.split(key, 20)
    f32 = jnp.float32
    n = lambda k, shape, s: jax.random.normal(k, shape, f32) * s
    L = DEPTH
    dt0 = jnp.exp(jax.random.uniform(ks[12], (L, SSM_HEADS), f32, math.log(1e-3), math.log(1e-1)))
    return {
        "x": jax.random.normal(ks[0], (BATCH, SEQ, D_MODEL), f32),
        "c": jax.random.normal(ks[1], (BATCH, D_MODEL), f32),
        "positions": jnp.broadcast_to(jnp.arange(SEQ, dtype=jnp.int32), (BATCH, SEQ)),
        "w_ada": n(ks[2], (L, D_MODEL, 3 * D_MODEL), 0.5 * D_MODEL ** -0.5),
        "b_ada": n(ks[3], (L, 3 * D_MODEL), 0.02),
        "w_in": n(ks[4], (L, D_MODEL, IN_WIDTH), D_MODEL ** -0.5),
        "q_norm_g": 1.0 + n(ks[5], (L, Q_LORA_RANK), 0.05),
        "w_qb": n(ks[6], (L, Q_LORA_RANK, MLA_HEADS * QK_HEAD_DIM), Q_LORA_RANK ** -0.5),
        "kv_norm_g": 1.0 + n(ks[7], (L, KV_LORA_RANK), 0.05),
        "w_kvb": n(ks[8], (L, KV_LORA_RANK, MLA_HEADS * (QK_NOPE_DIM + V_HEAD_DIM)), KV_LORA_RANK ** -0.5),
        "conv_w": n(ks[9], (L, CONV_WIDTH, CONV_CH), CONV_WIDTH ** -0.5),
        "conv_b": n(ks[10], (L, CONV_CH), 0.02),
        "dt_bias": dt0 + jnp.log(-jnp.expm1(-dt0)),
        "a_log": jnp.log(jax.random.uniform(ks[13], (L, SSM_HEADS), f32, 1.0, 16.0)),
        "d_skip": 1.0 + n(ks[14], (L, SSM_HEADS), 0.1),
        "ssm_norm_g": 1.0 + n(ks[15], (L, SSM_WIDTH), 0.05),
        "w_out": n(ks[16], (L, MIX_WIDTH, D_MODEL), DEEPNORM_BETA * MIX_WIDTH ** -0.5),
        "ln_g": 1.0 + n(ks[17], (L, D_MODEL), 0.05),
        "ln_b": n(ks[18], (L, D_MODEL), 0.02),
    }


def reference(x, c, positions, w_ada, b_ada, w_in, q_norm_g, w_qb, kv_norm_g, w_kvb,
              conv_w, conv_b, dt_bias, a_log, d_skip, ssm_norm_g, w_out, ln_g, ln_b):
    h = x
    for l in range(DEPTH):
        h = hybrid_layer(h, c, positions, w_ada[l], b_ada[l], w_in[l], q_norm_g[l], w_qb[l],
                         kv_norm_g[l], w_kvb[l], conv_w[l], conv_b[l], dt_bias[l], a_log[l],
                         d_skip[l], ssm_norm_g[l], w_out[l], ln_g[l], ln_b[l])
    return h
```

```python
import functools
import math

import jax
import jax.numpy as jnp
from jax import lax
from jax.experimental import pallas as pl
from jax.experimental.pallas import tpu as pltpu

F32 = jnp.float32
BF16 = jnp.bfloat16

MLA_HEADS = 8
QK_NOPE_DIM = 128
QK_ROPE_DIM = 64
V_HEAD_DIM = 128
ROPE_THETA = 10000.0
SSM_HEAD_DIM = 64
SSM_GROUPS = 2
SSM_STATE = 128
CHUNK = 128
RMS_EPS = 1e-6
LN_EPS = 1e-5

LANES = 128
SUBLANES = 8
MXU_DIM = 256
QK_PAD = MXU_DIM
VMEM_LIMIT = 56 * 1024 * 1024

LOG2E = 1.4426950408889634
NEG_BIG = -1e30


def _silu(z):
    return z * (1.0 / (1.0 + jnp.exp(-z)))


def _softplus(z):
    return jnp.maximum(z, 0.0) + jnp.log1p(jnp.exp(-jnp.abs(z)))


def _split_bf16(x, parts):
    out, rem = [], x
    for _ in range(parts):
        hi = rem.astype(BF16)
        out.append(hi)
        rem = rem - hi.astype(F32)
    return out


def _mod_kernel(c_ref, w_ref, b_ref, o_ref):
    o_ref[...] = jnp.sum(w_ref[...] * c_ref[...], axis=0, keepdims=True) + b_ref[...]


def _adaln_mod(c_col, w_ada, b_ada):
    d, n = w_ada.shape
    bn = 1024
    return pl.pallas_call(
        _mod_kernel,
        grid=(n // bn,),
        in_specs=[pl.BlockSpec((d, 1), lambda j: (0, 0)),
                  pl.BlockSpec((d, bn), lambda j: (0, j)),
                  pl.BlockSpec((1, bn), lambda j: (0, j))],
        out_specs=pl.BlockSpec((1, bn), lambda j: (0, j)),
        out_shape=jax.ShapeDtypeStruct((1, n), F32),
        name="adaln_mod",
    )(c_col, w_ada, b_ada)


def _in_kernel(x_ref, mod_ref, pos_ref, rope_ref, w_in_ref, qg_ref, w_qb_ref, kvg_ref, w_kvb_ref,
               conv_w_ref, conv_b_ref, dtb_ref,
               q_out, k_out, v_out, gza_out, xs_out, bm_out, cm_out, dt_out, gzs_out,
               xbuf, *, tm, off, q_scale):
    i = pl.program_id(0)
    shift = mod_ref[0:1, :]
    scale = mod_ref[1:2, :]
    u = x_ref[...] * (1.0 + scale) + shift
    proj = jnp.dot(u.astype(BF16), w_in_ref[...], preferred_element_type=F32)

    ang = pos_ref[...].astype(F32) * rope_ref[0:1, :] + rope_ref[1:2, :]
    cs = jnp.cos(ang)
    lane = lax.broadcasted_iota(jnp.int32, (tm, LANES), 1)
    low_half = lane < QK_ROPE_DIM

    def rope(rr):
        t = rr * cs
        return jnp.where(low_half, t + pltpu.roll(t, QK_ROPE_DIM, axis=1), 0.0)

    q_lat = proj[:, off["q"]:off["q"] + qg_ref.shape[1]]
    qn = q_lat * lax.rsqrt(jnp.mean(q_lat * q_lat, axis=-1, keepdims=True) + RMS_EPS) * qg_ref[...]
    qf = jnp.dot(qn.astype(BF16), w_qb_ref[...], preferred_element_type=F32)
    for h in range(MLA_HEADS):
        c0 = h * QK_PAD
        q_out[:, c0:c0 + QK_NOPE_DIM] = (qf[:, c0:c0 + QK_NOPE_DIM] * q_scale).astype(BF16)
        q_out[:, c0 + QK_NOPE_DIM:c0 + QK_PAD] = (rope(qf[:, c0 + QK_NOPE_DIM:c0 + QK_PAD]) * q_scale).astype(BF16)

    c_kv = proj[:, off["ckv"]:off["ckv"] + kvg_ref.shape[1]]
    ckvn = c_kv * lax.rsqrt(jnp.mean(c_kv * c_kv, axis=-1, keepdims=True) + RMS_EPS) * kvg_ref[...]
    kvf = jnp.dot(ckvn.astype(BF16), w_kvb_ref[...], preferred_element_type=F32)
    k_rope = rope(proj[:, off["krope"]:off["krope"] + LANES]).astype(BF16)
    for h in range(MLA_HEADS):
        c0 = h * (QK_NOPE_DIM + V_HEAD_DIM)
        k_out[:, h * QK_PAD:h * QK_PAD + QK_NOPE_DIM] = kvf[:, c0:c0 + QK_NOPE_DIM].astype(BF16)
        k_out[:, h * QK_PAD + QK_NOPE_DIM:(h + 1) * QK_PAD] = k_rope
        v_out[:, h * V_HEAD_DIM:(h + 1) * V_HEAD_DIM] = kvf[:, c0 + QK_NOPE_DIM:c0 + QK_NOPE_DIM + V_HEAD_DIM].astype(BF16)

    nz = gza_out.shape[1]
    gza_out[...] = _silu(proj[:, off["za"]:off["za"] + nz]).astype(BF16)
    nzs = gzs_out.shape[1]
    gzs_out[...] = _silu(proj[:, off["zs"]:off["zs"] + nzs]).astype(BF16)
    dt_out[...] = _softplus(proj[:, off["dt"]:off["dt"] + LANES] + dtb_ref[...])

    nconv = conv_w_ref.shape[0]
    nch = conv_w_ref.shape[1]

    @pl.when(i == 0)
    def _():
        xbuf[0:SUBLANES, :] = jnp.zeros((SUBLANES, nch), F32)

    xbuf[SUBLANES:SUBLANES + tm, :] = proj[:, off["xbc"]:off["xbc"] + nch]
    acc = conv_b_ref[...] + conv_w_ref[nconv - 1:nconv, :] * xbuf[SUBLANES:SUBLANES + tm, :]
    for j in range(nconv - 1):
        r0 = SUBLANES - (nconv - 1) + j
        acc = acc + conv_w_ref[j:j + 1, :] * xbuf[r0:r0 + tm, :]
    xbuf[0:SUBLANES, :] = xbuf[tm:tm + SUBLANES, :]
    xc = _silu(acc)
    nx = xs_out.shape[1]
    nb = bm_out.shape[1]
    xs_out[...] = xc[:, :nx].astype(BF16)
    bm_out[...] = xc[:, nx:nx + nb].astype(BF16)
    cm_out[...] = xc[:, nx + nb:nx + 2 * nb].astype(BF16)


def _in_proj(x2, mod3, pos_col, rope_tab, w_in_p, qg, w_qb_p, kvg, w_kvb, conv_w, conv_b, dtb,
             *, off, tm, q_scale):
    s, d = x2.shape
    nq = MLA_HEADS * QK_PAD
    nv = MLA_HEADS * V_HEAD_DIM
    nch = conv_w.shape[1]
    nbc = SSM_GROUPS * SSM_STATE
    nx = nch - 2 * nbc
    row = lambda w: pl.BlockSpec((tm, w), lambda i: (i, 0))
    full = lambda a: pl.BlockSpec(a.shape, lambda i: (0,) * a.ndim, pipeline_mode=pl.Buffered(1))
    outs = [
        jax.ShapeDtypeStruct((s, nq), BF16),
        jax.ShapeDtypeStruct((s, nq), BF16),
        jax.ShapeDtypeStruct((s, nv), BF16),
        jax.ShapeDtypeStruct((s, nv), BF16),
        jax.ShapeDtypeStruct((s, nx), BF16),
        jax.ShapeDtypeStruct((s, nbc), BF16),
        jax.ShapeDtypeStruct((s, nbc), BF16),
        jax.ShapeDtypeStruct((s, LANES), F32),
        jax.ShapeDtypeStruct((s, nx), BF16),
    ]
    return pl.pallas_call(
        functools.partial(_in_kernel, tm=tm, off=off, q_scale=q_scale),
        grid=(s // tm,),
        in_specs=[row(d), full(mod3), pl.BlockSpec((tm, 1), lambda i: (i, 0)), full(rope_tab),
                  full(w_in_p), full(qg), full(w_qb_p), full(kvg), full(w_kvb),
                  full(conv_w), full(conv_b), full(dtb)],
        out_specs=[row(o.shape[1]) for o in outs],
        out_shape=outs,
        scratch_shapes=[pltpu.VMEM((tm + 2 * SUBLANES, nch), F32)],
        compiler_params=pltpu.CompilerParams(dimension_semantics=("arbitrary",),
                                             vmem_limit_bytes=VMEM_LIMIT),
        name="in_proj",
    )(x2, mod3, pos_col, rope_tab, w_in_p, qg, w_qb_p, kvg, w_kvb, conv_w, conv_b, dtb)


def _attn_kernel(q_ref, k_ref, v_ref, o_ref, m_scr, l_scr, acc_scr, *, tq):
    qi = pl.program_id(1)
    q = q_ref[...]
    m_scr[...] = jnp.full(m_scr.shape, NEG_BIG, F32)
    l_scr[...] = jnp.zeros(l_scr.shape, F32)
    acc_scr[...] = jnp.zeros(acc_scr.shape, F32)

    def tile(j, masked):
        start = pl.multiple_of(j * tq, tq)
        k = k_ref[pl.ds(start, tq), :]
        v = v_ref[pl.ds(start, tq), :]
        s = lax.dot_general(q, k, (((1,), (1,)), ((), ())), preferred_element_type=F32)
        if masked:
            r = lax.broadcasted_iota(jnp.int32, (tq, tq), 0)
            c = lax.broadcasted_iota(jnp.int32, (tq, tq), 1)
            s = jnp.where(c <= r, s, NEG_BIG)
        m_old = m_scr[...]
        m_new = jnp.maximum(m_old, jnp.max(s, axis=-1, keepdims=True))
        alpha = jnp.exp2(m_old - m_new)
        p = jnp.exp2(s - m_new)
        l_scr[...] = alpha * l_scr[...] + jnp.sum(p, axis=-1, keepdims=True)
        acc_scr[...] = alpha * acc_scr[...] + jnp.dot(p.astype(BF16), v, preferred_element_type=F32)
        m_scr[...] = m_new

    def body(j, carry):
        tile(j, False)
        return carry

    lax.fori_loop(0, qi, body, 0)
    tile(qi, True)
    o_ref[...] = (acc_scr[...] * (1.0 / l_scr[...])).astype(o_ref.dtype)


def _attention(q, k, v, *, tq):
    s = q.shape[0]
    return pl.pallas_call(
        functools.partial(_attn_kernel, tq=tq),
        grid=(MLA_HEADS, s // tq),
        in_specs=[pl.BlockSpec((tq, QK_PAD), lambda h, i: (i, h)),
                  pl.BlockSpec((s, QK_PAD), lambda h, i: (0, h)),
                  pl.BlockSpec((s, V_HEAD_DIM), lambda h, i: (0, h))],
        out_specs=pl.BlockSpec((tq, V_HEAD_DIM), lambda h, i: (i, h)),
        out_shape=jax.ShapeDtypeStruct((s, MLA_HEADS * V_HEAD_DIM), BF16),
        scratch_shapes=[pltpu.VMEM((tq, 1), F32), pltpu.VMEM((tq, 1), F32),
                        pltpu.VMEM((tq, V_HEAD_DIM), F32)],
        compiler_params=pltpu.CompilerParams(dimension_semantics=("arbitrary", "arbitrary"),
                                             vmem_limit_bytes=VMEM_LIMIT),
        name="attention",
    )(q, k, v)


def _ssd_kernel(xs_ref, bm_ref, cm_ref, dt_ref, gz_ref, alog_ref, dskip_ref, g_ref, e2_ref,
                o_ref, state, y_scr, *, rows, nheads):
    i = pl.program_id(0)
    gw = SSM_STATE
    hw = state.shape[2]

    @pl.when(i == 0)
    def _():
        state[...] = jnp.zeros(state.shape, F32)

    a_neg = -jnp.exp(alog_ref[...])
    r_i = lax.broadcasted_iota(jnp.int32, (CHUNK, CHUNK), 0)
    c_i = lax.broadcasted_iota(jnp.int32, (CHUNK, CHUNK), 1)
    tri = c_i <= r_i
    tri_bf = tri.astype(BF16)
    lane = lax.broadcasted_iota(jnp.int32, (CHUNK, LANES), 1)
    first_copy = lane < nheads
    head_lo = lane < SSM_HEAD_DIM
    e2 = e2_ref[...]

    def expand(v):
        hi = v.astype(BF16)
        lo = (v - hi.astype(F32)).astype(BF16)
        return jnp.dot(jnp.where(first_copy, hi, lo), e2, preferred_element_type=F32)

    for cidx in range(rows // CHUNK):
        sl = pl.ds(cidx * CHUNK, CHUNK)
        dt = dt_ref[sl, :]
        da = dt * a_neg
        a_cum = jnp.zeros((CHUNK, LANES), F32)
        for part in _split_bf16(da, 3):
            a_cum = a_cum + jnp.dot(tri_bf, part, preferred_element_type=F32)
        a_cum_t = a_cum.T
        a_last = a_cum[CHUNK - 1:CHUNK, :]
        ea = jnp.exp(a_cum)
        dte = jnp.exp(a_last - a_cum)

        xs = xs_ref[sl, :].astype(F32)
        xd = xs * expand(dt)
        ea_x = expand(ea)
        xdd = (xd * expand(dte)).astype(BF16)
        xd_bf = xd.astype(BF16)

        for g in range(SSM_GROUPS):
            bg = bm_ref[sl, g * gw:(g + 1) * gw]
            cg = cm_ref[sl, g * gw:(g + 1) * gw]
            cb = lax.dot_general(cg, bg, (((1,), (1,)), ((), ())), preferred_element_type=F32)
            prev = state[g]
            y_off = jnp.dot(cg, prev.astype(BF16), preferred_element_type=F32)
            new = lax.dot_general(bg, xdd[:, g * hw:(g + 1) * hw], (((0,), (0,)), ((), ())),
                                  preferred_element_type=F32)
            state[g] = prev * ea_x[CHUNK - 1:CHUNK, g * hw:(g + 1) * hw] + new
            y_scr[:, g * hw:(g + 1) * hw] = y_off * ea_x[:, g * hw:(g + 1) * hw]

            hpg = hw // SSM_HEAD_DIM
            for pair in range(hpg // 2):
                h0 = g * hpg + 2 * pair
                c0 = h0 * SSM_HEAD_DIM
                xp = xd_bf[:, c0:c0 + LANES]
                yp = jnp.zeros((CHUNK, LANES), F32)
                for k, keep in ((0, head_lo), (1, ~head_lo)):
                    h = h0 + k
                    seg = a_cum[:, h:h + 1] - a_cum_t[h:h + 1, :]
                    m_h = (cb * jnp.exp(jnp.where(tri, seg, NEG_BIG))).astype(BF16)
                    yp = yp + jnp.dot(m_h, jnp.where(keep, xp, jnp.zeros_like(xp)),
                                      preferred_element_type=F32)
                y_scr[:, c0:c0 + LANES] = y_scr[:, c0:c0 + LANES] + yp

        y = y_scr[...] + xs * dskip_ref[...]
        hf = y * gz_ref[sl, :].astype(F32)
        for g in range(SSM_GROUPS):
            hg = hf[:, g * hw:(g + 1) * hw]
            ms = jnp.mean(hg * hg, axis=-1, keepdims=True)
            o_ref[sl, g * hw:(g + 1) * hw] = (hg * lax.rsqrt(ms + RMS_EPS)
                                               * g_ref[:, g * hw:(g + 1) * hw]).astype(o_ref.dtype)


def _ssd(xs, bm, cm, dt, gzs, alog, dskip_x, norm_g, e2, *, rows, nheads):
    s, nx = xs.shape
    hw = nx // SSM_GROUPS
    row = lambda a: pl.BlockSpec((rows, a.shape[1]), lambda i: (i, 0))
    full = lambda a: pl.BlockSpec(a.shape, lambda i: (0,) * a.ndim)
    return pl.pallas_call(
        functools.partial(_ssd_kernel, rows=rows, nheads=nheads),
        grid=(s // rows,),
        in_specs=[row(xs), row(bm), row(cm), row(dt), row(gzs),
                  full(alog), full(dskip_x), full(norm_g), full(e2)],
        out_specs=pl.BlockSpec((rows, nx), lambda i: (i, 0)),
        out_shape=jax.ShapeDtypeStruct((s, nx), BF16),
        scratch_shapes=[pltpu.VMEM((SSM_GROUPS, SSM_STATE, hw), F32),
                        pltpu.VMEM((CHUNK, nx), F32)],
        compiler_params=pltpu.CompilerParams(dimension_semantics=("arbitrary",),
                                             vmem_limit_bytes=VMEM_LIMIT),
        name="ssd",
    )(xs, bm, cm, dt, gzs, alog, dskip_x, norm_g, e2)


def _out_kernel(oa_ref, gza_ref, os_ref, x_ref, w_ref, mod_ref, lng_ref, lnb_ref, o_ref, *, alpha):
    na = oa_ref.shape[1]
    oa = (oa_ref[...].astype(F32) * gza_ref[...].astype(F32)).astype(BF16)
    mixed = jnp.dot(oa, w_ref[0:na, :], preferred_element_type=F32)
    mixed = mixed + jnp.dot(os_ref[...], w_ref[na:, :], preferred_element_type=F32)
    y = alpha * x_ref[...] + mod_ref[2:3, :] * mixed
    mu = jnp.mean(y, axis=-1, keepdims=True)
    yc = y - mu
    var = jnp.mean(yc * yc, axis=-1, keepdims=True)
    o_ref[...] = yc * lax.rsqrt(var + LN_EPS) * lng_ref[...] + lnb_ref[...]


def _out_proj(oa, gza, o_ssm, x2, w_out_bf, mod3, ln_g, ln_b, *, tm, alpha):
    s, d = x2.shape
    row = lambda a: pl.BlockSpec((tm, a.shape[1]), lambda i: (i, 0))
    full = lambda a: pl.BlockSpec(a.shape, lambda i: (0,) * a.ndim)
    return pl.pallas_call(
        functools.partial(_out_kernel, alpha=alpha),
        grid=(s // tm,),
        in_specs=[row(oa), row(gza), row(o_ssm), row(x2), full(w_out_bf), full(mod3),
                  full(ln_g), full(ln_b)],
        out_specs=pl.BlockSpec((tm, d), lambda i: (i, 0)),
        out_shape=jax.ShapeDtypeStruct((s, d), F32),
        compiler_params=pltpu.CompilerParams(dimension_semantics=("arbitrary",),
                                             vmem_limit_bytes=VMEM_LIMIT),
        name="out_proj",
    )(oa, gza, o_ssm, x2, w_out_bf, mod3, ln_g, ln_b)


def _pad_cols(a, width):
    return jnp.pad(a, ((0, 0), (0, width - a.shape[1])))


def _layer(x2, c, pos_col, w_ada, b_ada, w_in, q_norm_g, w_qb, kv_norm_g, w_kvb,
           conv_w, conv_b, dt_bias, a_log, d_skip, ssm_norm_g, w_out, ln_g, ln_b, *, depth):
    s, d = x2.shape
    q_rank = q_norm_g.shape[0]
    kv_rank = kv_norm_g.shape[0]
    nheads = dt_bias.shape[0]
    nch = conv_w.shape[1]
    nv = MLA_HEADS * V_HEAD_DIM
    nx = nheads * SSM_HEAD_DIM
    half = QK_ROPE_DIM // 2

    mod = _adaln_mod(c.reshape(d, 1), w_ada, b_ada.reshape(1, -1))
    mod3 = mod.reshape(3, d)

    o_q, o_ckv = 0, q_rank
    o_kr = o_ckv + kv_rank
    o_za = o_kr + QK_ROPE_DIM
    o_xbc = o_za + nv
    o_dt = o_xbc + nch
    o_zs = o_dt + nheads
    w_kr = w_in[:, o_kr:o_kr + QK_ROPE_DIM]
    w_kr_rot = jnp.concatenate([w_kr[:, half:], w_kr[:, :half]], axis=1)
    w_dt = w_in[:, o_dt:o_dt + nheads]
    pieces = [("q", w_in[:, o_q:o_ckv]), ("ckv", w_in[:, o_ckv:o_kr]),
              ("krope", jnp.concatenate([w_kr, w_kr_rot], axis=1)),
              ("za", w_in[:, o_za:o_xbc]), ("xbc", w_in[:, o_xbc:o_dt]),
              ("zs", w_in[:, o_zs:o_zs + nx]),
              ("dt", _pad_cols(jnp.concatenate([w_dt, w_dt], axis=1), LANES))]
    off, cur = {}, 0
    for name, p in pieces:
        assert cur % LANES == 0 and p.shape[1] % LANES == 0
        off[name] = cur
        cur += p.shape[1]
    w_in_p = jnp.concatenate([p for _, p in pieces], axis=1).astype(BF16)

    w3 = w_qb.reshape(q_rank, MLA_HEADS, QK_NOPE_DIM + QK_ROPE_DIM)
    w_rope = w3[:, :, QK_NOPE_DIM:]
    w_qb_p = jnp.concatenate([w3, w_rope[:, :, half:], w_rope[:, :, :half]], axis=2)
    w_qb_p = w_qb_p.reshape(q_rank, MLA_HEADS * QK_PAD).astype(BF16)

    inv_freq = 1.0 / (ROPE_THETA ** (jnp.arange(half, dtype=F32) / half))
    freq = jnp.tile(inv_freq, LANES // half)
    phase = jnp.concatenate([jnp.zeros((2 * half,), F32), jnp.full((half,), 0.5 * math.pi, F32),
                             jnp.full((half,), -0.5 * math.pi, F32)])
    rope_tab = jnp.stack([freq, phase])

    dtb = _pad_cols(jnp.concatenate([dt_bias, dt_bias]).reshape(1, -1), LANES)
    q_scale = (QK_NOPE_DIM + QK_ROPE_DIM) ** -0.5 * LOG2E

    q, k, v, gza, xs, bm, cm, dt, gzs = _in_proj(
        x2, mod3, pos_col, rope_tab, w_in_p, q_norm_g.reshape(1, -1), w_qb_p,
        kv_norm_g.reshape(1, -1), w_kvb.astype(BF16), conv_w, conv_b.reshape(1, -1), dtb,
        off=off, tm=256, q_scale=q_scale)

    o_attn = _attention(q, k, v, tq=512)

    alog = _pad_cols(jnp.concatenate([a_log, a_log]).reshape(1, -1), LANES)
    e_head = jnp.repeat(jnp.eye(nheads, dtype=BF16), SSM_HEAD_DIM, axis=1)
    e2 = jnp.pad(jnp.concatenate([e_head, e_head], axis=0), ((0, LANES - 2 * nheads), (0, 0)))
    dskip_x = jnp.repeat(d_skip, SSM_HEAD_DIM).reshape(1, -1)
    o_ssm = _ssd(xs, bm, cm, dt, gzs, alog, dskip_x, ssm_norm_g.reshape(1, -1), e2,
                 rows=512, nheads=nheads)

    alpha = (2.0 * depth) ** 0.25
    return _out_proj(o_attn, gza, o_ssm, x2, w_out.astype(BF16), mod3,
                     ln_g.reshape(1, -1), ln_b.reshape(1, -1), tm=512, alpha=alpha)


def kernel(x, c, positions, w_ada, b_ada, w_in, q_norm_g, w_qb, kv_norm_g, w_kvb, conv_w, conv_b,
           dt_bias, a_log, d_skip, ssm_norm_g, w_out, ln_g, ln_b):
    b, s, d = x.shape
    depth = w_in.shape[0]
    assert b == 1, "one sequence per call"
    h = x.reshape(s, d)
    pos_col = positions.reshape(s, 1)
    for l in range(depth):
        h = _layer(h, c, pos_col, w_ada[l], b_ada[l], w_in[l], q_norm_g[l], w_qb[l], kv_norm_g[l],
                   w_kvb[l], conv_w[l], conv_b[l], dt_bias[l], a_log[l], d_skip[l], ssm_norm_g[l],
                   w_out[l], ln_g[l], ln_b[l], depth=depth)
    return h.reshape(b, s, d)
```

```python
import functools
import math

import jax
import jax.numpy as jnp
from jax import lax
from jax.experimental import pallas as pl
from jax.experimental.pallas import tpu as pltpu

F32 = jnp.float32
BF16 = jnp.bfloat16

MLA_HEADS = 8
QK_NOPE_DIM = 128
QK_ROPE_DIM = 64
V_HEAD_DIM = 128
ROPE_THETA = 10000.0
SSM_HEAD_DIM = 64
SSM_GROUPS = 2
SSM_STATE = 128
CHUNK = 128
RMS_EPS = 1e-6
LN_EPS = 1e-5

LANES = 128
SUBLANES = 8
MXU_DIM = 256
QK_PAD = MXU_DIM
VMEM_LIMIT = 56 * 1024 * 1024

LOG2E = 1.4426950408889634
NEG_BIG = -1e30


def _silu(z):
    return z * (1.0 / (1.0 + jnp.exp(-z)))


def _softplus(z):
    return jnp.maximum(z, 0.0) + jnp.log1p(jnp.exp(-jnp.abs(z)))


def _split_bf16(x, parts):
    out, rem = [], x
    for _ in range(parts):
        hi = rem.astype(BF16)
        out.append(hi)
        rem = rem - hi.astype(F32)
    return out


def _mod_kernel(c_ref, w_ref, b_ref, o_ref):
    o_ref[...] = jnp.sum(w_ref[...] * c_ref[...], axis=0, keepdims=True) + b_ref[...]


def _adaln_mod(c_col, w_ada, b_ada):
    d, n = w_ada.shape
    bn = 1024
    return pl.pallas_call(
        _mod_kernel,
        grid=(n // bn,),
        in_specs=[pl.BlockSpec((d, 1), lambda j: (0, 0)),
                  pl.BlockSpec((d, bn), lambda j: (0, j)),
                  pl.BlockSpec((1, bn), lambda j: (0, j))],
        out_specs=pl.BlockSpec((1, bn), lambda j: (0, j)),
        out_shape=jax.ShapeDtypeStruct((1, n), F32),
        name="adaln_mod",
    )(c_col, w_ada, b_ada)


def _in_kernel(x_ref, mod_ref, pos_ref, rope_ref, w_in_ref, qg_ref, w_qb_ref, kvg_ref, w_k_ref, w_vt_ref,
               conv_w_ref, conv_b_ref, dtb_ref,
               q_out, k_out, vt_out, gza_out, xs_out, bm_out, cm_out, dt_out, gzs_out,
               xbuf, *, tm, off, q_scale):
    i = pl.program_id(0)
    shift = mod_ref[0:1, :]
    scale = mod_ref[1:2, :]
    u = x_ref[...] * (1.0 + scale) + shift
    proj = jnp.dot(u.astype(BF16), w_in_ref[...], preferred_element_type=F32)

    ang = pos_ref[...].astype(F32) * rope_ref[0:1, :] + rope_ref[1:2, :]
    cs = jnp.cos(ang)
    lane = lax.broadcasted_iota(jnp.int32, (tm, LANES), 1)
    low_half = lane < QK_ROPE_DIM

    def rope(rr):
        t = rr * cs
        return jnp.where(low_half, t + pltpu.roll(t, QK_ROPE_DIM, axis=1), 0.0)

    q_lat = proj[:, off["q"]:off["q"] + qg_ref.shape[1]]
    qn = q_lat * lax.rsqrt(jnp.mean(q_lat * q_lat, axis=-1, keepdims=True) + RMS_EPS) * qg_ref[...]
    qf = jnp.dot(qn.astype(BF16), w_qb_ref[...], preferred_element_type=F32)
    for h in range(MLA_HEADS):
        c0 = h * QK_PAD
        q_out[:, c0:c0 + QK_NOPE_DIM] = (qf[:, c0:c0 + QK_NOPE_DIM] * q_scale).astype(BF16)
        q_out[:, c0 + QK_NOPE_DIM:c0 + QK_PAD] = (rope(qf[:, c0 + QK_NOPE_DIM:c0 + QK_PAD]) * q_scale).astype(BF16)

    c_kv = proj[:, off["ckv"]:off["ckv"] + kvg_ref.shape[1]]
    ckvn = c_kv * lax.rsqrt(jnp.mean(c_kv * c_kv, axis=-1, keepdims=True) + RMS_EPS) * kvg_ref[...]
    ckvn_bf = ckvn.astype(BF16)
    kf = jnp.dot(ckvn_bf, w_k_ref[...], preferred_element_type=F32)
    vt_out[...] = lax.dot_general(w_vt_ref[...], ckvn_bf, (((1,), (1,)), ((), ())),
                                  preferred_element_type=F32).astype(BF16)
    k_rope = rope(proj[:, off["krope"]:off["krope"] + LANES]).astype(BF16)
    for h in range(MLA_HEADS):
        k_out[:, h * QK_PAD:h * QK_PAD + QK_NOPE_DIM] = kf[:, h * QK_NOPE_DIM:(h + 1) * QK_NOPE_DIM].astype(BF16)
        k_out[:, h * QK_PAD + QK_NOPE_DIM:(h + 1) * QK_PAD] = k_rope

    nz = gza_out.shape[1]
    gza_out[...] = _silu(proj[:, off["za"]:off["za"] + nz]).astype(BF16)
    nzs = gzs_out.shape[1]
    gzs_out[...] = _silu(proj[:, off["zs"]:off["zs"] + nzs]).astype(BF16)
    dt_out[...] = _softplus(proj[:, off["dt"]:off["dt"] + LANES] + dtb_ref[...])

    nconv = conv_w_ref.shape[0]
    nch = conv_w_ref.shape[1]

    @pl.when(i == 0)
    def _():
        xbuf[0:SUBLANES, :] = jnp.zeros((SUBLANES, nch), F32)

    xbuf[SUBLANES:SUBLANES + tm, :] = proj[:, off["xbc"]:off["xbc"] + nch]
    acc = conv_b_ref[...] + conv_w_ref[nconv - 1:nconv, :] * xbuf[SUBLANES:SUBLANES + tm, :]
    for j in range(nconv - 1):
        r0 = SUBLANES - (nconv - 1) + j
        acc = acc + conv_w_ref[j:j + 1, :] * xbuf[r0:r0 + tm, :]
    xbuf[0:SUBLANES, :] = xbuf[tm:tm + SUBLANES, :]
    xc = _silu(acc)
    nx = xs_out.shape[1]
    nb = bm_out.shape[1]
    xs_out[...] = xc[:, :nx].astype(BF16)
    bm_out[...] = xc[:, nx:nx + nb].astype(BF16)
    cm_out[...] = xc[:, nx + nb:nx + 2 * nb].astype(BF16)


def _in_proj(x2, mod3, pos_col, rope_tab, w_in_p, qg, w_qb_p, kvg, w_k, w_vt, conv_w, conv_b, dtb,
             *, off, tm, q_scale):
    s, d = x2.shape
    nq = MLA_HEADS * QK_PAD
    nv = MLA_HEADS * V_HEAD_DIM
    nch = conv_w.shape[1]
    nbc = SSM_GROUPS * SSM_STATE
    nx = nch - 2 * nbc
    row = lambda w: pl.BlockSpec((tm, w), lambda i: (i, 0))
    full = lambda a: pl.BlockSpec(a.shape, lambda i: (0,) * a.ndim, pipeline_mode=pl.Buffered(1))
    outs = [
        jax.ShapeDtypeStruct((s, nq), BF16),
        jax.ShapeDtypeStruct((s, nq), BF16),
        jax.ShapeDtypeStruct((nv, s), BF16),
        jax.ShapeDtypeStruct((s, nv), BF16),
        jax.ShapeDtypeStruct((s, nx), BF16),
        jax.ShapeDtypeStruct((s, nbc), BF16),
        jax.ShapeDtypeStruct((s, nbc), BF16),
        jax.ShapeDtypeStruct((s, LANES), F32),
        jax.ShapeDtypeStruct((s, nx), BF16),
    ]
    return pl.pallas_call(
        functools.partial(_in_kernel, tm=tm, off=off, q_scale=q_scale),
        grid=(s // tm,),
        in_specs=[row(d), full(mod3), pl.BlockSpec((tm, 1), lambda i: (i, 0)), full(rope_tab),
                  full(w_in_p), full(qg), full(w_qb_p), full(kvg), full(w_k), full(w_vt),
                  full(conv_w), full(conv_b), full(dtb)],
        out_specs=[pl.BlockSpec((nv, tm), lambda i: (0, i)) if n == 2 else row(o.shape[1])
                   for n, o in enumerate(outs)],
        out_shape=outs,
        scratch_shapes=[pltpu.VMEM((tm + 2 * SUBLANES, nch), F32)],
        compiler_params=pltpu.CompilerParams(dimension_semantics=("arbitrary",),
                                             vmem_limit_bytes=VMEM_LIMIT),
        name="in_proj",
    )(x2, mod3, pos_col, rope_tab, w_in_p, qg, w_qb_p, kvg, w_k, w_vt, conv_w, conv_b, dtb)


def _sublane_allmax(x):
    shift = SUBLANES // 2
    while shift:
        x = jnp.maximum(x, pltpu.roll(x, shift, axis=0))
        shift //= 2
    return x


def _sublane_allsum(x):
    shift = SUBLANES // 2
    while shift:
        x = x + pltpu.roll(x, shift, axis=0)
        shift //= 2
    return x


def _attn_kernel(q_ref, k_ref, vt_ref, o_ref, m_scr, l_scr, acc_scr, *, tq, tk):
    qi = pl.program_id(1)
    q = q_ref[...]
    m_scr[...] = jnp.full(m_scr.shape, NEG_BIG, F32)
    l_scr[...] = jnp.zeros(l_scr.shape, F32)
    acc_scr[...] = jnp.zeros(acc_scr.shape, F32)

    def tile(j, masked):
        start = pl.multiple_of(j * tk, tk)
        k = k_ref[pl.ds(start, tk), :]
        vt = vt_ref[:, pl.ds(start, tk)]
        st = lax.dot_general(k, q, (((1,), (1,)), ((), ())), preferred_element_type=F32)
        if masked:
            key = lax.broadcasted_iota(jnp.int32, (tk, tq), 0) + (j * tk - qi * tq)
            qry = lax.broadcasted_iota(jnp.int32, (tk, tq), 1)
            st = jnp.where(key <= qry, st, NEG_BIG)
        st3 = st.reshape(tk // SUBLANES, SUBLANES, tq)
        m_old = m_scr[...]
        m_new = jnp.maximum(m_old, _sublane_allmax(jnp.max(st3, axis=0)))
        alpha = jnp.exp2(m_old - m_new)
        pt3 = jnp.exp2(st3 - m_new[None])
        l_scr[...] = alpha * l_scr[...] + jnp.sum(pt3, axis=0)
        pv = jnp.dot(vt, pt3.reshape(tk, tq).astype(BF16), preferred_element_type=F32)
        acc3 = acc_scr[...].reshape(V_HEAD_DIM // SUBLANES, SUBLANES, tq) * alpha[None]
        acc_scr[...] = acc3.reshape(V_HEAD_DIM, tq) + pv
        m_scr[...] = m_new

    per_q = tq // tk

    def body(jj, carry):
        for d in range(per_q):
            tile(jj * per_q + d, False)
        return carry

    lax.fori_loop(0, qi, body, 0)
    for d in range(per_q):
        tile(qi * per_q + d, True)
    inv_l = 1.0 / _sublane_allsum(l_scr[...])
    o_t = (acc_scr[...].reshape(V_HEAD_DIM // SUBLANES, SUBLANES, tq) * inv_l[None]).reshape(V_HEAD_DIM, tq)
    o_ref[...] = o_t.T.astype(o_ref.dtype)


def _attention(q, k, vt, *, tq, tk):
    s = q.shape[0]
    return pl.pallas_call(
        functools.partial(_attn_kernel, tq=tq, tk=tk),
        grid=(MLA_HEADS, s // tq),
        in_specs=[pl.BlockSpec((tq, QK_PAD), lambda h, i: (i, h)),
                  pl.BlockSpec((s, QK_PAD), lambda h, i: (0, h)),
                  pl.BlockSpec((V_HEAD_DIM, s), lambda h, i: (h, 0))],
        out_specs=pl.BlockSpec((tq, V_HEAD_DIM), lambda h, i: (i, h)),
        out_shape=jax.ShapeDtypeStruct((s, MLA_HEADS * V_HEAD_DIM), BF16),
        scratch_shapes=[pltpu.VMEM((SUBLANES, tq), F32), pltpu.VMEM((SUBLANES, tq), F32),
                        pltpu.VMEM((V_HEAD_DIM, tq), F32)],
        compiler_params=pltpu.CompilerParams(dimension_semantics=("arbitrary", "arbitrary"),
                                             vmem_limit_bytes=VMEM_LIMIT),
        name="attention",
    )(q, k, vt)


def _ssd_kernel(xs_ref, bm_ref, cm_ref, dt_ref, gz_ref, alog_ref, dskip_ref, g_ref, e2_ref,
                o_ref, state, y_scr, *, rows, nheads):
    i = pl.program_id(0)
    gw = SSM_STATE
    hw = state.shape[2]

    @pl.when(i == 0)
    def _():
        state[...] = jnp.zeros(state.shape, F32)

    a_neg = -jnp.exp(alog_ref[...])
    r_i = lax.broadcasted_iota(jnp.int32, (CHUNK, CHUNK), 0)
    c_i = lax.broadcasted_iota(jnp.int32, (CHUNK, CHUNK), 1)
    tri = c_i <= r_i
    tri_bf = tri.astype(BF16)
    lane = lax.broadcasted_iota(jnp.int32, (CHUNK, LANES), 1)
    first_copy = lane < nheads
    head_lo = lane < SSM_HEAD_DIM
    e2 = e2_ref[...]

    def expand(v):
        hi = v.astype(BF16)
        lo = (v - hi.astype(F32)).astype(BF16)
        return jnp.dot(jnp.where(first_copy, hi, lo), e2, preferred_element_type=F32)

    for cidx in range(rows // CHUNK):
        sl = pl.ds(cidx * CHUNK, CHUNK)
        dt = dt_ref[sl, :]
        da = dt * a_neg
        a_cum = jnp.zeros((CHUNK, LANES), F32)
        for part in _split_bf16(da, 3):
            a_cum = a_cum + jnp.dot(tri_bf, part, preferred_element_type=F32)
        a_cum_t = a_cum.T
        a_last = a_cum[CHUNK - 1:CHUNK, :]
        ea = jnp.exp(a_cum)
        dte = jnp.exp(a_last - a_cum)

        xs = xs_ref[sl, :].astype(F32)
        xd = xs * expand(dt)
        ea_x = expand(ea)
        xdd = (xd * expand(dte)).astype(BF16)
        xd_bf = xd.astype(BF16)

        for g in range(SSM_GROUPS):
            bg = bm_ref[sl, g * gw:(g + 1) * gw]
            cg = cm_ref[sl, g * gw:(g + 1) * gw]
            cb = lax.dot_general(cg, bg, (((1,), (1,)), ((), ())), preferred_element_type=F32)
            prev = state[g]
            y_off = jnp.dot(cg, prev.astype(BF16), preferred_element_type=F32)
            new = lax.dot_general(bg, xdd[:, g * hw:(g + 1) * hw], (((0,), (0,)), ((), ())),
                                  preferred_element_type=F32)
            state[g] = prev * ea_x[CHUNK - 1:CHUNK, g * hw:(g + 1) * hw] + new
            y_scr[:, g * hw:(g + 1) * hw] = y_off * ea_x[:, g * hw:(g + 1) * hw]

            hpg = hw // SSM_HEAD_DIM
            for pair in range(hpg // 2):
                h0 = g * hpg + 2 * pair
                c0 = h0 * SSM_HEAD_DIM
                xp = xd_bf[:, c0:c0 + LANES]
                yp = jnp.zeros((CHUNK, LANES), F32)
                for k, keep in ((0, head_lo), (1, ~head_lo)):
                    h = h0 + k
                    seg = a_cum[:, h:h + 1] - a_cum_t[h:h + 1, :]
                    m_h = (cb * jnp.exp(jnp.where(tri, seg, NEG_BIG))).astype(BF16)
                    yp = yp + jnp.dot(m_h, jnp.where(keep, xp, jnp.zeros_like(xp)),
                                      preferred_element_type=F32)
                y_scr[:, c0:c0 + LANES] = y_scr[:, c0:c0 + LANES] + yp

        y = y_scr[...] + xs * dskip_ref[...]
        hf = y * gz_ref[sl, :].astype(F32)
        for g in range(SSM_GROUPS):
            hg = hf[:, g * hw:(g + 1) * hw]
            ms = jnp.mean(hg * hg, axis=-1, keepdims=True)
            o_ref[sl, g * hw:(g + 1) * hw] = (hg * lax.rsqrt(ms + RMS_EPS)
                                               * g_ref[:, g * hw:(g + 1) * hw]).astype(o_ref.dtype)


def _ssd(xs, bm, cm, dt, gzs, alog, dskip_x, norm_g, e2, *, rows, nheads):
    s, nx = xs.shape
    hw = nx // SSM_GROUPS
    row = lambda a: pl.BlockSpec((rows, a.shape[1]), lambda i: (i, 0))
    full = lambda a: pl.BlockSpec(a.shape, lambda i: (0,) * a.ndim)
    return pl.pallas_call(
        functools.partial(_ssd_kernel, rows=rows, nheads=nheads),
        grid=(s // rows,),
        in_specs=[row(xs), row(bm), row(cm), row(dt), row(gzs),
                  full(alog), full(dskip_x), full(norm_g), full(e2)],
        out_specs=pl.BlockSpec((rows, nx), lambda i: (i, 0)),
        out_shape=jax.ShapeDtypeStruct((s, nx), BF16),
        scratch_shapes=[pltpu.VMEM((SSM_GROUPS, SSM_STATE, hw), F32),
                        pltpu.VMEM((CHUNK, nx), F32)],
        compiler_params=pltpu.CompilerParams(dimension_semantics=("arbitrary",),
                                             vmem_limit_bytes=VMEM_LIMIT),
        name="ssd",
    )(xs, bm, cm, dt, gzs, alog, dskip_x, norm_g, e2)


def _out_kernel(oa_ref, gza_ref, os_ref, x_ref, w_ref, mod_ref, lng_ref, lnb_ref, o_ref, *, alpha):
    na = oa_ref.shape[1]
    oa = (oa_ref[...].astype(F32) * gza_ref[...].astype(F32)).astype(BF16)
    mixed = jnp.dot(oa, w_ref[0:na, :], preferred_element_type=F32)
    mixed = mixed + jnp.dot(os_ref[...], w_ref[na:, :], preferred_element_type=F32)
    y = alpha * x_ref[...] + mod_ref[2:3, :] * mixed
    mu = jnp.mean(y, axis=-1, keepdims=True)
    yc = y - mu
    var = jnp.mean(yc * yc, axis=-1, keepdims=True)
    o_ref[...] = yc * lax.rsqrt(var + LN_EPS) * lng_ref[...] + lnb_ref[...]


def _out_proj(oa, gza, o_ssm, x2, w_out_bf, mod3, ln_g, ln_b, *, tm, alpha):
    s, d = x2.shape
    row = lambda a: pl.BlockSpec((tm, a.shape[1]), lambda i: (i, 0))
    full = lambda a: pl.BlockSpec(a.shape, lambda i: (0,) * a.ndim)
    return pl.pallas_call(
        functools.partial(_out_kernel, alpha=alpha),
        grid=(s // tm,),
        in_specs=[row(oa), row(gza), row(o_ssm), row(x2), full(w_out_bf), full(mod3),
                  full(ln_g), full(ln_b)],
        out_specs=pl.BlockSpec((tm, d), lambda i: (i, 0)),
        out_shape=jax.ShapeDtypeStruct((s, d), F32),
        compiler_params=pltpu.CompilerParams(dimension_semantics=("arbitrary",),
                                             vmem_limit_bytes=VMEM_LIMIT),
        name="out_proj",
    )(oa, gza, o_ssm, x2, w_out_bf, mod3, ln_g, ln_b)


def _pad_cols(a, width):
    return jnp.pad(a, ((0, 0), (0, width - a.shape[1])))


def _layer(x2, c, pos_col, w_ada, b_ada, w_in, q_norm_g, w_qb, kv_norm_g, w_kvb,
           conv_w, conv_b, dt_bias, a_log, d_skip, ssm_norm_g, w_out, ln_g, ln_b, *, depth):
    s, d = x2.shape
    q_rank = q_norm_g.shape[0]
    kv_rank = kv_norm_g.shape[0]
    nheads = dt_bias.shape[0]
    nch = conv_w.shape[1]
    nv = MLA_HEADS * V_HEAD_DIM
    nx = nheads * SSM_HEAD_DIM
    half = QK_ROPE_DIM // 2

    mod = _adaln_mod(c.reshape(d, 1), w_ada, b_ada.reshape(1, -1))
    mod3 = mod.reshape(3, d)

    o_q, o_ckv = 0, q_rank
    o_kr = o_ckv + kv_rank
    o_za = o_kr + QK_ROPE_DIM
    o_xbc = o_za + nv
    o_dt = o_xbc + nch
    o_zs = o_dt + nheads
    w_kr = w_in[:, o_kr:o_kr + QK_ROPE_DIM]
    w_kr_rot = jnp.concatenate([w_kr[:, half:], w_kr[:, :half]], axis=1)
    w_dt = w_in[:, o_dt:o_dt + nheads]
    pieces = [("q", w_in[:, o_q:o_ckv]), ("ckv", w_in[:, o_ckv:o_kr]),
              ("krope", jnp.concatenate([w_kr, w_kr_rot], axis=1)),
              ("za", w_in[:, o_za:o_xbc]), ("xbc", w_in[:, o_xbc:o_dt]),
              ("zs", w_in[:, o_zs:o_zs + nx]),
              ("dt", _pad_cols(jnp.concatenate([w_dt, w_dt], axis=1), LANES))]
    off, cur = {}, 0
    for name, p in pieces:
        assert cur % LANES == 0 and p.shape[1] % LANES == 0
        off[name] = cur
        cur += p.shape[1]
    w_in_p = jnp.concatenate([p for _, p in pieces], axis=1).astype(BF16)

    w3 = w_qb.reshape(q_rank, MLA_HEADS, QK_NOPE_DIM + QK_ROPE_DIM)
    w_rope = w3[:, :, QK_NOPE_DIM:]
    w_qb_p = jnp.concatenate([w3, w_rope[:, :, half:], w_rope[:, :, :half]], axis=2)
    w_qb_p = w_qb_p.reshape(q_rank, MLA_HEADS * QK_PAD).astype(BF16)

    inv_freq = 1.0 / (ROPE_THETA ** (jnp.arange(half, dtype=F32) / half))
    freq = jnp.tile(inv_freq, LANES // half)
    phase = jnp.concatenate([jnp.zeros((2 * half,), F32), jnp.full((half,), 0.5 * math.pi, F32),
                             jnp.full((half,), -0.5 * math.pi, F32)])
    rope_tab = jnp.stack([freq, phase])

    dtb = _pad_cols(jnp.concatenate([dt_bias, dt_bias]).reshape(1, -1), LANES)
    q_scale = (QK_NOPE_DIM + QK_ROPE_DIM) ** -0.5 * LOG2E

    wkv3 = w_kvb.reshape(kv_rank, MLA_HEADS, QK_NOPE_DIM + V_HEAD_DIM)
    w_k = wkv3[:, :, :QK_NOPE_DIM].reshape(kv_rank, MLA_HEADS * QK_NOPE_DIM).astype(BF16)
    w_vt = wkv3[:, :, QK_NOPE_DIM:].reshape(kv_rank, nv).T.astype(BF16)

    q, k, vt, gza, xs, bm, cm, dt, gzs = _in_proj(
        x2, mod3, pos_col, rope_tab, w_in_p, q_norm_g.reshape(1, -1), w_qb_p,
        kv_norm_g.reshape(1, -1), w_k, w_vt, conv_w, conv_b.reshape(1, -1), dtb,
        off=off, tm=256, q_scale=q_scale)

    o_attn = _attention(q, k, vt, tq=1024, tk=512)

    alog = _pad_cols(jnp.concatenate([a_log, a_log]).reshape(1, -1), LANES)
    e_head = jnp.repeat(jnp.eye(nheads, dtype=BF16), SSM_HEAD_DIM, axis=1)
    e2 = jnp.pad(jnp.concatenate([e_head, e_head], axis=0), ((0, LANES - 2 * nheads), (0, 0)))
    dskip_x = jnp.repeat(d_skip, SSM_HEAD_DIM).reshape(1, -1)
    o_ssm = _ssd(xs, bm, cm, dt, gzs, alog, dskip_x, ssm_norm_g.reshape(1, -1), e2,
                 rows=512, nheads=nheads)

    alpha = (2.0 * depth) ** 0.25
    return _out_proj(o_attn, gza, o_ssm, x2, w_out.astype(BF16), mod3,
                     ln_g.reshape(1, -1), ln_b.reshape(1, -1), tm=512, alpha=alpha)


def kernel(x, c, positions, w_ada, b_ada, w_in, q_norm_g, w_qb, kv_norm_g, w_kvb, conv_w, conv_b,
           dt_bias, a_log, d_skip, ssm_norm_g, w_out, ln_g, ln_b):
    b, s, d = x.shape
    depth = w_in.shape[0]
    assert b == 1, "one sequence per call"
    h = x.reshape(s, d)
    pos_col = positions.reshape(s, 1)
    for l in range(depth):
        h = _layer(h, c, pos_col, w_ada[l], b_ada[l], w_in[l], q_norm_g[l], w_qb[l], kv_norm_g[l],
                   w_kvb[l], conv_w[l], conv_b[l], dt_bias[l], a_log[l], d_skip[l], ssm_norm_g[l],
                   w_out[l], ln_g[l], ln_b[l], depth=depth)
    return h.reshape(b, s, d)
```

```python
import functools
import math

import jax
import jax.numpy as jnp
from jax import lax
from jax.experimental import pallas as pl
from jax.experimental.pallas import tpu as pltpu

F32 = jnp.float32
BF16 = jnp.bfloat16

MLA_HEADS = 8
QK_NOPE_DIM = 128
QK_ROPE_DIM = 64
V_HEAD_DIM = 128
ROPE_THETA = 10000.0
SSM_HEAD_DIM = 64
SSM_GROUPS = 2
SSM_STATE = 128
CHUNK = 128
RMS_EPS = 1e-6
LN_EPS = 1e-5

LANES = 128
SUBLANES = 8
MXU_DIM = 256
QK_PAD = MXU_DIM
VMEM_LIMIT = 56 * 1024 * 1024

LOG2E = 1.4426950408889634
NEG_BIG = -1e30


def _silu(z):
    return z * (1.0 / (1.0 + jnp.exp(-z)))


def _softplus(z):
    return jnp.maximum(z, 0.0) + jnp.log1p(jnp.exp(-jnp.abs(z)))


def _split_bf16(x, parts):
    out, rem = [], x
    for _ in range(parts):
        hi = rem.astype(BF16)
        out.append(hi)
        rem = rem - hi.astype(F32)
    return out


def _mod_kernel(c_ref, w_ref, b_ref, o_ref):
    o_ref[...] = jnp.sum(w_ref[...] * c_ref[...], axis=0, keepdims=True) + b_ref[...]


def _adaln_mod(c_col, w_ada, b_ada):
    d, n = w_ada.shape
    bn = 1024
    return pl.pallas_call(
        _mod_kernel,
        grid=(n // bn,),
        in_specs=[pl.BlockSpec((d, 1), lambda j: (0, 0)),
                  pl.BlockSpec((d, bn), lambda j: (0, j)),
                  pl.BlockSpec((1, bn), lambda j: (0, j))],
        out_specs=pl.BlockSpec((1, bn), lambda j: (0, j)),
        out_shape=jax.ShapeDtypeStruct((1, n), F32),
        name="adaln_mod",
    )(c_col, w_ada, b_ada)


def _in_kernel(x_ref, mod_ref, pos_ref, rope_ref, w_in_ref, qg_ref, w_qb_ref, kvg_ref, w_k_ref, w_vt_ref,
               conv_w_ref, conv_b_ref, dtb_ref,
               q_out, k_out, vt_out, gza_out, xs_out, bm_out, cm_out, dt_out, gzs_out,
               xbuf, *, tm, off, q_scale):
    i = pl.program_id(0)
    shift = mod_ref[0:1, :]
    scale = mod_ref[1:2, :]
    u = x_ref[...] * (1.0 + scale) + shift
    proj = jnp.dot(u.astype(BF16), w_in_ref[...], preferred_element_type=F32)

    ang = pos_ref[...].astype(F32) * rope_ref[0:1, :] + rope_ref[1:2, :]
    cs = jnp.cos(ang)
    lane = lax.broadcasted_iota(jnp.int32, (tm, LANES), 1)
    low_half = lane < QK_ROPE_DIM

    def rope(rr):
        t = rr * cs
        return jnp.where(low_half, t + pltpu.roll(t, QK_ROPE_DIM, axis=1), 0.0)

    q_lat = proj[:, off["q"]:off["q"] + qg_ref.shape[1]]
    qn = q_lat * lax.rsqrt(jnp.mean(q_lat * q_lat, axis=-1, keepdims=True) + RMS_EPS) * qg_ref[...]
    qf = jnp.dot(qn.astype(BF16), w_qb_ref[...], preferred_element_type=F32)
    for h in range(MLA_HEADS):
        c0 = h * QK_PAD
        q_out[:, c0:c0 + QK_NOPE_DIM] = (qf[:, c0:c0 + QK_NOPE_DIM] * q_scale).astype(BF16)
        q_out[:, c0 + QK_NOPE_DIM:c0 + QK_PAD] = (rope(qf[:, c0 + QK_NOPE_DIM:c0 + QK_PAD]) * q_scale).astype(BF16)

    c_kv = proj[:, off["ckv"]:off["ckv"] + kvg_ref.shape[1]]
    ckvn = c_kv * lax.rsqrt(jnp.mean(c_kv * c_kv, axis=-1, keepdims=True) + RMS_EPS) * kvg_ref[...]
    ckvn_bf = ckvn.astype(BF16)
    kf = jnp.dot(ckvn_bf, w_k_ref[...], preferred_element_type=F32)
    vt_out[...] = lax.dot_general(w_vt_ref[...], ckvn_bf, (((1,), (1,)), ((), ())),
                                  preferred_element_type=F32).astype(BF16)
    k_rope = rope(proj[:, off["krope"]:off["krope"] + LANES]).astype(BF16)
    for h in range(MLA_HEADS):
        k_out[:, h * QK_PAD:h * QK_PAD + QK_NOPE_DIM] = kf[:, h * QK_NOPE_DIM:(h + 1) * QK_NOPE_DIM].astype(BF16)
        k_out[:, h * QK_PAD + QK_NOPE_DIM:(h + 1) * QK_PAD] = k_rope

    nz = gza_out.shape[1]
    gza_out[...] = _silu(proj[:, off["za"]:off["za"] + nz]).astype(BF16)
    nzs = gzs_out.shape[1]
    gzs_out[...] = _silu(proj[:, off["zs"]:off["zs"] + nzs]).astype(BF16)
    dt_out[...] = _softplus(proj[:, off["dt"]:off["dt"] + LANES] + dtb_ref[...])

    nconv = conv_w_ref.shape[0]
    nch = conv_w_ref.shape[1]

    @pl.when(i == 0)
    def _():
        xbuf[0:SUBLANES, :] = jnp.zeros((SUBLANES, nch), F32)

    xbuf[SUBLANES:SUBLANES + tm, :] = proj[:, off["xbc"]:off["xbc"] + nch]
    acc = conv_b_ref[...] + conv_w_ref[nconv - 1:nconv, :] * xbuf[SUBLANES:SUBLANES + tm, :]
    for j in range(nconv - 1):
        r0 = SUBLANES - (nconv - 1) + j
        acc = acc + conv_w_ref[j:j + 1, :] * xbuf[r0:r0 + tm, :]
    xbuf[0:SUBLANES, :] = xbuf[tm:tm + SUBLANES, :]
    xc = _silu(acc)
    nx = xs_out.shape[1]
    nb = bm_out.shape[1]
    xs_out[...] = xc[:, :nx].astype(BF16)
    bm_out[...] = xc[:, nx:nx + nb].astype(BF16)
    cm_out[...] = xc[:, nx + nb:nx + 2 * nb].astype(BF16)


def _in_proj(x2, mod3, pos_col, rope_tab, w_in_p, qg, w_qb_p, kvg, w_k, w_vt, conv_w, conv_b, dtb,
             *, off, tm, q_scale):
    s, d = x2.shape
    nq = MLA_HEADS * QK_PAD
    nv = MLA_HEADS * V_HEAD_DIM
    nch = conv_w.shape[1]
    nbc = SSM_GROUPS * SSM_STATE
    nx = nch - 2 * nbc
    row = lambda w: pl.BlockSpec((tm, w), lambda i: (i, 0))
    full = lambda a: pl.BlockSpec(a.shape, lambda i: (0,) * a.ndim, pipeline_mode=pl.Buffered(1))
    outs = [
        jax.ShapeDtypeStruct((s, nq), BF16),
        jax.ShapeDtypeStruct((s, nq), BF16),
        jax.ShapeDtypeStruct((nv, s), BF16),
        jax.ShapeDtypeStruct((s, nv), BF16),
        jax.ShapeDtypeStruct((s, nx), BF16),
        jax.ShapeDtypeStruct((s, nbc), BF16),
        jax.ShapeDtypeStruct((s, nbc), BF16),
        jax.ShapeDtypeStruct((s, LANES), F32),
        jax.ShapeDtypeStruct((s, nx), BF16),
    ]
    return pl.pallas_call(
        functools.partial(_in_kernel, tm=tm, off=off, q_scale=q_scale),
        grid=(s // tm,),
        in_specs=[row(d), full(mod3), pl.BlockSpec((tm, 1), lambda i: (i, 0)), full(rope_tab),
                  full(w_in_p), full(qg), full(w_qb_p), full(kvg), full(w_k), full(w_vt),
                  full(conv_w), full(conv_b), full(dtb)],
        out_specs=[pl.BlockSpec((nv, tm), lambda i: (0, i)) if n == 2 else row(o.shape[1])
                   for n, o in enumerate(outs)],
        out_shape=outs,
        scratch_shapes=[pltpu.VMEM((tm + 2 * SUBLANES, nch), F32)],
        compiler_params=pltpu.CompilerParams(dimension_semantics=("arbitrary",),
                                             vmem_limit_bytes=VMEM_LIMIT),
        name="in_proj",
    )(x2, mod3, pos_col, rope_tab, w_in_p, qg, w_qb_p, kvg, w_k, w_vt, conv_w, conv_b, dtb)


def _sublane_allmax(x):
    shift = SUBLANES // 2
    while shift:
        x = jnp.maximum(x, pltpu.roll(x, shift, axis=0))
        shift //= 2
    return x


def _sublane_allsum(x):
    shift = SUBLANES // 2
    while shift:
        x = x + pltpu.roll(x, shift, axis=0)
        shift //= 2
    return x


BF16_ROWS = 16
ACC_ROWS = V_HEAD_DIM + BF16_ROWS
EXP_ROWS = 32


def _attn_kernel(q_ref, k_ref, vt_ref, o_ref, m_scr, acc_scr,
                 s_a, s_b, cmax_a, cmax_b, p_a, p_b, al_a, al_b, *, tq, tk):
    qi = pl.program_id(1)
    nsub = tk // SUBLANES
    nacc = ACC_ROWS // SUBLANES

    m_scr[...] = jnp.full(m_scr.shape, NEG_BIG, F32)
    acc_scr[...] = jnp.zeros(acc_scr.shape, F32)
    s_b[...] = jnp.full(s_b.shape, 2.0 * NEG_BIG, F32)
    cmax_b[...] = jnp.full(cmax_b.shape, 2.0 * NEG_BIG, F32)
    for p_buf, al_buf in ((p_a, al_a), (p_b, al_b)):
        p_buf[...] = jnp.zeros(p_buf.shape, BF16)
        al_buf[...] = jnp.ones(al_buf.shape, F32)

    def qk_stage(t, s_out, cmax_out, diag):
        start = pl.multiple_of(t * tk, tk)
        k = k_ref[pl.ds(start, tk), :]
        st = lax.dot_general(k, q_ref[...], (((1,), (1,)), ((), ())),
                             preferred_element_type=F32)
        if diag is not None:
            key = lax.broadcasted_iota(jnp.int32, (tk, tq), 0) + diag * tk
            qry = lax.broadcasted_iota(jnp.int32, (tk, tq), 1)
            st = jnp.where(key <= qry, st, NEG_BIG)
        s_out[...] = st
        cmax_out[...] = _sublane_allmax(jnp.max(st.reshape(nsub, SUBLANES, tq), axis=0))

    def softmax_stage(s_in, cmax_in, p_out, al_out):
        m_old = m_scr[...]
        m_new = jnp.maximum(m_old, cmax_in[...])
        al_out[...] = jnp.exp2(m_old - m_new)
        m_scr[...] = m_new
        for r in range(0, tk, EXP_ROWS):
            sc = s_in[r:r + EXP_ROWS, :].reshape(EXP_ROWS // SUBLANES, SUBLANES, tq)
            p_out[r:r + EXP_ROWS, :] = jnp.exp2(sc - m_new[None]).reshape(EXP_ROWS, tq).astype(BF16)

    ones_rows = jnp.ones((ACC_ROWS - V_HEAD_DIM, tk), BF16)

    def pv_stage(t, p_in, al_in):
        start = pl.multiple_of(jnp.maximum(t, 0) * tk, tk)
        vt_aug = jnp.concatenate([vt_ref[:, pl.ds(start, tk)], ones_rows], axis=0)
        pv = jnp.dot(vt_aug, p_in[...], preferred_element_type=F32)
        acc3 = acc_scr[...].reshape(nacc, SUBLANES, tq) * al_in[...][None]
        acc_scr[...] = acc3.reshape(ACC_ROWS, tq) + pv

    def step_even(t, diag):
        qk_stage(t, s_a, cmax_a, diag)
        pv_stage(t - 2, p_a, al_a)
        softmax_stage(s_b, cmax_b, p_b, al_b)

    def step_odd(t, diag):
        qk_stage(t, s_b, cmax_b, diag)
        pv_stage(t - 2, p_b, al_b)
        softmax_stage(s_a, cmax_a, p_a, al_a)

    assert tq == 2 * tk

    def body(jj, carry):
        step_even(2 * jj, None)
        step_odd(2 * jj + 1, None)
        return carry

    lax.fori_loop(0, qi, body, 0)
    t0 = 2 * qi
    step_even(t0, 0)
    step_odd(t0 + 1, 1)
    pv_stage(t0, p_a, al_a)
    softmax_stage(s_b, cmax_b, p_b, al_b)
    pv_stage(t0 + 1, p_b, al_b)

    inv_l = 1.0 / acc_scr[V_HEAD_DIM:V_HEAD_DIM + SUBLANES, :]
    o_t = acc_scr[0:V_HEAD_DIM, :].reshape(V_HEAD_DIM // SUBLANES, SUBLANES, tq) * inv_l[None]
    o_ref[...] = o_t.reshape(V_HEAD_DIM, tq).T.astype(o_ref.dtype)


def _attention(q, k, vt, *, tq, tk):
    s = q.shape[0]
    stat = pltpu.VMEM((SUBLANES, tq), F32)
    return pl.pallas_call(
        functools.partial(_attn_kernel, tq=tq, tk=tk),
        grid=(MLA_HEADS, s // tq),
        in_specs=[pl.BlockSpec((tq, QK_PAD), lambda h, i: (i, h)),
                  pl.BlockSpec((s, QK_PAD), lambda h, i: (0, h)),
                  pl.BlockSpec((V_HEAD_DIM, s), lambda h, i: (h, 0))],
        out_specs=pl.BlockSpec((tq, V_HEAD_DIM), lambda h, i: (i, h)),
        out_shape=jax.ShapeDtypeStruct((s, MLA_HEADS * V_HEAD_DIM), BF16),
        scratch_shapes=[stat, pltpu.VMEM((ACC_ROWS, tq), F32),
                        pltpu.VMEM((tk, tq), F32), pltpu.VMEM((tk, tq), F32), stat, stat,
                        pltpu.VMEM((tk, tq), BF16), pltpu.VMEM((tk, tq), BF16), stat, stat],
        compiler_params=pltpu.CompilerParams(dimension_semantics=("arbitrary", "arbitrary"),
                                             vmem_limit_bytes=VMEM_LIMIT),
        name="attention",
    )(q, k, vt)


def _ssd_kernel(xs_ref, bm_ref, cm_ref, dt_ref, gz_ref, alog_ref, dskip_ref, g_ref, e2_ref,
                o_ref, state, y_scr, *, rows, nheads):
    i = pl.program_id(0)
    gw = SSM_STATE
    hw = state.shape[2]

    @pl.when(i == 0)
    def _():
        state[...] = jnp.zeros(state.shape, F32)

    a_neg = -jnp.exp(alog_ref[...])
    r_i = lax.broadcasted_iota(jnp.int32, (CHUNK, CHUNK), 0)
    c_i = lax.broadcasted_iota(jnp.int32, (CHUNK, CHUNK), 1)
    tri = c_i <= r_i
    tri_bf = tri.astype(BF16)
    lane = lax.broadcasted_iota(jnp.int32, (CHUNK, LANES), 1)
    first_copy = lane < nheads
    head_lo = lane < SSM_HEAD_DIM
    e2 = e2_ref[...]

    def expand(v):
        hi = v.astype(BF16)
        lo = (v - hi.astype(F32)).astype(BF16)
        return jnp.dot(jnp.where(first_copy, hi, lo), e2, preferred_element_type=F32)

    for cidx in range(rows // CHUNK):
        sl = pl.ds(cidx * CHUNK, CHUNK)
        dt = dt_ref[sl, :]
        da = dt * a_neg
        a_cum = jnp.zeros((CHUNK, LANES), F32)
        for part in _split_bf16(da, 3):
            a_cum = a_cum + jnp.dot(tri_bf, part, preferred_element_type=F32)
        a_cum_t = a_cum.T
        a_last = a_cum[CHUNK - 1:CHUNK, :]
        ea = jnp.exp(a_cum)
        dte = jnp.exp(a_last - a_cum)

        xs = xs_ref[sl, :].astype(F32)
        xd = xs * expand(dt)
        ea_x = expand(ea)
        xdd = (xd * expand(dte)).astype(BF16)
        xd_bf = xd.astype(BF16)

        for g in range(SSM_GROUPS):
            bg = bm_ref[sl, g * gw:(g + 1) * gw]
            cg = cm_ref[sl, g * gw:(g + 1) * gw]
            cb = lax.dot_general(cg, bg, (((1,), (1,)), ((), ())), preferred_element_type=F32)
            prev = state[g]
            y_off = jnp.dot(cg, prev.astype(BF16), preferred_element_type=F32)
            new = lax.dot_general(bg, xdd[:, g * hw:(g + 1) * hw], (((0,), (0,)), ((), ())),
                                  preferred_element_type=F32)
            state[g] = prev * ea_x[CHUNK - 1:CHUNK, g * hw:(g + 1) * hw] + new
            y_scr[:, g * hw:(g + 1) * hw] = y_off * ea_x[:, g * hw:(g + 1) * hw]

            hpg = hw // SSM_HEAD_DIM
            for pair in range(hpg // 2):
                h0 = g * hpg + 2 * pair
                c0 = h0 * SSM_HEAD_DIM
                xp = xd_bf[:, c0:c0 + LANES]
                yp = jnp.zeros((CHUNK, LANES), F32)
                for k, keep in ((0, head_lo), (1, ~head_lo)):
                    h = h0 + k
                    seg = a_cum[:, h:h + 1] - a_cum_t[h:h + 1, :]
                    m_h = (cb * jnp.exp(jnp.where(tri, seg, NEG_BIG))).astype(BF16)
                    yp = yp + jnp.dot(m_h, jnp.where(keep, xp, jnp.zeros_like(xp)),
                                      preferred_element_type=F32)
                y_scr[:, c0:c0 + LANES] = y_scr[:, c0:c0 + LANES] + yp

        y = y_scr[...] + xs * dskip_ref[...]
        hf = y * gz_ref[sl, :].astype(F32)
        for g in range(SSM_GROUPS):
            hg = hf[:, g * hw:(g + 1) * hw]
            ms = jnp.mean(hg * hg, axis=-1, keepdims=True)
            o_ref[sl, g * hw:(g + 1) * hw] = (hg * lax.rsqrt(ms + RMS_EPS)
                                               * g_ref[:, g * hw:(g + 1) * hw]).astype(o_ref.dtype)


def _ssd(xs, bm, cm, dt, gzs, alog, dskip_x, norm_g, e2, *, rows, nheads):
    s, nx = xs.shape
    hw = nx // SSM_GROUPS
    row = lambda a: pl.BlockSpec((rows, a.shape[1]), lambda i: (i, 0))
    full = lambda a: pl.BlockSpec(a.shape, lambda i: (0,) * a.ndim)
    return pl.pallas_call(
        functools.partial(_ssd_kernel, rows=rows, nheads=nheads),
        grid=(s // rows,),
        in_specs=[row(xs), row(bm), row(cm), row(dt), row(gzs),
                  full(alog), full(dskip_x), full(norm_g), full(e2)],
        out_specs=pl.BlockSpec((rows, nx), lambda i: (i, 0)),
        out_shape=jax.ShapeDtypeStruct((s, nx), BF16),
        scratch_shapes=[pltpu.VMEM((SSM_GROUPS, SSM_STATE, hw), F32),
                        pltpu.VMEM((CHUNK, nx), F32)],
        compiler_params=pltpu.CompilerParams(dimension_semantics=("arbitrary",),
                                             vmem_limit_bytes=VMEM_LIMIT),
        name="ssd",
    )(xs, bm, cm, dt, gzs, alog, dskip_x, norm_g, e2)


def _out_kernel(oa_ref, gza_ref, os_ref, x_ref, w_ref, mod_ref, lng_ref, lnb_ref, o_ref, *, alpha):
    na = oa_ref.shape[1]
    oa = (oa_ref[...].astype(F32) * gza_ref[...].astype(F32)).astype(BF16)
    mixed = jnp.dot(oa, w_ref[0:na, :], preferred_element_type=F32)
    mixed = mixed + jnp.dot(os_ref[...], w_ref[na:, :], preferred_element_type=F32)
    y = alpha * x_ref[...] + mod_ref[2:3, :] * mixed
    mu = jnp.mean(y, axis=-1, keepdims=True)
    yc = y - mu
    var = jnp.mean(yc * yc, axis=-1, keepdims=True)
    o_ref[...] = yc * lax.rsqrt(var + LN_EPS) * lng_ref[...] + lnb_ref[...]


def _out_proj(oa, gza, o_ssm, x2, w_out_bf, mod3, ln_g, ln_b, *, tm, alpha):
    s, d = x2.shape
    row = lambda a: pl.BlockSpec((tm, a.shape[1]), lambda i: (i, 0))
    full = lambda a: pl.BlockSpec(a.shape, lambda i: (0,) * a.ndim)
    return pl.pallas_call(
        functools.partial(_out_kernel, alpha=alpha),
        grid=(s // tm,),
        in_specs=[row(oa), row(gza), row(o_ssm), row(x2), full(w_out_bf), full(mod3),
                  full(ln_g), full(ln_b)],
        out_specs=pl.BlockSpec((tm, d), lambda i: (i, 0)),
        out_shape=jax.ShapeDtypeStruct((s, d), F32),
        compiler_params=pltpu.CompilerParams(dimension_semantics=("arbitrary",),
                                             vmem_limit_bytes=VMEM_LIMIT),
        name="out_proj",
    )(oa, gza, o_ssm, x2, w_out_bf, mod3, ln_g, ln_b)


def _pad_cols(a, width):
    return jnp.pad(a, ((0, 0), (0, width - a.shape[1])))


def _layer(x2, c, pos_col, w_ada, b_ada, w_in, q_norm_g, w_qb, kv_norm_g, w_kvb,
           conv_w, conv_b, dt_bias, a_log, d_skip, ssm_norm_g, w_out, ln_g, ln_b, *, depth):
    s, d = x2.shape
    q_rank = q_norm_g.shape[0]
    kv_rank = kv_norm_g.shape[0]
    nheads = dt_bias.shape[0]
    nch = conv_w.shape[1]
    nv = MLA_HEADS * V_HEAD_DIM
    nx = nheads * SSM_HEAD_DIM
    half = QK_ROPE_DIM // 2

    mod = _adaln_mod(c.reshape(d, 1), w_ada, b_ada.reshape(1, -1))
    mod3 = mod.reshape(3, d)

    o_q, o_ckv = 0, q_rank
    o_kr = o_ckv + kv_rank
    o_za = o_kr + QK_ROPE_DIM
    o_xbc = o_za + nv
    o_dt = o_xbc + nch
    o_zs = o_dt + nheads
    w_kr = w_in[:, o_kr:o_kr + QK_ROPE_DIM]
    w_kr_rot = jnp.concatenate([w_kr[:, half:], w_kr[:, :half]], axis=1)
    w_dt = w_in[:, o_dt:o_dt + nheads]
    pieces = [("q", w_in[:, o_q:o_ckv]), ("ckv", w_in[:, o_ckv:o_kr]),
              ("krope", jnp.concatenate([w_kr, w_kr_rot], axis=1)),
              ("za", w_in[:, o_za:o_xbc]), ("xbc", w_in[:, o_xbc:o_dt]),
              ("zs", w_in[:, o_zs:o_zs + nx]),
              ("dt", _pad_cols(jnp.concatenate([w_dt, w_dt], axis=1), LANES))]
    off, cur = {}, 0
    for name, p in pieces:
        assert cur % LANES == 0 and p.shape[1] % LANES == 0
        off[name] = cur
        cur += p.shape[1]
    w_in_p = jnp.concatenate([p for _, p in pieces], axis=1).astype(BF16)

    w3 = w_qb.reshape(q_rank, MLA_HEADS, QK_NOPE_DIM + QK_ROPE_DIM)
    w_rope = w3[:, :, QK_NOPE_DIM:]
    w_qb_p = jnp.concatenate([w3, w_rope[:, :, half:], w_rope[:, :, :half]], axis=2)
    w_qb_p = w_qb_p.reshape(q_rank, MLA_HEADS * QK_PAD).astype(BF16)

    inv_freq = 1.0 / (ROPE_THETA ** (jnp.arange(half, dtype=F32) / half))
    freq = jnp.tile(inv_freq, LANES // half)
    phase = jnp.concatenate([jnp.zeros((2 * half,), F32), jnp.full((half,), 0.5 * math.pi, F32),
                             jnp.full((half,), -0.5 * math.pi, F32)])
    rope_tab = jnp.stack([freq, phase])

    dtb = _pad_cols(jnp.concatenate([dt_bias, dt_bias]).reshape(1, -1), LANES)
    q_scale = (QK_NOPE_DIM + QK_ROPE_DIM) ** -0.5 * LOG2E

    wkv3 = w_kvb.reshape(kv_rank, MLA_HEADS, QK_NOPE_DIM + V_HEAD_DIM)
    w_k = wkv3[:, :, :QK_NOPE_DIM].reshape(kv_rank, MLA_HEADS * QK_NOPE_DIM).astype(BF16)
    w_vt = wkv3[:, :, QK_NOPE_DIM:].reshape(kv_rank, nv).T.astype(BF16)

    q, k, vt, gza, xs, bm, cm, dt, gzs = _in_proj(
        x2, mod3, pos_col, rope_tab, w_in_p, q_norm_g.reshape(1, -1), w_qb_p,
        kv_norm_g.reshape(1, -1), w_k, w_vt, conv_w, conv_b.reshape(1, -1), dtb,
        off=off, tm=256, q_scale=q_scale)

    o_attn = _attention(q, k, vt, tq=1024, tk=512)

    alog = _pad_cols(jnp.concatenate([a_log, a_log]).reshape(1, -1), LANES)
    e_head = jnp.repeat(jnp.eye(nheads, dtype=BF16), SSM_HEAD_DIM, axis=1)
    e2 = jnp.pad(jnp.concatenate([e_head, e_head], axis=0), ((0, LANES - 2 * nheads), (0, 0)))
    dskip_x = jnp.repeat(d_skip, SSM_HEAD_DIM).reshape(1, -1)
    o_ssm = _ssd(xs, bm, cm, dt, gzs, alog, dskip_x, ssm_norm_g.reshape(1, -1), e2,
                 rows=512, nheads=nheads)

    alpha = (2.0 * depth) ** 0.25
    return _out_proj(o_attn, gza, o_ssm, x2, w_out.astype(BF16), mod3,
                     ln_g.reshape(1, -1), ln_b.reshape(1, -1), tm=512, alpha=alpha)


def kernel(x, c, positions, w_ada, b_ada, w_in, q_norm_g, w_qb, kv_norm_g, w_kvb, conv_w, conv_b,
           dt_bias, a_log, d_skip, ssm_norm_g, w_out, ln_g, ln_b):
    b, s, d = x.shape
    depth = w_in.shape[0]
    assert b == 1, "one sequence per call"
    h = x.reshape(s, d)
    pos_col = positions.reshape(s, 1)
    for l in range(depth):
        h = _layer(h, c, pos_col, w_ada[l], b_ada[l], w_in[l], q_norm_g[l], w_qb[l], kv_norm_g[l],
                   w_kvb[l], conv_w[l], conv_b[l], dt_bias[l], a_log[l], d_skip[l], ssm_norm_g[l],
                   w_out[l], ln_g[l], ln_b[l], depth=depth)
    return h.reshape(b, s, d)
```

```python
import functools
import math

import jax
import jax.numpy as jnp
from jax import lax
from jax.experimental import pallas as pl
from jax.experimental.pallas import tpu as pltpu

F32 = jnp.float32
BF16 = jnp.bfloat16

MLA_HEADS = 8
QK_NOPE_DIM = 128
QK_ROPE_DIM = 64
V_HEAD_DIM = 128
ROPE_THETA = 10000.0
SSM_HEAD_DIM = 64
SSM_GROUPS = 2
SSM_STATE = 128
CHUNK = 128
RMS_EPS = 1e-6
LN_EPS = 1e-5

LANES = 128
SUBLANES = 8
MXU_DIM = 256
QK_PAD = MXU_DIM
VMEM_LIMIT = 56 * 1024 * 1024

LOG2E = 1.4426950408889634
NEG_BIG = -1e30


def _silu(z):
    return z * (1.0 / (1.0 + jnp.exp(-z)))


def _softplus(z):
    return jnp.maximum(z, 0.0) + jnp.log1p(jnp.exp(-jnp.abs(z)))


def _split_bf16(x, parts):
    out, rem = [], x
    for _ in range(parts):
        hi = rem.astype(BF16)
        out.append(hi)
        rem = rem - hi.astype(F32)
    return out


def _mod_kernel(c_ref, w_ref, b_ref, o_ref):
    o_ref[...] = jnp.sum(w_ref[...] * c_ref[...], axis=0, keepdims=True) + b_ref[...]


def _adaln_mod(c_col, w_ada, b_ada):
    d, n = w_ada.shape
    bn = 1024
    return pl.pallas_call(
        _mod_kernel,
        grid=(n // bn,),
        in_specs=[pl.BlockSpec((d, 1), lambda j: (0, 0)),
                  pl.BlockSpec((d, bn), lambda j: (0, j)),
                  pl.BlockSpec((1, bn), lambda j: (0, j))],
        out_specs=pl.BlockSpec((1, bn), lambda j: (0, j)),
        out_shape=jax.ShapeDtypeStruct((1, n), F32),
        name="adaln_mod",
    )(c_col, w_ada, b_ada)


def _in_kernel(x_ref, mod_ref, pos_ref, rope_ref, w_in_ref, qg_ref, w_qb_ref, kvg_ref, w_k_ref, w_vt_ref,
               conv_w_ref, conv_b_ref, dtb_ref,
               q_out, k_out, vt_out, gza_out, xs_out, bm_out, cm_out, dt_out, gzs_out,
               xbuf, *, tm, off, q_scale):
    i = pl.program_id(0)
    shift = mod_ref[0:1, :]
    scale = mod_ref[1:2, :]
    u = x_ref[...] * (1.0 + scale) + shift
    proj = jnp.dot(u.astype(BF16), w_in_ref[...], preferred_element_type=F32)

    ang = pos_ref[...].astype(F32) * rope_ref[0:1, :] + rope_ref[1:2, :]
    cs = jnp.cos(ang)
    lane = lax.broadcasted_iota(jnp.int32, (tm, LANES), 1)
    low_half = lane < QK_ROPE_DIM

    def rope(rr):
        t = rr * cs
        return jnp.where(low_half, t + pltpu.roll(t, QK_ROPE_DIM, axis=1), 0.0)

    q_lat = proj[:, off["q"]:off["q"] + qg_ref.shape[1]]
    qn = q_lat * lax.rsqrt(jnp.mean(q_lat * q_lat, axis=-1, keepdims=True) + RMS_EPS) * qg_ref[...]
    qf = jnp.dot(qn.astype(BF16), w_qb_ref[...], preferred_element_type=F32)
    for h in range(MLA_HEADS):
        c0 = h * QK_PAD
        q_out[:, c0:c0 + QK_NOPE_DIM] = (qf[:, c0:c0 + QK_NOPE_DIM] * q_scale).astype(BF16)
        q_out[:, c0 + QK_NOPE_DIM:c0 + QK_PAD] = (rope(qf[:, c0 + QK_NOPE_DIM:c0 + QK_PAD]) * q_scale).astype(BF16)

    c_kv = proj[:, off["ckv"]:off["ckv"] + kvg_ref.shape[1]]
    ckvn = c_kv * lax.rsqrt(jnp.mean(c_kv * c_kv, axis=-1, keepdims=True) + RMS_EPS) * kvg_ref[...]
    ckvn_bf = ckvn.astype(BF16)
    kf = jnp.dot(ckvn_bf, w_k_ref[...], preferred_element_type=F32)
    vt_out[...] = lax.dot_general(w_vt_ref[...], ckvn_bf, (((1,), (1,)), ((), ())),
                                  preferred_element_type=F32).astype(BF16)
    k_rope = rope(proj[:, off["krope"]:off["krope"] + LANES]).astype(BF16)
    for h in range(MLA_HEADS):
        k_out[:, h * QK_PAD:h * QK_PAD + QK_NOPE_DIM] = kf[:, h * QK_NOPE_DIM:(h + 1) * QK_NOPE_DIM].astype(BF16)
        k_out[:, h * QK_PAD + QK_NOPE_DIM:(h + 1) * QK_PAD] = k_rope

    nz = gza_out.shape[1]
    gza_out[...] = _silu(proj[:, off["za"]:off["za"] + nz]).astype(BF16)
    nzs = gzs_out.shape[1]
    gzs_out[...] = _silu(proj[:, off["zs"]:off["zs"] + nzs]).astype(BF16)
    dt_out[...] = _softplus(proj[:, off["dt"]:off["dt"] + LANES] + dtb_ref[...])

    nconv = conv_w_ref.shape[0]
    nch = conv_w_ref.shape[1]

    @pl.when(i == 0)
    def _():
        xbuf[0:SUBLANES, :] = jnp.zeros((SUBLANES, nch), F32)

    xbuf[SUBLANES:SUBLANES + tm, :] = proj[:, off["xbc"]:off["xbc"] + nch]
    acc = conv_b_ref[...] + conv_w_ref[nconv - 1:nconv, :] * xbuf[SUBLANES:SUBLANES + tm, :]
    for j in range(nconv - 1):
        r0 = SUBLANES - (nconv - 1) + j
        acc = acc + conv_w_ref[j:j + 1, :] * xbuf[r0:r0 + tm, :]
    xbuf[0:SUBLANES, :] = xbuf[tm:tm + SUBLANES, :]
    xc = _silu(acc)
    nx = xs_out.shape[1]
    nb = bm_out.shape[1]
    xs_out[...] = xc[:, :nx].astype(BF16)
    bm_out[...] = xc[:, nx:nx + nb].astype(BF16)
    cm_out[...] = xc[:, nx + nb:nx + 2 * nb].astype(BF16)


def _in_proj(x2, mod3, pos_col, rope_tab, w_in_p, qg, w_qb_p, kvg, w_k, w_vt, conv_w, conv_b, dtb,
             *, off, tm, q_scale):
    s, d = x2.shape
    nq = MLA_HEADS * QK_PAD
    nv = MLA_HEADS * V_HEAD_DIM
    nch = conv_w.shape[1]
    nbc = SSM_GROUPS * SSM_STATE
    nx = nch - 2 * nbc
    row = lambda w: pl.BlockSpec((tm, w), lambda i: (i, 0))
    full = lambda a: pl.BlockSpec(a.shape, lambda i: (0,) * a.ndim, pipeline_mode=pl.Buffered(1))
    outs = [
        jax.ShapeDtypeStruct((s, nq), BF16),
        jax.ShapeDtypeStruct((s, nq), BF16),
        jax.ShapeDtypeStruct((nv, s), BF16),
        jax.ShapeDtypeStruct((s, nv), BF16),
        jax.ShapeDtypeStruct((s, nx), BF16),
        jax.ShapeDtypeStruct((s, nbc), BF16),
        jax.ShapeDtypeStruct((s, nbc), BF16),
        jax.ShapeDtypeStruct((s, LANES), F32),
        jax.ShapeDtypeStruct((s, nx), BF16),
    ]
    return pl.pallas_call(
        functools.partial(_in_kernel, tm=tm, off=off, q_scale=q_scale),
        grid=(s // tm,),
        in_specs=[row(d), full(mod3), pl.BlockSpec((tm, 1), lambda i: (i, 0)), full(rope_tab),
                  full(w_in_p), full(qg), full(w_qb_p), full(kvg), full(w_k), full(w_vt),
                  full(conv_w), full(conv_b), full(dtb)],
        out_specs=[pl.BlockSpec((nv, tm), lambda i: (0, i)) if n == 2 else row(o.shape[1])
                   for n, o in enumerate(outs)],
        out_shape=outs,
        scratch_shapes=[pltpu.VMEM((tm + 2 * SUBLANES, nch), F32)],
        compiler_params=pltpu.CompilerParams(dimension_semantics=("arbitrary",),
                                             vmem_limit_bytes=VMEM_LIMIT),
        name="in_proj",
    )(x2, mod3, pos_col, rope_tab, w_in_p, qg, w_qb_p, kvg, w_k, w_vt, conv_w, conv_b, dtb)


def _sublane_allmax(x):
    shift = SUBLANES // 2
    while shift:
        x = jnp.maximum(x, pltpu.roll(x, shift, axis=0))
        shift //= 2
    return x


def _sublane_allsum(x):
    shift = SUBLANES // 2
    while shift:
        x = x + pltpu.roll(x, shift, axis=0)
        shift //= 2
    return x


BF16_ROWS = 16
ACC_ROWS = V_HEAD_DIM + BF16_ROWS
EXP_ROWS = 32


def _attn_kernel(q_ref, k_ref, vt_ref, o_ref, m_scr, acc_scr,
                 s_a, s_b, cmax_a, cmax_b, p_a, p_b, al_a, al_b, *, tq, tk):
    assert tq == tk
    qi = pl.program_id(1)
    nsub = tk // SUBLANES
    nacc = ACC_ROWS // SUBLANES
    buf_a = (s_a, cmax_a, p_a, al_a)
    buf_b = (s_b, cmax_b, p_b, al_b)

    m_scr[...] = jnp.full(m_scr.shape, NEG_BIG, F32)
    acc_scr[...] = jnp.zeros(acc_scr.shape, F32)
    p_b[...] = jnp.zeros(p_b.shape, BF16)
    al_b[...] = jnp.ones(al_b.shape, F32)

    def kv_tile(pos):
        return jnp.maximum(jnp.where(pos == 0, qi, pos - 1), 0)

    def qk_stage(pos, buf, diag=False):
        s_out, cmax_out = buf[0], buf[1]
        start = pl.multiple_of(kv_tile(pos) * tk, tk)
        k = k_ref[pl.ds(start, tk), :]
        st = lax.dot_general(k, q_ref[...], (((1,), (1,)), ((), ())),
                             preferred_element_type=F32)
        if diag:
            key = lax.broadcasted_iota(jnp.int32, (tk, tq), 0)
            qry = lax.broadcasted_iota(jnp.int32, (tk, tq), 1)
            st = jnp.where(key <= qry, st, NEG_BIG)
        s_out[...] = st
        cmax_out[...] = _sublane_allmax(jnp.max(st.reshape(nsub, SUBLANES, tq), axis=0))

    def softmax_stage(buf):
        s_in, cmax_in, p_out, al_out = buf
        m_old = m_scr[...]
        m_new = jnp.maximum(m_old, cmax_in[...])
        al_out[...] = jnp.exp2(m_old - m_new)
        m_scr[...] = m_new
        for r in range(0, tk, EXP_ROWS):
            sc = s_in[r:r + EXP_ROWS, :].reshape(EXP_ROWS // SUBLANES, SUBLANES, tq)
            p_out[r:r + EXP_ROWS, :] = jnp.exp2(sc - m_new[None]).reshape(EXP_ROWS, tq).astype(BF16)

    ones_rows = jnp.ones((ACC_ROWS - V_HEAD_DIM, tk), BF16)

    def pv_stage(pos, buf):
        p_in, al_in = buf[2], buf[3]
        start = pl.multiple_of(kv_tile(pos) * tk, tk)
        vt_aug = jnp.concatenate([vt_ref[:, pl.ds(start, tk)], ones_rows], axis=0)
        pv = jnp.dot(vt_aug, p_in[...], preferred_element_type=F32)
        acc3 = acc_scr[...].reshape(nacc, SUBLANES, tq) * al_in[...][None]
        acc_scr[...] = acc3.reshape(ACC_ROWS, tq) + pv

    def step(pos, buf, other):
        qk_stage(pos, buf)
        pv_stage(pos - 2, buf)
        softmax_stage(other)

    def drain(last, buf, other):
        pv_stage(last - 1, other)
        softmax_stage(buf)
        pv_stage(last, buf)

    qk_stage(0, buf_a, diag=True)

    def body(jj, carry):
        step(2 * jj + 1, buf_b, buf_a)
        step(2 * jj + 2, buf_a, buf_b)
        return carry

    lax.fori_loop(0, qi // 2, body, 0)

    @pl.when(qi % 2 == 0)
    def _():
        drain(qi, buf_a, buf_b)

    @pl.when(qi % 2 == 1)
    def _():
        step(qi, buf_b, buf_a)
        drain(qi, buf_b, buf_a)

    inv_l = 1.0 / acc_scr[V_HEAD_DIM:V_HEAD_DIM + SUBLANES, :]
    o_t = acc_scr[0:V_HEAD_DIM, :].reshape(V_HEAD_DIM // SUBLANES, SUBLANES, tq) * inv_l[None]
    o_ref[...] = o_t.reshape(V_HEAD_DIM, tq).T.astype(o_ref.dtype)


def _attention(q, k, vt, *, tq, tk):
    s = q.shape[0]
    stat = pltpu.VMEM((SUBLANES, tq), F32)
    return pl.pallas_call(
        functools.partial(_attn_kernel, tq=tq, tk=tk),
        grid=(MLA_HEADS, s // tq),
        in_specs=[pl.BlockSpec((tq, QK_PAD), lambda h, i: (i, h)),
                  pl.BlockSpec((s, QK_PAD), lambda h, i: (0, h)),
                  pl.BlockSpec((V_HEAD_DIM, s), lambda h, i: (h, 0))],
        out_specs=pl.BlockSpec((tq, V_HEAD_DIM), lambda h, i: (i, h)),
        out_shape=jax.ShapeDtypeStruct((s, MLA_HEADS * V_HEAD_DIM), BF16),
        scratch_shapes=[stat, pltpu.VMEM((ACC_ROWS, tq), F32),
                        pltpu.VMEM((tk, tq), F32), pltpu.VMEM((tk, tq), F32), stat, stat,
                        pltpu.VMEM((tk, tq), BF16), pltpu.VMEM((tk, tq), BF16), stat, stat],
        compiler_params=pltpu.CompilerParams(dimension_semantics=("arbitrary", "arbitrary"),
                                             vmem_limit_bytes=VMEM_LIMIT),
        name="attention",
    )(q, k, vt)


def _ssd_kernel(xs_ref, bm_ref, cm_ref, dt_ref, gz_ref, alog_ref, dskip_ref, g_ref, e2_ref,
                o_ref, state, y_scr, *, rows, nheads):
    i = pl.program_id(0)
    gw = SSM_STATE
    hw = state.shape[2]

    @pl.when(i == 0)
    def _():
        state[...] = jnp.zeros(state.shape, F32)

    a_neg = -jnp.exp(alog_ref[...])
    r_i = lax.broadcasted_iota(jnp.int32, (CHUNK, CHUNK), 0)
    c_i = lax.broadcasted_iota(jnp.int32, (CHUNK, CHUNK), 1)
    tri = c_i <= r_i
    tri_bf = tri.astype(BF16)
    lane = lax.broadcasted_iota(jnp.int32, (CHUNK, LANES), 1)
    first_copy = lane < nheads
    head_lo = lane < SSM_HEAD_DIM
    e2 = e2_ref[...]

    def expand(v):
        hi = v.astype(BF16)
        lo = (v - hi.astype(F32)).astype(BF16)
        return jnp.dot(jnp.where(first_copy, hi, lo), e2, preferred_element_type=F32)

    for cidx in range(rows // CHUNK):
        sl = pl.ds(cidx * CHUNK, CHUNK)
        dt = dt_ref[sl, :]
        da = dt * a_neg
        a_cum = jnp.zeros((CHUNK, LANES), F32)
        for part in _split_bf16(da, 3):
            a_cum = a_cum + jnp.dot(tri_bf, part, preferred_element_type=F32)
        a_cum_t = a_cum.T
        a_last = a_cum[CHUNK - 1:CHUNK, :]
        ea = jnp.exp(a_cum)
        dte = jnp.exp(a_last - a_cum)

        xs = xs_ref[sl, :].astype(F32)
        xd = xs * expand(dt)
        ea_x = expand(ea)
        xdd = (xd * expand(dte)).astype(BF16)
        xd_bf = xd.astype(BF16)

        for g in range(SSM_GROUPS):
            bg = bm_ref[sl, g * gw:(g + 1) * gw]
            cg = cm_ref[sl, g * gw:(g + 1) * gw]
            cb = lax.dot_general(cg, bg, (((1,), (1,)), ((), ())), preferred_element_type=F32)
            prev = state[g]
            y_off = jnp.dot(cg, prev.astype(BF16), preferred_element_type=F32)
            new = lax.dot_general(bg, xdd[:, g * hw:(g + 1) * hw], (((0,), (0,)), ((), ())),
                                  preferred_element_type=F32)
            state[g] = prev * ea_x[CHUNK - 1:CHUNK, g * hw:(g + 1) * hw] + new
            y_scr[:, g * hw:(g + 1) * hw] = y_off * ea_x[:, g * hw:(g + 1) * hw]

            hpg = hw // SSM_HEAD_DIM
            for pair in range(hpg // 2):
                h0 = g * hpg + 2 * pair
                c0 = h0 * SSM_HEAD_DIM
                xp = xd_bf[:, c0:c0 + LANES]
                yp = jnp.zeros((CHUNK, LANES), F32)
                for k, keep in ((0, head_lo), (1, ~head_lo)):
                    h = h0 + k
                    seg = a_cum[:, h:h + 1] - a_cum_t[h:h + 1, :]
                    m_h = (cb * jnp.exp(jnp.where(tri, seg, NEG_BIG))).astype(BF16)
                    yp = yp + jnp.dot(m_h, jnp.where(keep, xp, jnp.zeros_like(xp)),
                                      preferred_element_type=F32)
                y_scr[:, c0:c0 + LANES] = y_scr[:, c0:c0 + LANES] + yp

        y = y_scr[...] + xs * dskip_ref[...]
        hf = y * gz_ref[sl, :].astype(F32)
        for g in range(SSM_GROUPS):
            hg = hf[:, g * hw:(g + 1) * hw]
            ms = jnp.mean(hg * hg, axis=-1, keepdims=True)
            o_ref[sl, g * hw:(g + 1) * hw] = (hg * lax.rsqrt(ms + RMS_EPS)
                                               * g_ref[:, g * hw:(g + 1) * hw]).astype(o_ref.dtype)


def _ssd(xs, bm, cm, dt, gzs, alog, dskip_x, norm_g, e2, *, rows, nheads):
    s, nx = xs.shape
    hw = nx // SSM_GROUPS
    row = lambda a: pl.BlockSpec((rows, a.shape[1]), lambda i: (i, 0))
    full = lambda a: pl.BlockSpec(a.shape, lambda i: (0,) * a.ndim)
    return pl.pallas_call(
        functools.partial(_ssd_kernel, rows=rows, nheads=nheads),
        grid=(s // rows,),
        in_specs=[row(xs), row(bm), row(cm), row(dt), row(gzs),
                  full(alog), full(dskip_x), full(norm_g), full(e2)],
        out_specs=pl.BlockSpec((rows, nx), lambda i: (i, 0)),
        out_shape=jax.ShapeDtypeStruct((s, nx), BF16),
        scratch_shapes=[pltpu.VMEM((SSM_GROUPS, SSM_STATE, hw), F32),
                        pltpu.VMEM((CHUNK, nx), F32)],
        compiler_params=pltpu.CompilerParams(dimension_semantics=("arbitrary",),
                                             vmem_limit_bytes=VMEM_LIMIT),
        name="ssd",
    )(xs, bm, cm, dt, gzs, alog, dskip_x, norm_g, e2)


def _out_kernel(oa_ref, gza_ref, os_ref, x_ref, w_ref, mod_ref, lng_ref, lnb_ref, o_ref, *, alpha):
    na = oa_ref.shape[1]
    oa = (oa_ref[...].astype(F32) * gza_ref[...].astype(F32)).astype(BF16)
    mixed = jnp.dot(oa, w_ref[0:na, :], preferred_element_type=F32)
    mixed = mixed + jnp.dot(os_ref[...], w_ref[na:, :], preferred_element_type=F32)
    y = alpha * x_ref[...] + mod_ref[2:3, :] * mixed
    mu = jnp.mean(y, axis=-1, keepdims=True)
    yc = y - mu
    var = jnp.mean(yc * yc, axis=-1, keepdims=True)
    o_ref[...] = yc * lax.rsqrt(var + LN_EPS) * lng_ref[...] + lnb_ref[...]


def _out_proj(oa, gza, o_ssm, x2, w_out_bf, mod3, ln_g, ln_b, *, tm, alpha):
    s, d = x2.shape
    row = lambda a: pl.BlockSpec((tm, a.shape[1]), lambda i: (i, 0))
    full = lambda a: pl.BlockSpec(a.shape, lambda i: (0,) * a.ndim)
    return pl.pallas_call(
        functools.partial(_out_kernel, alpha=alpha),
        grid=(s // tm,),
        in_specs=[row(oa), row(gza), row(o_ssm), row(x2), full(w_out_bf), full(mod3),
                  full(ln_g), full(ln_b)],
        out_specs=pl.BlockSpec((tm, d), lambda i: (i, 0)),
        out_shape=jax.ShapeDtypeStruct((s, d), F32),
        compiler_params=pltpu.CompilerParams(dimension_semantics=("arbitrary",),
                                             vmem_limit_bytes=VMEM_LIMIT),
        name="out_proj",
    )(oa, gza, o_ssm, x2, w_out_bf, mod3, ln_g, ln_b)


def _pad_cols(a, width):
    return jnp.pad(a, ((0, 0), (0, width - a.shape[1])))


def _layer(x2, c, pos_col, w_ada, b_ada, w_in, q_norm_g, w_qb, kv_norm_g, w_kvb,
           conv_w, conv_b, dt_bias, a_log, d_skip, ssm_norm_g, w_out, ln_g, ln_b, *, depth):
    s, d = x2.shape
    q_rank = q_norm_g.shape[0]
    kv_rank = kv_norm_g.shape[0]
    nheads = dt_bias.shape[0]
    nch = conv_w.shape[1]
    nv = MLA_HEADS * V_HEAD_DIM
    nx = nheads * SSM_HEAD_DIM
    half = QK_ROPE_DIM // 2

    mod = _adaln_mod(c.reshape(d, 1), w_ada, b_ada.reshape(1, -1))
    mod3 = mod.reshape(3, d)

    o_q, o_ckv = 0, q_rank
    o_kr = o_ckv + kv_rank
    o_za = o_kr + QK_ROPE_DIM
    o_xbc = o_za + nv
    o_dt = o_xbc + nch
    o_zs = o_dt + nheads
    w_kr = w_in[:, o_kr:o_kr + QK_ROPE_DIM]
    w_kr_rot = jnp.concatenate([w_kr[:, half:], w_kr[:, :half]], axis=1)
    w_dt = w_in[:, o_dt:o_dt + nheads]
    pieces = [("q", w_in[:, o_q:o_ckv]), ("ckv", w_in[:, o_ckv:o_kr]),
              ("krope", jnp.concatenate([w_kr, w_kr_rot], axis=1)),
              ("za", w_in[:, o_za:o_xbc]), ("xbc", w_in[:, o_xbc:o_dt]),
              ("zs", w_in[:, o_zs:o_zs + nx]),
              ("dt", _pad_cols(jnp.concatenate([w_dt, w_dt], axis=1), LANES))]
    off, cur = {}, 0
    for name, p in pieces:
        assert cur % LANES == 0 and p.shape[1] % LANES == 0
        off[name] = cur
        cur += p.shape[1]
    w_in_p = jnp.concatenate([p for _, p in pieces], axis=1).astype(BF16)

    w3 = w_qb.reshape(q_rank, MLA_HEADS, QK_NOPE_DIM + QK_ROPE_DIM)
    w_rope = w3[:, :, QK_NOPE_DIM:]
    w_qb_p = jnp.concatenate([w3, w_rope[:, :, half:], w_rope[:, :, :half]], axis=2)
    w_qb_p = w_qb_p.reshape(q_rank, MLA_HEADS * QK_PAD).astype(BF16)

    inv_freq = 1.0 / (ROPE_THETA ** (jnp.arange(half, dtype=F32) / half))
    freq = jnp.tile(inv_freq, LANES // half)
    phase = jnp.concatenate([jnp.zeros((2 * half,), F32), jnp.full((half,), 0.5 * math.pi, F32),
                             jnp.full((half,), -0.5 * math.pi, F32)])
    rope_tab = jnp.stack([freq, phase])

    dtb = _pad_cols(jnp.concatenate([dt_bias, dt_bias]).reshape(1, -1), LANES)
    q_scale = (QK_NOPE_DIM + QK_ROPE_DIM) ** -0.5 * LOG2E

    wkv3 = w_kvb.reshape(kv_rank, MLA_HEADS, QK_NOPE_DIM + V_HEAD_DIM)
    w_k = wkv3[:, :, :QK_NOPE_DIM].reshape(kv_rank, MLA_HEADS * QK_NOPE_DIM).astype(BF16)
    w_vt = wkv3[:, :, QK_NOPE_DIM:].reshape(kv_rank, nv).T.astype(BF16)

    q, k, vt, gza, xs, bm, cm, dt, gzs = _in_proj(
        x2, mod3, pos_col, rope_tab, w_in_p, q_norm_g.reshape(1, -1), w_qb_p,
        kv_norm_g.reshape(1, -1), w_k, w_vt, conv_w, conv_b.reshape(1, -1), dtb,
        off=off, tm=256, q_scale=q_scale)

    o_attn = _attention(q, k, vt, tq=1024, tk=1024)

    alog = _pad_cols(jnp.concatenate([a_log, a_log]).reshape(1, -1), LANES)
    e_head = jnp.repeat(jnp.eye(nheads, dtype=BF16), SSM_HEAD_DIM, axis=1)
    e2 = jnp.pad(jnp.concatenate([e_head, e_head], axis=0), ((0, LANES - 2 * nheads), (0, 0)))
    dskip_x = jnp.repeat(d_skip, SSM_HEAD_DIM).reshape(1, -1)
    o_ssm = _ssd(xs, bm, cm, dt, gzs, alog, dskip_x, ssm_norm_g.reshape(1, -1), e2,
                 rows=512, nheads=nheads)

    alpha = (2.0 * depth) ** 0.25
    return _out_proj(o_attn, gza, o_ssm, x2, w_out.astype(BF16), mod3,
                     ln_g.reshape(1, -1), ln_b.reshape(1, -1), tm=512, alpha=alpha)


def kernel(x, c, positions, w_ada, b_ada, w_in, q_norm_g, w_qb, kv_norm_g, w_kvb, conv_w, conv_b,
           dt_bias, a_log, d_skip, ssm_norm_g, w_out, ln_g, ln_b):
    b, s, d = x.shape
    depth = w_in.shape[0]
    assert b == 1, "one sequence per call"
    h = x.reshape(s, d)
    pos_col = positions.reshape(s, 1)
    for l in range(depth):
        h = _layer(h, c, pos_col, w_ada[l], b_ada[l], w_in[l], q_norm_g[l], w_qb[l], kv_norm_g[l],
                   w_kvb[l], conv_w[l], conv_b[l], dt_bias[l], a_log[l], d_skip[l], ssm_norm_g[l],
                   w_out[l], ln_g[l], ln_b[l], depth=depth)
    return h.reshape(b, s, d)
```

```python
import functools
import math

import jax
import jax.numpy as jnp
from jax import lax
from jax.experimental import pallas as pl
from jax.experimental.pallas import tpu as pltpu

F32 = jnp.float32
BF16 = jnp.bfloat16

MLA_HEADS = 8
QK_NOPE_DIM = 128
QK_ROPE_DIM = 64
V_HEAD_DIM = 128
ROPE_THETA = 10000.0
SSM_HEAD_DIM = 64
SSM_GROUPS = 2
SSM_STATE = 128
CHUNK = 128
RMS_EPS = 1e-6
LN_EPS = 1e-5

LANES = 128
SUBLANES = 8
MXU_DIM = 256
QK_PAD = MXU_DIM
VMEM_LIMIT = 56 * 1024 * 1024

IN_ROWS = 512
IN_SUB = 256
OUT_ROWS = 1024
OUT_SUB = 256

LOG2E = 1.4426950408889634
NEG_BIG = -1e30


def _silu(z):
    return z * (1.0 / (1.0 + jnp.exp(-z)))


def _softplus(z):
    return jnp.maximum(z, 0.0) + jnp.log1p(jnp.exp(-jnp.abs(z)))


def _split_bf16(x, parts):
    out, rem = [], x
    for _ in range(parts):
        hi = rem.astype(BF16)
        out.append(hi)
        rem = rem - hi.astype(F32)
    return out


def _mod_kernel(c_ref, w_ref, b_ref, o_ref):
    o_ref[...] = jnp.sum(w_ref[...] * c_ref[...], axis=0, keepdims=True) + b_ref[...]


def _adaln_mod(c_col, w_ada, b_ada):
    d, n = w_ada.shape
    bn = 1024
    return pl.pallas_call(
        _mod_kernel,
        grid=(n // bn,),
        in_specs=[pl.BlockSpec((d, 1), lambda j: (0, 0)),
                  pl.BlockSpec((d, bn), lambda j: (0, j)),
                  pl.BlockSpec((1, bn), lambda j: (0, j))],
        out_specs=pl.BlockSpec((1, bn), lambda j: (0, j)),
        out_shape=jax.ShapeDtypeStruct((1, n), F32),
        name="adaln_mod",
    )(c_col, w_ada, b_ada)


def _in_kernel(x_ref, mod_ref, pos_ref, rope_ref, w_in_ref, qg_ref, w_qb_ref, kvg_ref, w_k_ref, w_vt_ref,
               conv_w_ref, conv_b_ref, dtb_ref,
               q_out, k_out, vt_out, gza_out, xs_out, bm_out, cm_out, dt_out, gzs_out,
               xbuf, *, tm, sub, off, q_scale):
    i = pl.program_id(0)
    shift = mod_ref[0:1, :]
    scale1 = 1.0 + mod_ref[1:2, :]
    lane = lax.broadcasted_iota(jnp.int32, (sub, LANES), 1)
    low_half = lane < QK_ROPE_DIM
    nconv, nch = conv_w_ref.shape
    nx = xs_out.shape[1]
    nb = bm_out.shape[1]

    @pl.when(i == 0)
    def _():
        xbuf[0:SUBLANES, :] = jnp.zeros((SUBLANES, nch), F32)

    projs = []
    for h in range(tm // sub):
        u = x_ref[h * sub:(h + 1) * sub, :] * scale1 + shift
        projs.append(jnp.dot(u.astype(BF16), w_in_ref[...], preferred_element_type=F32))

    for h, proj in enumerate(projs):
        rows = slice(h * sub, (h + 1) * sub)

        ang_t = pos_ref[:, rows].astype(F32) * rope_ref[0] + rope_ref[1]
        cs = jnp.cos(ang_t).T

        def rope(rr):
            t = rr * cs
            return jnp.where(low_half, t + pltpu.roll(t, QK_ROPE_DIM, axis=1), 0.0)

        q_lat = proj[:, off["q"]:off["q"] + qg_ref.shape[1]]
        qn = q_lat * lax.rsqrt(jnp.mean(q_lat * q_lat, axis=-1, keepdims=True) + RMS_EPS) * qg_ref[...]
        qf = jnp.dot(qn.astype(BF16), w_qb_ref[...], preferred_element_type=F32)
        for hd in range(MLA_HEADS):
            c0 = hd * QK_PAD
            q_out[rows, c0:c0 + QK_NOPE_DIM] = (qf[:, c0:c0 + QK_NOPE_DIM] * q_scale).astype(BF16)
            q_out[rows, c0 + QK_NOPE_DIM:c0 + QK_PAD] = (
                rope(qf[:, c0 + QK_NOPE_DIM:c0 + QK_PAD]) * q_scale).astype(BF16)

        c_kv = proj[:, off["ckv"]:off["ckv"] + kvg_ref.shape[1]]
        ckvn = c_kv * lax.rsqrt(jnp.mean(c_kv * c_kv, axis=-1, keepdims=True) + RMS_EPS) * kvg_ref[...]
        ckvn_bf = ckvn.astype(BF16)
        kf = jnp.dot(ckvn_bf, w_k_ref[...], preferred_element_type=F32)
        vt_out[:, rows] = lax.dot_general(w_vt_ref[...], ckvn_bf, (((1,), (1,)), ((), ())),
                                          preferred_element_type=F32).astype(BF16)
        k_rope = rope(proj[:, off["krope"]:off["krope"] + LANES]).astype(BF16)
        for hd in range(MLA_HEADS):
            k_out[rows, hd * QK_PAD:hd * QK_PAD + QK_NOPE_DIM] = (
                kf[:, hd * QK_NOPE_DIM:(hd + 1) * QK_NOPE_DIM].astype(BF16))
            k_out[rows, hd * QK_PAD + QK_NOPE_DIM:(hd + 1) * QK_PAD] = k_rope

        gza_out[rows, :] = _silu(proj[:, off["za"]:off["za"] + gza_out.shape[1]]).astype(BF16)
        gzs_out[rows, :] = _silu(proj[:, off["zs"]:off["zs"] + gzs_out.shape[1]]).astype(BF16)
        dt_out[rows, :] = _softplus(proj[:, off["dt"]:off["dt"] + LANES] + dtb_ref[...])

        r1 = SUBLANES + h * sub
        xbuf[r1:r1 + sub, :] = proj[:, off["xbc"]:off["xbc"] + nch]
        acc = conv_b_ref[...] + conv_w_ref[nconv - 1:nconv, :] * xbuf[r1:r1 + sub, :]
        for j in range(nconv - 1):
            r0 = r1 - (nconv - 1) + j
            acc = acc + conv_w_ref[j:j + 1, :] * xbuf[r0:r0 + sub, :]
        xc = _silu(acc)
        xs_out[rows, :] = xc[:, :nx].astype(BF16)
        bm_out[rows, :] = xc[:, nx:nx + nb].astype(BF16)
        cm_out[rows, :] = xc[:, nx + nb:nx + 2 * nb].astype(BF16)

    xbuf[0:SUBLANES, :] = xbuf[tm:tm + SUBLANES, :]


def _in_proj(x2, mod3, pos_row, rope_tab, w_in_p, qg, w_qb_p, kvg, w_k, w_vt, conv_w, conv_b, dtb,
             *, off, tm, sub, q_scale):
    s, d = x2.shape
    nq = MLA_HEADS * QK_PAD
    nv = MLA_HEADS * V_HEAD_DIM
    nch = conv_w.shape[1]
    nbc = SSM_GROUPS * SSM_STATE
    nx = nch - 2 * nbc
    row = lambda w: pl.BlockSpec((tm, w), lambda i: (i, 0))
    full = lambda a: pl.BlockSpec(a.shape, lambda i: (0,) * a.ndim, pipeline_mode=pl.Buffered(1))
    outs = [
        jax.ShapeDtypeStruct((s, nq), BF16),
        jax.ShapeDtypeStruct((s, nq), BF16),
        jax.ShapeDtypeStruct((nv, s), BF16),
        jax.ShapeDtypeStruct((s, nv), BF16),
        jax.ShapeDtypeStruct((s, nx), BF16),
        jax.ShapeDtypeStruct((s, nbc), BF16),
        jax.ShapeDtypeStruct((s, nbc), BF16),
        jax.ShapeDtypeStruct((s, LANES), F32),
        jax.ShapeDtypeStruct((s, nx), BF16),
    ]
    return pl.pallas_call(
        functools.partial(_in_kernel, tm=tm, sub=sub, off=off, q_scale=q_scale),
        grid=(s // tm,),
        in_specs=[row(d), full(mod3), pl.BlockSpec((1, tm), lambda i: (0, i)), full(rope_tab),
                  full(w_in_p), full(qg), full(w_qb_p), full(kvg), full(w_k), full(w_vt),
                  full(conv_w), full(conv_b), full(dtb)],
        out_specs=[pl.BlockSpec((nv, tm), lambda i: (0, i)) if n == 2 else row(o.shape[1])
                   for n, o in enumerate(outs)],
        out_shape=outs,
        scratch_shapes=[pltpu.VMEM((tm + 2 * SUBLANES, nch), F32)],
        compiler_params=pltpu.CompilerParams(dimension_semantics=("arbitrary",),
                                             vmem_limit_bytes=VMEM_LIMIT),
        name="in_proj",
    )(x2, mod3, pos_row, rope_tab, w_in_p, qg, w_qb_p, kvg, w_k, w_vt, conv_w, conv_b, dtb)


def _sublane_allmax(x):
    shift = SUBLANES // 2
    while shift:
        x = jnp.maximum(x, pltpu.roll(x, shift, axis=0))
        shift //= 2
    return x


def _sublane_allsum(x):
    shift = SUBLANES // 2
    while shift:
        x = x + pltpu.roll(x, shift, axis=0)
        shift //= 2
    return x


BF16_ROWS = 16
ACC_ROWS = V_HEAD_DIM + BF16_ROWS
EXP_ROWS = 32


def _attn_kernel(q_ref, k_ref, vt_ref, o_ref, m_scr, acc_scr,
                 s_a, s_b, cmax_a, cmax_b, p_a, p_b, al_a, al_b, *, tq, tk):
    assert tq == tk
    qi = pl.program_id(1)
    nsub = tk // SUBLANES
    nacc = ACC_ROWS // SUBLANES
    buf_a = (s_a, cmax_a, p_a, al_a)
    buf_b = (s_b, cmax_b, p_b, al_b)

    m_scr[...] = jnp.full(m_scr.shape, NEG_BIG, F32)
    acc_scr[...] = jnp.zeros(acc_scr.shape, F32)
    p_b[...] = jnp.zeros(p_b.shape, BF16)
    al_b[...] = jnp.ones(al_b.shape, F32)

    def kv_tile(pos):
        return jnp.maximum(jnp.where(pos == 0, qi, pos - 1), 0)

    def qk_stage(pos, buf, diag=False):
        s_out, cmax_out = buf[0], buf[1]
        start = pl.multiple_of(kv_tile(pos) * tk, tk)
        k = k_ref[pl.ds(start, tk), :]
        st = lax.dot_general(k, q_ref[...], (((1,), (1,)), ((), ())),
                             preferred_element_type=F32)
        if diag:
            key = lax.broadcasted_iota(jnp.int32, (tk, tq), 0)
            qry = lax.broadcasted_iota(jnp.int32, (tk, tq), 1)
            st = jnp.where(key <= qry, st, NEG_BIG)
        s_out[...] = st
        cmax_out[...] = _sublane_allmax(jnp.max(st.reshape(nsub, SUBLANES, tq), axis=0))

    def softmax_stage(buf):
        s_in, cmax_in, p_out, al_out = buf
        m_old = m_scr[...]
        m_new = jnp.maximum(m_old, cmax_in[...])
        al_out[...] = jnp.exp2(m_old - m_new)
        m_scr[...] = m_new
        for r in range(0, tk, EXP_ROWS):
            sc = s_in[r:r + EXP_ROWS, :].reshape(EXP_ROWS // SUBLANES, SUBLANES, tq)
            p_out[r:r + EXP_ROWS, :] = jnp.exp2(sc - m_new[None]).reshape(EXP_ROWS, tq).astype(BF16)

    ones_rows = jnp.ones((ACC_ROWS - V_HEAD_DIM, tk), BF16)

    def pv_stage(pos, buf):
        p_in, al_in = buf[2], buf[3]
        start = pl.multiple_of(kv_tile(pos) * tk, tk)
        vt_aug = jnp.concatenate([vt_ref[:, pl.ds(start, tk)], ones_rows], axis=0)
        pv = jnp.dot(vt_aug, p_in[...], preferred_element_type=F32)
        acc3 = acc_scr[...].reshape(nacc, SUBLANES, tq) * al_in[...][None]
        acc_scr[...] = acc3.reshape(ACC_ROWS, tq) + pv

    def step(pos, buf, other):
        qk_stage(pos, buf)
        pv_stage(pos - 2, buf)
        softmax_stage(other)

    def drain(last, buf, other):
        pv_stage(last - 1, other)
        softmax_stage(buf)
        pv_stage(last, buf)

    qk_stage(0, buf_a, diag=True)

    def body(jj, carry):
        step(2 * jj + 1, buf_b, buf_a)
        step(2 * jj + 2, buf_a, buf_b)
        return carry

    lax.fori_loop(0, qi // 2, body, 0)

    @pl.when(qi % 2 == 0)
    def _():
        drain(qi, buf_a, buf_b)

    @pl.when(qi % 2 == 1)
    def _():
        step(qi, buf_b, buf_a)
        drain(qi, buf_b, buf_a)

    inv_l = 1.0 / acc_scr[V_HEAD_DIM:V_HEAD_DIM + SUBLANES, :]
    o_t = acc_scr[0:V_HEAD_DIM, :].reshape(V_HEAD_DIM // SUBLANES, SUBLANES, tq) * inv_l[None]
    o_ref[...] = o_t.reshape(V_HEAD_DIM, tq).T.astype(o_ref.dtype)


def _attention(q, k, vt, *, tq, tk):
    s = q.shape[0]
    stat = pltpu.VMEM((SUBLANES, tq), F32)
    return pl.pallas_call(
        functools.partial(_attn_kernel, tq=tq, tk=tk),
        grid=(MLA_HEADS, s // tq),
        in_specs=[pl.BlockSpec((tq, QK_PAD), lambda h, i: (i, h)),
                  pl.BlockSpec((s, QK_PAD), lambda h, i: (0, h)),
                  pl.BlockSpec((V_HEAD_DIM, s), lambda h, i: (h, 0))],
        out_specs=pl.BlockSpec((tq, V_HEAD_DIM), lambda h, i: (i, h)),
        out_shape=jax.ShapeDtypeStruct((s, MLA_HEADS * V_HEAD_DIM), BF16),
        scratch_shapes=[stat, pltpu.VMEM((ACC_ROWS, tq), F32),
                        pltpu.VMEM((tk, tq), F32), pltpu.VMEM((tk, tq), F32), stat, stat,
                        pltpu.VMEM((tk, tq), BF16), pltpu.VMEM((tk, tq), BF16), stat, stat],
        compiler_params=pltpu.CompilerParams(dimension_semantics=("arbitrary", "arbitrary"),
                                             vmem_limit_bytes=VMEM_LIMIT),
        name="attention",
    )(q, k, vt)


def _ssd_kernel(xs_ref, bm_ref, cm_ref, dt_ref, gz_ref, alog_ref, dskip_ref, g_ref, e2_ref,
                o_ref, state, y_scr, *, rows, nheads):
    i = pl.program_id(0)
    gw = SSM_STATE
    hw = state.shape[2]

    @pl.when(i == 0)
    def _():
        state[...] = jnp.zeros(state.shape, F32)

    a_neg = -jnp.exp(alog_ref[...])
    r_i = lax.broadcasted_iota(jnp.int32, (CHUNK, CHUNK), 0)
    c_i = lax.broadcasted_iota(jnp.int32, (CHUNK, CHUNK), 1)
    tri = c_i <= r_i
    tri_bf = tri.astype(BF16)
    lane = lax.broadcasted_iota(jnp.int32, (CHUNK, LANES), 1)
    first_copy = lane < nheads
    head_lo = lane < SSM_HEAD_DIM
    e2 = e2_ref[...]

    def expand(v):
        hi = v.astype(BF16)
        lo = (v - hi.astype(F32)).astype(BF16)
        return jnp.dot(jnp.where(first_copy, hi, lo), e2, preferred_element_type=F32)

    for cidx in range(rows // CHUNK):
        sl = pl.ds(cidx * CHUNK, CHUNK)
        dt = dt_ref[sl, :]
        da = dt * a_neg
        a_cum = jnp.zeros((CHUNK, LANES), F32)
        for part in _split_bf16(da, 3):
            a_cum = a_cum + jnp.dot(tri_bf, part, preferred_element_type=F32)
        a_cum_t = a_cum.T
        a_last = a_cum[CHUNK - 1:CHUNK, :]
        ea = jnp.exp(a_cum)
        dte = jnp.exp(a_last - a_cum)

        xs = xs_ref[sl, :].astype(F32)
        xd = xs * expand(dt)
        ea_x = expand(ea)
        xdd = (xd * expand(dte)).astype(BF16)
        xd_bf = xd.astype(BF16)

        for g in range(SSM_GROUPS):
            bg = bm_ref[sl, g * gw:(g + 1) * gw]
            cg = cm_ref[sl, g * gw:(g + 1) * gw]
            cb = lax.dot_general(cg, bg, (((1,), (1,)), ((), ())), preferred_element_type=F32)
            prev = state[g]
            y_off = jnp.dot(cg, prev.astype(BF16), preferred_element_type=F32)
            new = lax.dot_general(bg, xdd[:, g * hw:(g + 1) * hw], (((0,), (0,)), ((), ())),
                                  preferred_element_type=F32)
            state[g] = prev * ea_x[CHUNK - 1:CHUNK, g * hw:(g + 1) * hw] + new
            y_scr[:, g * hw:(g + 1) * hw] = y_off * ea_x[:, g * hw:(g + 1) * hw]

            hpg = hw // SSM_HEAD_DIM
            for pair in range(hpg // 2):
                h0 = g * hpg + 2 * pair
                c0 = h0 * SSM_HEAD_DIM
                xp = xd_bf[:, c0:c0 + LANES]
                yp = jnp.zeros((CHUNK, LANES), F32)
                for k, keep in ((0, head_lo), (1, ~head_lo)):
                    h = h0 + k
                    seg = a_cum[:, h:h + 1] - a_cum_t[h:h + 1, :]
                    m_h = (cb * jnp.exp(jnp.where(tri, seg, NEG_BIG))).astype(BF16)
                    yp = yp + jnp.dot(m_h, jnp.where(keep, xp, jnp.zeros_like(xp)),
                                      preferred_element_type=F32)
                y_scr[:, c0:c0 + LANES] = y_scr[:, c0:c0 + LANES] + yp

        y = y_scr[...] + xs * dskip_ref[...]
        hf = y * gz_ref[sl, :].astype(F32)
        for g in range(SSM_GROUPS):
            hg = hf[:, g * hw:(g + 1) * hw]
            ms = jnp.mean(hg * hg, axis=-1, keepdims=True)
            o_ref[sl, g * hw:(g + 1) * hw] = (hg * lax.rsqrt(ms + RMS_EPS)
                                               * g_ref[:, g * hw:(g + 1) * hw]).astype(o_ref.dtype)


def _ssd(xs, bm, cm, dt, gzs, alog, dskip_x, norm_g, e2, *, rows, nheads):
    s, nx = xs.shape
    hw = nx // SSM_GROUPS
    row = lambda a: pl.BlockSpec((rows, a.shape[1]), lambda i: (i, 0))
    full = lambda a: pl.BlockSpec(a.shape, lambda i: (0,) * a.ndim)
    return pl.pallas_call(
        functools.partial(_ssd_kernel, rows=rows, nheads=nheads),
        grid=(s // rows,),
        in_specs=[row(xs), row(bm), row(cm), row(dt), row(gzs),
                  full(alog), full(dskip_x), full(norm_g), full(e2)],
        out_specs=pl.BlockSpec((rows, nx), lambda i: (i, 0)),
        out_shape=jax.ShapeDtypeStruct((s, nx), BF16),
        scratch_shapes=[pltpu.VMEM((SSM_GROUPS, SSM_STATE, hw), F32),
                        pltpu.VMEM((CHUNK, nx), F32)],
        compiler_params=pltpu.CompilerParams(dimension_semantics=("arbitrary",),
                                             vmem_limit_bytes=VMEM_LIMIT),
        name="ssd",
    )(xs, bm, cm, dt, gzs, alog, dskip_x, norm_g, e2)


def _out_kernel(oa_ref, gza_ref, os_ref, x_ref, w_ref, mod_ref, lng_ref, lnb_ref, o_ref, *, alpha, sub):
    na = oa_ref.shape[1]
    mixes = []
    for r in range(0, o_ref.shape[0], sub):
        rows = slice(r, r + sub)
        oa = (oa_ref[rows, :].astype(F32) * gza_ref[rows, :].astype(F32)).astype(BF16)
        mixed = jnp.dot(oa, w_ref[0:na, :], preferred_element_type=F32)
        mixes.append(mixed + jnp.dot(os_ref[rows, :], w_ref[na:, :], preferred_element_type=F32))
    for n, mixed in enumerate(mixes):
        rows = slice(n * sub, (n + 1) * sub)
        y = alpha * x_ref[rows, :] + mod_ref[2:3, :] * mixed
        mu = jnp.mean(y, axis=-1, keepdims=True)
        yc = y - mu
        var = jnp.mean(yc * yc, axis=-1, keepdims=True)
        o_ref[rows, :] = yc * lax.rsqrt(var + LN_EPS) * lng_ref[...] + lnb_ref[...]


def _out_proj(oa, gza, o_ssm, x2, w_out_bf, mod3, ln_g, ln_b, *, tm, sub, alpha):
    s, d = x2.shape
    row = lambda a: pl.BlockSpec((tm, a.shape[1]), lambda i: (i, 0))
    full = lambda a: pl.BlockSpec(a.shape, lambda i: (0,) * a.ndim)
    return pl.pallas_call(
        functools.partial(_out_kernel, alpha=alpha, sub=sub),
        grid=(s // tm,),
        in_specs=[row(oa), row(gza), row(o_ssm), row(x2), full(w_out_bf), full(mod3),
                  full(ln_g), full(ln_b)],
        out_specs=pl.BlockSpec((tm, d), lambda i: (i, 0)),
        out_shape=jax.ShapeDtypeStruct((s, d), F32),
        compiler_params=pltpu.CompilerParams(dimension_semantics=("arbitrary",),
                                             vmem_limit_bytes=VMEM_LIMIT),
        name="out_proj",
    )(oa, gza, o_ssm, x2, w_out_bf, mod3, ln_g, ln_b)


def _pad_cols(a, width):
    return jnp.pad(a, ((0, 0), (0, width - a.shape[1])))


def _layer(x2, c, pos_row, w_ada, b_ada, w_in, q_norm_g, w_qb, kv_norm_g, w_kvb,
           conv_w, conv_b, dt_bias, a_log, d_skip, ssm_norm_g, w_out, ln_g, ln_b, *, depth):
    s, d = x2.shape
    q_rank = q_norm_g.shape[0]
    kv_rank = kv_norm_g.shape[0]
    nheads = dt_bias.shape[0]
    nch = conv_w.shape[1]
    nv = MLA_HEADS * V_HEAD_DIM
    nx = nheads * SSM_HEAD_DIM
    half = QK_ROPE_DIM // 2

    mod = _adaln_mod(c.reshape(d, 1), w_ada, b_ada.reshape(1, -1))
    mod3 = mod.reshape(3, d)

    o_q, o_ckv = 0, q_rank
    o_kr = o_ckv + kv_rank
    o_za = o_kr + QK_ROPE_DIM
    o_xbc = o_za + nv
    o_dt = o_xbc + nch
    o_zs = o_dt + nheads
    w_kr = w_in[:, o_kr:o_kr + QK_ROPE_DIM]
    w_kr_rot = jnp.concatenate([w_kr[:, half:], w_kr[:, :half]], axis=1)
    w_dt = w_in[:, o_dt:o_dt + nheads]
    pieces = [("q", w_in[:, o_q:o_ckv]), ("ckv", w_in[:, o_ckv:o_kr]),
              ("krope", jnp.concatenate([w_kr, w_kr_rot], axis=1)),
              ("za", w_in[:, o_za:o_xbc]), ("xbc", w_in[:, o_xbc:o_dt]),
              ("zs", w_in[:, o_zs:o_zs + nx]),
              ("dt", _pad_cols(jnp.concatenate([w_dt, w_dt], axis=1), LANES))]
    off, cur = {}, 0
    for name, p in pieces:
        assert cur % LANES == 0 and p.shape[1] % LANES == 0
        off[name] = cur
        cur += p.shape[1]
    w_in_p = jnp.concatenate([p for _, p in pieces], axis=1).astype(BF16)

    w3 = w_qb.reshape(q_rank, MLA_HEADS, QK_NOPE_DIM + QK_ROPE_DIM)
    w_rope = w3[:, :, QK_NOPE_DIM:]
    w_qb_p = jnp.concatenate([w3, w_rope[:, :, half:], w_rope[:, :, :half]], axis=2)
    w_qb_p = w_qb_p.reshape(q_rank, MLA_HEADS * QK_PAD).astype(BF16)

    inv_freq = 1.0 / (ROPE_THETA ** (jnp.arange(half, dtype=F32) / half))
    freq = jnp.tile(inv_freq, LANES // half)
    phase = jnp.concatenate([jnp.zeros((2 * half,), F32), jnp.full((half,), 0.5 * math.pi, F32),
                             jnp.full((half,), -0.5 * math.pi, F32)])
    rope_tab = jnp.broadcast_to(jnp.stack([freq, phase])[:, :, None], (2, LANES, IN_SUB))

    dtb = _pad_cols(jnp.concatenate([dt_bias, dt_bias]).reshape(1, -1), LANES)
    q_scale = (QK_NOPE_DIM + QK_ROPE_DIM) ** -0.5 * LOG2E

    wkv3 = w_kvb.reshape(kv_rank, MLA_HEADS, QK_NOPE_DIM + V_HEAD_DIM)
    w_k = wkv3[:, :, :QK_NOPE_DIM].reshape(kv_rank, MLA_HEADS * QK_NOPE_DIM).astype(BF16)
    w_vt = wkv3[:, :, QK_NOPE_DIM:].reshape(kv_rank, nv).T.astype(BF16)

    q, k, vt, gza, xs, bm, cm, dt, gzs = _in_proj(
        x2, mod3, pos_row, rope_tab, w_in_p, q_norm_g.reshape(1, -1), w_qb_p,
        kv_norm_g.reshape(1, -1), w_k, w_vt, conv_w, conv_b.reshape(1, -1), dtb,
        off=off, tm=IN_ROWS, sub=IN_SUB, q_scale=q_scale)

    o_attn = _attention(q, k, vt, tq=1024, tk=1024)

    alog = _pad_cols(jnp.concatenate([a_log, a_log]).reshape(1, -1), LANES)
    e_head = jnp.repeat(jnp.eye(nheads, dtype=BF16), SSM_HEAD_DIM, axis=1)
    e2 = jnp.pad(jnp.concatenate([e_head, e_head], axis=0), ((0, LANES - 2 * nheads), (0, 0)))
    dskip_x = jnp.repeat(d_skip, SSM_HEAD_DIM).reshape(1, -1)
    o_ssm = _ssd(xs, bm, cm, dt, gzs, alog, dskip_x, ssm_norm_g.reshape(1, -1), e2,
                 rows=512, nheads=nheads)

    alpha = (2.0 * depth) ** 0.25
    return _out_proj(o_attn, gza, o_ssm, x2, w_out.astype(BF16), mod3,
                     ln_g.reshape(1, -1), ln_b.reshape(1, -1), tm=OUT_ROWS, sub=OUT_SUB, alpha=alpha)


def kernel(x, c, positions, w_ada, b_ada, w_in, q_norm_g, w_qb, kv_norm_g, w_kvb, conv_w, conv_b,
           dt_bias, a_log, d_skip, ssm_norm_g, w_out, ln_g, ln_b):
    b, s, d = x.shape
    depth = w_in.shape[0]
    assert b == 1, "one sequence per call"
    h = x.reshape(s, d)
    pos_row = positions.reshape(1, s)
    for l in range(depth):
        h = _layer(h, c, pos_row, w_ada[l], b_ada[l], w_in[l], q_norm_g[l], w_qb[l], kv_norm_g[l],
                   w_kvb[l], conv_w[l], conv_b[l], dt_bias[l], a_log[l], d_skip[l], ssm_norm_g[l],
                   w_out[l], ln_g[l], ln_b[l], depth=depth)
    return h.reshape(b, s, d)
```

```python
import functools
import math

import jax
import jax.numpy as jnp
from jax import lax
from jax.experimental import pallas as pl
from jax.experimental.pallas import tpu as pltpu

F32 = jnp.float32
BF16 = jnp.bfloat16

MLA_HEADS = 8
QK_NOPE_DIM = 128
QK_ROPE_DIM = 64
V_HEAD_DIM = 128
ROPE_THETA = 10000.0
SSM_HEAD_DIM = 64
SSM_GROUPS = 2
SSM_STATE = 128
CHUNK = 128
RMS_EPS = 1e-6
LN_EPS = 1e-5

LANES = 128
SUBLANES = 8
MXU_DIM = 256
QK_PAD = MXU_DIM
VMEM_LIMIT = 56 * 1024 * 1024

IN_ROWS = 512
IN_SUB = 256
OUT_ROWS = 1024
OUT_SUB = 256

LOG2E = 1.4426950408889634
NEG_BIG = -1e30


def _silu(z):
    return z * (1.0 / (1.0 + jnp.exp(-z)))


def _softplus(z):
    return jnp.maximum(z, 0.0) + jnp.log1p(jnp.exp(-jnp.abs(z)))


def _split_bf16(x, parts):
    out, rem = [], x
    for _ in range(parts):
        hi = rem.astype(BF16)
        out.append(hi)
        rem = rem - hi.astype(F32)
    return out


def _mod_kernel(c_ref, w_ref, b_ref, o_ref):
    o_ref[...] = jnp.sum(w_ref[...] * c_ref[...], axis=0, keepdims=True) + b_ref[...]


def _adaln_mod(c_col, w_ada, b_ada):
    d, n = w_ada.shape
    bn = 1024
    return pl.pallas_call(
        _mod_kernel,
        grid=(n // bn,),
        in_specs=[pl.BlockSpec((d, 1), lambda j: (0, 0)),
                  pl.BlockSpec((d, bn), lambda j: (0, j)),
                  pl.BlockSpec((1, bn), lambda j: (0, j))],
        out_specs=pl.BlockSpec((1, bn), lambda j: (0, j)),
        out_shape=jax.ShapeDtypeStruct((1, n), F32),
        name="adaln_mod",
    )(c_col, w_ada, b_ada)


def _in_kernel(x_ref, mod_ref, pos_ref, rope_ref, w_in_ref, qg_ref, w_qb_ref, kvg_ref, w_k_ref, w_vt_ref,
               conv_w_ref, conv_b_ref, dtb_ref,
               q_out, k_out, vt_out, gza_out, xs_out, bm_out, cm_out, dt_out, gzs_out,
               xbuf, *, tm, sub, off, q_scale):
    i = pl.program_id(0)
    shift = mod_ref[0:1, :]
    scale1 = 1.0 + mod_ref[1:2, :]
    lane = lax.broadcasted_iota(jnp.int32, (sub, LANES), 1)
    low_half = lane < QK_ROPE_DIM
    nconv, nch = conv_w_ref.shape
    nx = xs_out.shape[1]
    nb = bm_out.shape[1]

    @pl.when(i == 0)
    def _():
        xbuf[0:SUBLANES, :] = jnp.zeros((SUBLANES, nch), F32)

    projs = []
    for h in range(tm // sub):
        u = x_ref[h * sub:(h + 1) * sub, :] * scale1 + shift
        projs.append(jnp.dot(u.astype(BF16), w_in_ref[...], preferred_element_type=F32))

    for h, proj in enumerate(projs):
        rows = slice(h * sub, (h + 1) * sub)

        ang_t = pos_ref[:, rows].astype(F32) * rope_ref[0] + rope_ref[1]
        cs = jnp.cos(ang_t).T

        def rope(rr):
            t = rr * cs
            return jnp.where(low_half, t + pltpu.roll(t, QK_ROPE_DIM, axis=1), 0.0)

        q_lat = proj[:, off["q"]:off["q"] + qg_ref.shape[1]]
        qn = q_lat * lax.rsqrt(jnp.mean(q_lat * q_lat, axis=-1, keepdims=True) + RMS_EPS) * qg_ref[...]
        qf = jnp.dot(qn.astype(BF16), w_qb_ref[...], preferred_element_type=F32)
        for hd in range(MLA_HEADS):
            c0 = hd * QK_PAD
            q_out[rows, c0:c0 + QK_NOPE_DIM] = (qf[:, c0:c0 + QK_NOPE_DIM] * q_scale).astype(BF16)
            q_out[rows, c0 + QK_NOPE_DIM:c0 + QK_PAD] = (
                rope(qf[:, c0 + QK_NOPE_DIM:c0 + QK_PAD]) * q_scale).astype(BF16)

        c_kv = proj[:, off["ckv"]:off["ckv"] + kvg_ref.shape[1]]
        ckvn = c_kv * lax.rsqrt(jnp.mean(c_kv * c_kv, axis=-1, keepdims=True) + RMS_EPS) * kvg_ref[...]
        ckvn_bf = ckvn.astype(BF16)
        kf = jnp.dot(ckvn_bf, w_k_ref[...], preferred_element_type=F32)
        vt_out[:, rows] = lax.dot_general(w_vt_ref[...], ckvn_bf, (((1,), (1,)), ((), ())),
                                          preferred_element_type=F32).astype(BF16)
        k_rope = rope(proj[:, off["krope"]:off["krope"] + LANES]).astype(BF16)
        for hd in range(MLA_HEADS):
            k_out[rows, hd * QK_PAD:hd * QK_PAD + QK_NOPE_DIM] = (
                kf[:, hd * QK_NOPE_DIM:(hd + 1) * QK_NOPE_DIM].astype(BF16))
            k_out[rows, hd * QK_PAD + QK_NOPE_DIM:(hd + 1) * QK_PAD] = k_rope

        gza_out[rows, :] = _silu(proj[:, off["za"]:off["za"] + gza_out.shape[1]]).astype(BF16)
        gzs_out[rows, :] = _silu(proj[:, off["zs"]:off["zs"] + gzs_out.shape[1]]).astype(BF16)
        dt_out[rows, :] = _softplus(proj[:, off["dt"]:off["dt"] + LANES] + dtb_ref[...])

        r1 = SUBLANES + h * sub
        xbuf[r1:r1 + sub, :] = proj[:, off["xbc"]:off["xbc"] + nch]
        acc = conv_b_ref[...] + conv_w_ref[nconv - 1:nconv, :] * xbuf[r1:r1 + sub, :]
        for j in range(nconv - 1):
            r0 = r1 - (nconv - 1) + j
            acc = acc + conv_w_ref[j:j + 1, :] * xbuf[r0:r0 + sub, :]
        xc = _silu(acc)
        xs_out[rows, :] = xc[:, :nx].astype(BF16)
        bm_out[rows, :] = xc[:, nx:nx + nb].astype(BF16)
        cm_out[rows, :] = xc[:, nx + nb:nx + 2 * nb].astype(BF16)

    xbuf[0:SUBLANES, :] = xbuf[tm:tm + SUBLANES, :]


def _in_proj(x2, mod3, pos_row, rope_tab, w_in_p, qg, w_qb_p, kvg, w_k, w_vt, conv_w, conv_b, dtb,
             *, off, tm, sub, q_scale):
    s, d = x2.shape
    nq = MLA_HEADS * QK_PAD
    nv = MLA_HEADS * V_HEAD_DIM
    nch = conv_w.shape[1]
    nbc = SSM_GROUPS * SSM_STATE
    nx = nch - 2 * nbc
    row = lambda w: pl.BlockSpec((tm, w), lambda i: (i, 0))
    full = lambda a: pl.BlockSpec(a.shape, lambda i: (0,) * a.ndim, pipeline_mode=pl.Buffered(1))
    outs = [
        jax.ShapeDtypeStruct((s, nq), BF16),
        jax.ShapeDtypeStruct((s, nq), BF16),
        jax.ShapeDtypeStruct((nv, s), BF16),
        jax.ShapeDtypeStruct((s, nv), BF16),
        jax.ShapeDtypeStruct((s, nx), BF16),
        jax.ShapeDtypeStruct((s, nbc), BF16),
        jax.ShapeDtypeStruct((s, nbc), BF16),
        jax.ShapeDtypeStruct((s, LANES), F32),
        jax.ShapeDtypeStruct((s, nx), BF16),
    ]
    return pl.pallas_call(
        functools.partial(_in_kernel, tm=tm, sub=sub, off=off, q_scale=q_scale),
        grid=(s // tm,),
        in_specs=[row(d), full(mod3), pl.BlockSpec((1, tm), lambda i: (0, i)), full(rope_tab),
                  full(w_in_p), full(qg), full(w_qb_p), full(kvg), full(w_k), full(w_vt),
                  full(conv_w), full(conv_b), full(dtb)],
        out_specs=[pl.BlockSpec((nv, tm), lambda i: (0, i)) if n == 2 else row(o.shape[1])
                   for n, o in enumerate(outs)],
        out_shape=outs,
        scratch_shapes=[pltpu.VMEM((tm + 2 * SUBLANES, nch), F32)],
        compiler_params=pltpu.CompilerParams(dimension_semantics=("arbitrary",),
                                             vmem_limit_bytes=VMEM_LIMIT),
        name="in_proj",
    )(x2, mod3, pos_row, rope_tab, w_in_p, qg, w_qb_p, kvg, w_k, w_vt, conv_w, conv_b, dtb)


def _sublane_allmax(x):
    shift = SUBLANES // 2
    while shift:
        x = jnp.maximum(x, pltpu.roll(x, shift, axis=0))
        shift //= 2
    return x


def _sublane_allsum(x):
    shift = SUBLANES // 2
    while shift:
        x = x + pltpu.roll(x, shift, axis=0)
        shift //= 2
    return x


BF16_ROWS = 16
ACC_ROWS = V_HEAD_DIM + BF16_ROWS
EXP_ROWS = 32


def _attn_kernel(q_ref, k_ref, vt_ref, o_ref, m_a, acc_a, m_b, acc_b,
                 s_a, s_b, cmax_a, cmax_b, p_a, p_b, al_a, al_b, *, tq, tk):
    assert tq == tk
    g = pl.program_id(1)
    nsub = tk // SUBLANES
    nacc = ACC_ROWS // SUBLANES
    buf_a = (s_a, cmax_a, p_a, al_a)
    buf_b = (s_b, cmax_b, p_b, al_b)
    rows_a, rows_b = slice(0, tq), slice(tq, 2 * tq)
    ones_rows = jnp.ones((ACC_ROWS - V_HEAD_DIM, tk), BF16)

    def reset(stats):
        m_scr, acc_scr = stats
        m_scr[...] = jnp.full(m_scr.shape, NEG_BIG, F32)
        acc_scr[...] = jnp.zeros(acc_scr.shape, F32)

    def qk_stage(kv, q_rows, buf, diag=False):
        s_out, cmax_out = buf[0], buf[1]
        start = pl.multiple_of(kv * tk, tk)
        k = k_ref[pl.ds(start, tk), :]
        st = lax.dot_general(k, q_ref[q_rows, :], (((1,), (1,)), ((), ())),
                             preferred_element_type=F32)
        if diag:
            key = lax.broadcasted_iota(jnp.int32, (tk, tq), 0)
            qry = lax.broadcasted_iota(jnp.int32, (tk, tq), 1)
            st = jnp.where(key <= qry, st, NEG_BIG)
        s_out[...] = st
        cmax_out[...] = _sublane_allmax(jnp.max(st.reshape(nsub, SUBLANES, tq), axis=0))

    def softmax_stage(buf, stats):
        s_in, cmax_in, p_out, al_out = buf
        m_scr = stats[0]
        m_old = m_scr[...]
        m_new = jnp.maximum(m_old, cmax_in[...])
        al_out[...] = jnp.exp2(m_old - m_new)
        m_scr[...] = m_new
        for c in range(0, tq, MXU_DIM):
            m_c = m_new[:, c:c + MXU_DIM][None]
            for r in range(0, tk, EXP_ROWS):
                sc = s_in[r:r + EXP_ROWS, c:c + MXU_DIM].reshape(EXP_ROWS // SUBLANES, SUBLANES, MXU_DIM)
                p_out[r:r + EXP_ROWS, c:c + MXU_DIM] = (
                    jnp.exp2(sc - m_c).reshape(EXP_ROWS, MXU_DIM).astype(BF16))

    def pv_stage(kv, buf, stats):
        p_in, al_in = buf[2], buf[3]
        acc_scr = stats[1]
        start = pl.multiple_of(jnp.maximum(kv, 0) * tk, tk)
        vt_aug = jnp.concatenate([vt_ref[:, pl.ds(start, tk)], ones_rows], axis=0)
        pv = jnp.dot(vt_aug, p_in[...], preferred_element_type=F32)
        acc3 = acc_scr[...].reshape(nacc, SUBLANES, tq) * al_in[...][None]
        acc_scr[...] = acc3.reshape(ACC_ROWS, tq) + pv

    def finalize(q_rows, stats):
        acc_scr = stats[1]
        inv_l = 1.0 / acc_scr[V_HEAD_DIM:V_HEAD_DIM + SUBLANES, :]
        o_t = acc_scr[0:V_HEAD_DIM, :].reshape(V_HEAD_DIM // SUBLANES, SUBLANES, tq) * inv_l[None]
        o_ref[q_rows, :] = o_t.reshape(V_HEAD_DIM, tq).T.astype(o_ref.dtype)

    def make_step(qi, q_rows, stats):
        kv_of = lambda pos: jnp.where(pos == 0, qi, pos - 1)

        def step(pos, buf, other):
            qk_stage(kv_of(pos), q_rows, buf)
            pv_stage(kv_of(pos - 2), buf, stats)
            softmax_stage(other, stats)

        return kv_of, step

    qa = 2 * g
    qb = qa + 1
    st_a, st_b = (m_a, acc_a), (m_b, acc_b)
    kv_a, step_a = make_step(qa, rows_a, st_a)
    kv_b, step_b = make_step(qb, rows_b, st_b)
    reset(st_a)
    reset(st_b)
    qk_stage(qa, rows_a, buf_a, diag=True)

    @pl.when(g == 0)
    def _():
        p_b[...] = jnp.zeros(p_b.shape, BF16)
        al_b[...] = jnp.ones(al_b.shape, F32)

    @pl.when(g > 0)
    def _():
        qk_stage(kv_a(1), rows_a, buf_b)
        softmax_stage(buf_a, st_a)
        step_a(2, buf_a, buf_b)

        def body(jj, carry):
            step_a(2 * jj + 1, buf_b, buf_a)
            step_a(2 * jj + 2, buf_a, buf_b)
            return carry

        lax.fori_loop(1, g, body, 0)

    pv_stage(kv_a(qa - 1), buf_b, st_a)
    qk_stage(qb, rows_b, buf_b, diag=True)
    softmax_stage(buf_a, st_a)
    pv_stage(kv_a(qa), buf_a, st_a)
    qk_stage(kv_b(1), rows_b, buf_a)
    softmax_stage(buf_b, st_b)
    finalize(rows_a, st_a)

    def body_b(jj, carry):
        step_b(2 * jj, buf_b, buf_a)
        step_b(2 * jj + 1, buf_a, buf_b)
        return carry

    lax.fori_loop(1, g + 1, body_b, 0)
    pv_stage(kv_b(qb - 1), buf_b, st_b)
    softmax_stage(buf_a, st_b)
    pv_stage(kv_b(qb), buf_a, st_b)
    finalize(rows_b, st_b)


def _attention(q, k, vt, *, tq, tk):
    s = q.shape[0]
    stat = pltpu.VMEM((SUBLANES, tq), F32)
    return pl.pallas_call(
        functools.partial(_attn_kernel, tq=tq, tk=tk),
        grid=(MLA_HEADS, s // (2 * tq)),
        in_specs=[pl.BlockSpec((2 * tq, QK_PAD), lambda h, i: (i, h)),
                  pl.BlockSpec((s, QK_PAD), lambda h, i: (0, h)),
                  pl.BlockSpec((V_HEAD_DIM, s), lambda h, i: (h, 0))],
        out_specs=pl.BlockSpec((2 * tq, V_HEAD_DIM), lambda h, i: (i, h)),
        out_shape=jax.ShapeDtypeStruct((s, MLA_HEADS * V_HEAD_DIM), BF16),
        scratch_shapes=[stat, pltpu.VMEM((ACC_ROWS, tq), F32), stat, pltpu.VMEM((ACC_ROWS, tq), F32),
                        pltpu.VMEM((tk, tq), F32), pltpu.VMEM((tk, tq), F32), stat, stat,
                        pltpu.VMEM((tk, tq), BF16), pltpu.VMEM((tk, tq), BF16), stat, stat],
        compiler_params=pltpu.CompilerParams(dimension_semantics=("arbitrary", "arbitrary"),
                                             vmem_limit_bytes=VMEM_LIMIT),
        name="attention",
    )(q, k, vt)


def _ssd_kernel(xs_ref, bm_ref, cm_ref, dt_ref, gz_ref, alog_ref, dskip_ref, g_ref, e2_ref,
                o_ref, state, y_scr, *, rows, nheads):
    i = pl.program_id(0)
    gw = SSM_STATE
    hw = state.shape[2]

    @pl.when(i == 0)
    def _():
        state[...] = jnp.zeros(state.shape, F32)

    a_neg = -jnp.exp(alog_ref[...])
    r_i = lax.broadcasted_iota(jnp.int32, (CHUNK, CHUNK), 0)
    c_i = lax.broadcasted_iota(jnp.int32, (CHUNK, CHUNK), 1)
    tri = c_i <= r_i
    tri_bf = tri.astype(BF16)
    lane = lax.broadcasted_iota(jnp.int32, (CHUNK, LANES), 1)
    first_copy = lane < nheads
    head_lo = lane < SSM_HEAD_DIM
    e2 = e2_ref[...]

    def expand(v):
        hi = v.astype(BF16)
        lo = (v - hi.astype(F32)).astype(BF16)
        return jnp.dot(jnp.where(first_copy, hi, lo), e2, preferred_element_type=F32)

    for cidx in range(rows // CHUNK):
        sl = pl.ds(cidx * CHUNK, CHUNK)
        dt = dt_ref[sl, :]
        da = dt * a_neg
        a_cum = jnp.zeros((CHUNK, LANES), F32)
        for part in _split_bf16(da, 3):
            a_cum = a_cum + jnp.dot(tri_bf, part, preferred_element_type=F32)
        a_cum_t = a_cum.T
        a_last = a_cum[CHUNK - 1:CHUNK, :]
        ea = jnp.exp(a_cum)
        dte = jnp.exp(a_last - a_cum)

        xs = xs_ref[sl, :].astype(F32)
        xd = xs * expand(dt)
        ea_x = expand(ea)
        xdd = (xd * expand(dte)).astype(BF16)
        xd_bf = xd.astype(BF16)

        for g in range(SSM_GROUPS):
            bg = bm_ref[sl, g * gw:(g + 1) * gw]
            cg = cm_ref[sl, g * gw:(g + 1) * gw]
            cb = lax.dot_general(cg, bg, (((1,), (1,)), ((), ())), preferred_element_type=F32)
            prev = state[g]
            y_off = jnp.dot(cg, prev.astype(BF16), preferred_element_type=F32)
            new = lax.dot_general(bg, xdd[:, g * hw:(g + 1) * hw], (((0,), (0,)), ((), ())),
                                  preferred_element_type=F32)
            state[g] = prev * ea_x[CHUNK - 1:CHUNK, g * hw:(g + 1) * hw] + new
            y_scr[:, g * hw:(g + 1) * hw] = y_off * ea_x[:, g * hw:(g + 1) * hw]

            hpg = hw // SSM_HEAD_DIM
            for pair in range(hpg // 2):
                h0 = g * hpg + 2 * pair
                c0 = h0 * SSM_HEAD_DIM
                xp = xd_bf[:, c0:c0 + LANES]
                yp = jnp.zeros((CHUNK, LANES), F32)
                for k, keep in ((0, head_lo), (1, ~head_lo)):
                    h = h0 + k
                    seg = a_cum[:, h:h + 1] - a_cum_t[h:h + 1, :]
                    m_h = (cb * jnp.exp(jnp.where(tri, seg, NEG_BIG))).astype(BF16)
                    yp = yp + jnp.dot(m_h, jnp.where(keep, xp, jnp.zeros_like(xp)),
                                      preferred_element_type=F32)
                y_scr[:, c0:c0 + LANES] = y_scr[:, c0:c0 + LANES] + yp

        y = y_scr[...] + xs * dskip_ref[...]
        hf = y * gz_ref[sl, :].astype(F32)
        for g in range(SSM_GROUPS):
            hg = hf[:, g * hw:(g + 1) * hw]
            ms = jnp.mean(hg * hg, axis=-1, keepdims=True)
            o_ref[sl, g * hw:(g + 1) * hw] = (hg * lax.rsqrt(ms + RMS_EPS)
                                               * g_ref[:, g * hw:(g + 1) * hw]).astype(o_ref.dtype)


def _ssd(xs, bm, cm, dt, gzs, alog, dskip_x, norm_g, e2, *, rows, nheads):
    s, nx = xs.shape
    hw = nx // SSM_GROUPS
    row = lambda a: pl.BlockSpec((rows, a.shape[1]), lambda i: (i, 0))
    full = lambda a: pl.BlockSpec(a.shape, lambda i: (0,) * a.ndim)
    return pl.pallas_call(
        functools.partial(_ssd_kernel, rows=rows, nheads=nheads),
        grid=(s // rows,),
        in_specs=[row(xs), row(bm), row(cm), row(dt), row(gzs),
                  full(alog), full(dskip_x), full(norm_g), full(e2)],
        out_specs=pl.BlockSpec((rows, nx), lambda i: (i, 0)),
        out_shape=jax.ShapeDtypeStruct((s, nx), BF16),
        scratch_shapes=[pltpu.VMEM((SSM_GROUPS, SSM_STATE, hw), F32),
                        pltpu.VMEM((CHUNK, nx), F32)],
        compiler_params=pltpu.CompilerParams(dimension_semantics=("arbitrary",),
                                             vmem_limit_bytes=VMEM_LIMIT),
        name="ssd",
    )(xs, bm, cm, dt, gzs, alog, dskip_x, norm_g, e2)


def _out_kernel(oa_ref, gza_ref, os_ref, x_ref, w32_ref, mod_ref, lng_ref, lnb_ref, o_ref, w_ref,
                *, alpha, sub):
    na = oa_ref.shape[1]

    @pl.when(pl.program_id(0) == 0)
    def _():
        for r in range(0, w_ref.shape[0], sub):
            w_ref[r:r + sub, :] = w32_ref[r:r + sub, :].astype(BF16)

    mixes = []
    for r in range(0, o_ref.shape[0], sub):
        rows = slice(r, r + sub)
        oa = (oa_ref[rows, :].astype(F32) * gza_ref[rows, :].astype(F32)).astype(BF16)
        mixed = jnp.dot(oa, w_ref[0:na, :], preferred_element_type=F32)
        mixes.append(mixed + jnp.dot(os_ref[rows, :], w_ref[na:, :], preferred_element_type=F32))
    for n, mixed in enumerate(mixes):
        rows = slice(n * sub, (n + 1) * sub)
        y = alpha * x_ref[rows, :] + mod_ref[2:3, :] * mixed
        mu = jnp.mean(y, axis=-1, keepdims=True)
        yc = y - mu
        var = jnp.mean(yc * yc, axis=-1, keepdims=True)
        o_ref[rows, :] = yc * lax.rsqrt(var + LN_EPS) * lng_ref[...] + lnb_ref[...]


def _out_proj(oa, gza, o_ssm, x2, w_out, mod3, ln_g, ln_b, *, tm, sub, alpha):
    s, d = x2.shape
    row = lambda a: pl.BlockSpec((tm, a.shape[1]), lambda i: (i, 0))
    full = lambda a: pl.BlockSpec(a.shape, lambda i: (0,) * a.ndim)
    return pl.pallas_call(
        functools.partial(_out_kernel, alpha=alpha, sub=sub),
        grid=(s // tm,),
        in_specs=[row(oa), row(gza), row(o_ssm), row(x2),
                  pl.BlockSpec(w_out.shape, lambda i: (0, 0), pipeline_mode=pl.Buffered(1)),
                  full(mod3), full(ln_g), full(ln_b)],
        out_specs=pl.BlockSpec((tm, d), lambda i: (i, 0)),
        out_shape=jax.ShapeDtypeStruct((s, d), F32),
        scratch_shapes=[pltpu.VMEM(w_out.shape, BF16)],
        compiler_params=pltpu.CompilerParams(dimension_semantics=("arbitrary",),
                                             vmem_limit_bytes=VMEM_LIMIT),
        name="out_proj",
    )(oa, gza, o_ssm, x2, w_out, mod3, ln_g, ln_b)


def _pad_cols(a, width):
    return jnp.pad(a, ((0, 0), (0, width - a.shape[1])))


def _regroup_kernel(w_ref, o_ref, *, segments):
    w = w_ref[...]
    parts = [jnp.zeros((w.shape[0], b), w.dtype) if a is None else w[:, a:b] for a, b in segments]
    o_ref[...] = jnp.concatenate(parts, axis=1).astype(o_ref.dtype)


def _regroup_columns(w, segments, width):
    rows = w.shape[0]
    br = 256
    return pl.pallas_call(
        functools.partial(_regroup_kernel, segments=segments),
        grid=(rows // br,),
        in_specs=[pl.BlockSpec((br, w.shape[1]), lambda i: (i, 0))],
        out_specs=pl.BlockSpec((br, width), lambda i: (i, 0)),
        out_shape=jax.ShapeDtypeStruct((rows, width), BF16),
        name="regroup_w_in",
    )(w)


def _layer(x2, c, pos_row, w_ada, b_ada, w_in, q_norm_g, w_qb, kv_norm_g, w_kvb,
           conv_w, conv_b, dt_bias, a_log, d_skip, ssm_norm_g, w_out, ln_g, ln_b, *, depth):
    s, d = x2.shape
    q_rank = q_norm_g.shape[0]
    kv_rank = kv_norm_g.shape[0]
    nheads = dt_bias.shape[0]
    nch = conv_w.shape[1]
    nv = MLA_HEADS * V_HEAD_DIM
    nx = nheads * SSM_HEAD_DIM
    half = QK_ROPE_DIM // 2

    mod = _adaln_mod(c.reshape(d, 1), w_ada, b_ada.reshape(1, -1))
    mod3 = mod.reshape(3, d)

    o_q, o_ckv = 0, q_rank
    o_kr = o_ckv + kv_rank
    o_za = o_kr + QK_ROPE_DIM
    o_xbc = o_za + nv
    o_dt = o_xbc + nch
    o_zs = o_dt + nheads
    dt_pad = LANES - 2 * nheads
    groups = [("q", [(o_q, o_ckv)]), ("ckv", [(o_ckv, o_kr)]),
              ("krope", [(o_kr, o_za), (o_kr + half, o_za), (o_kr, o_kr + half)]),
              ("za", [(o_za, o_xbc)]), ("xbc", [(o_xbc, o_dt)]), ("zs", [(o_zs, o_zs + nx)]),
              ("dt", [(o_dt, o_zs), (o_dt, o_zs), (None, dt_pad)])]
    off, cur, segments = {}, 0, []
    for name, segs in groups:
        assert cur % LANES == 0
        off[name] = cur
        for a, b in segs:
            cur += b if a is None else b - a
            if segments and a is not None and segments[-1][0] is not None and segments[-1][1] == a:
                segments[-1] = (segments[-1][0], b)
            else:
                segments.append((a, b))
    assert cur % LANES == 0
    w_in_p = _regroup_columns(w_in, tuple(segments), cur)

    w3 = w_qb.reshape(q_rank, MLA_HEADS, QK_NOPE_DIM + QK_ROPE_DIM)
    w_rope = w3[:, :, QK_NOPE_DIM:]
    w_qb_p = jnp.concatenate([w3, w_rope[:, :, half:], w_rope[:, :, :half]], axis=2)
    w_qb_p = w_qb_p.reshape(q_rank, MLA_HEADS * QK_PAD).astype(BF16)

    inv_freq = 1.0 / (ROPE_THETA ** (jnp.arange(half, dtype=F32) / half))
    freq = jnp.tile(inv_freq, LANES // half)
    phase = jnp.concatenate([jnp.zeros((2 * half,), F32), jnp.full((half,), 0.5 * math.pi, F32),
                             jnp.full((half,), -0.5 * math.pi, F32)])
    rope_tab = jnp.broadcast_to(jnp.stack([freq, phase])[:, :, None], (2, LANES, IN_SUB))

    dtb = _pad_cols(jnp.concatenate([dt_bias, dt_bias]).reshape(1, -1), LANES)
    q_scale = (QK_NOPE_DIM + QK_ROPE_DIM) ** -0.5 * LOG2E

    wkv3 = w_kvb.reshape(kv_rank, MLA_HEADS, QK_NOPE_DIM + V_HEAD_DIM)
    w_k = wkv3[:, :, :QK_NOPE_DIM].reshape(kv_rank, MLA_HEADS * QK_NOPE_DIM).astype(BF16)
    w_vt = wkv3[:, :, QK_NOPE_DIM:].reshape(kv_rank, nv).T.astype(BF16)

    q, k, vt, gza, xs, bm, cm, dt, gzs = _in_proj(
        x2, mod3, pos_row, rope_tab, w_in_p, q_norm_g.reshape(1, -1), w_qb_p,
        kv_norm_g.reshape(1, -1), w_k, w_vt, conv_w, conv_b.reshape(1, -1), dtb,
        off=off, tm=IN_ROWS, sub=IN_SUB, q_scale=q_scale)

    o_attn = _attention(q, k, vt, tq=1024, tk=1024)

    alog = _pad_cols(jnp.concatenate([a_log, a_log]).reshape(1, -1), LANES)
    e_head = jnp.repeat(jnp.eye(nheads, dtype=BF16), SSM_HEAD_DIM, axis=1)
    e2 = jnp.pad(jnp.concatenate([e_head, e_head], axis=0), ((0, LANES - 2 * nheads), (0, 0)))
    dskip_x = jnp.repeat(d_skip, SSM_HEAD_DIM).reshape(1, -1)
    o_ssm = _ssd(xs, bm, cm, dt, gzs, alog, dskip_x, ssm_norm_g.reshape(1, -1), e2,
                 rows=512, nheads=nheads)

    alpha = (2.0 * depth) ** 0.25
    return _out_proj(o_attn, gza, o_ssm, x2, w_out, mod3,
                     ln_g.reshape(1, -1), ln_b.reshape(1, -1), tm=OUT_ROWS, sub=OUT_SUB, alpha=alpha)


def kernel(x, c, positions, w_ada, b_ada, w_in, q_norm_g, w_qb, kv_norm_g, w_kvb, conv_w, conv_b,
           dt_bias, a_log, d_skip, ssm_norm_g, w_out, ln_g, ln_b):
    b, s, d = x.shape
    depth = w_in.shape[0]
    assert b == 1, "one sequence per call"
    h = x.reshape(s, d)
    pos_row = positions.reshape(1, s)
    for l in range(depth):
        h = _layer(h, c, pos_row, w_ada[l], b_ada[l], w_in[l], q_norm_g[l], w_qb[l], kv_norm_g[l],
                   w_kvb[l], conv_w[l], conv_b[l], dt_bias[l], a_log[l], d_skip[l], ssm_norm_g[l],
                   w_out[l], ln_g[l], ln_b[l], depth=depth)
    return h.reshape(b, s, d)
```

```python
import functools
import math

import jax
import jax.numpy as jnp
from jax import lax
from jax.experimental import pallas as pl
from jax.experimental.pallas import tpu as pltpu

F32 = jnp.float32
BF16 = jnp.bfloat16

MLA_HEADS = 8
QK_NOPE_DIM = 128
QK_ROPE_DIM = 64
V_HEAD_DIM = 128
ROPE_THETA = 10000.0
SSM_HEAD_DIM = 64
SSM_GROUPS = 2
SSM_STATE = 128
CHUNK = 128
RMS_EPS = 1e-6
LN_EPS = 1e-5

LANES = 128
SUBLANES = 8
MXU_DIM = 256
QK_PAD = MXU_DIM
VMEM_LIMIT = 56 * 1024 * 1024

IN_ROWS = 512
IN_SUB = 256
OUT_ROWS = 1024
OUT_SUB = 256

LOG2E = 1.4426950408889634
NEG_BIG = -1e30


def _silu(z):
    return z * (1.0 / (1.0 + jnp.exp(-z)))


def _softplus(z):
    return jnp.maximum(z, 0.0) + jnp.log1p(jnp.exp(-jnp.abs(z)))


def _split_bf16(x, parts):
    out, rem = [], x
    for _ in range(parts):
        hi = rem.astype(BF16)
        out.append(hi)
        rem = rem - hi.astype(F32)
    return out


def _mod_kernel(c_ref, w_ref, b_ref, o_ref):
    o_ref[...] = jnp.sum(w_ref[...] * c_ref[...], axis=0, keepdims=True) + b_ref[...]


def _adaln_mod(c_col, w_ada, b_ada):
    d, n = w_ada.shape
    bn = 1024
    return pl.pallas_call(
        _mod_kernel,
        grid=(n // bn,),
        in_specs=[pl.BlockSpec((d, 1), lambda j: (0, 0)),
                  pl.BlockSpec((d, bn), lambda j: (0, j)),
                  pl.BlockSpec((1, bn), lambda j: (0, j))],
        out_specs=pl.BlockSpec((1, bn), lambda j: (0, j)),
        out_shape=jax.ShapeDtypeStruct((1, n), F32),
        name="adaln_mod",
    )(c_col, w_ada, b_ada)


def _in_kernel(x_ref, mod_ref, pos_ref, rope_ref, w_in_ref, qg_ref, w_qb_ref, kvg_ref, w_k_ref, w_vt_ref,
               conv_w_ref, conv_b_ref, dtb_ref,
               q_out, k_out, vt_out, gza_out, xs_out, bm_out, cm_out, dt_out, gzs_out,
               xbuf, *, tm, sub, off, q_scale):
    i = pl.program_id(0)
    shift = mod_ref[0:1, :]
    scale1 = 1.0 + mod_ref[1:2, :]
    lane = lax.broadcasted_iota(jnp.int32, (sub, LANES), 1)
    low_half = lane < QK_ROPE_DIM
    nconv, nch = conv_w_ref.shape
    nx = xs_out.shape[1]
    nb = bm_out.shape[1]

    @pl.when(i == 0)
    def _():
        xbuf[0:SUBLANES, :] = jnp.zeros((SUBLANES, nch), F32)

    projs = []
    for h in range(tm // sub):
        u = x_ref[h * sub:(h + 1) * sub, :] * scale1 + shift
        projs.append(jnp.dot(u.astype(BF16), w_in_ref[...], preferred_element_type=F32))

    for h, proj in enumerate(projs):
        rows = slice(h * sub, (h + 1) * sub)

        ang_t = pos_ref[:, rows].astype(F32) * rope_ref[0] + rope_ref[1]
        cs = jnp.cos(ang_t).T

        def rope(rr):
            t = rr * cs
            return jnp.where(low_half, t + pltpu.roll(t, QK_ROPE_DIM, axis=1), 0.0)

        q_lat = proj[:, off["q"]:off["q"] + qg_ref.shape[1]]
        qn = q_lat * lax.rsqrt(jnp.mean(q_lat * q_lat, axis=-1, keepdims=True) + RMS_EPS) * qg_ref[...]
        qf = jnp.dot(qn.astype(BF16), w_qb_ref[...], preferred_element_type=F32)
        for hd in range(MLA_HEADS):
            c0 = hd * QK_PAD
            q_out[rows, c0:c0 + QK_NOPE_DIM] = (qf[:, c0:c0 + QK_NOPE_DIM] * q_scale).astype(BF16)
            q_out[rows, c0 + QK_NOPE_DIM:c0 + QK_PAD] = (
                rope(qf[:, c0 + QK_NOPE_DIM:c0 + QK_PAD]) * q_scale).astype(BF16)

        c_kv = proj[:, off["ckv"]:off["ckv"] + kvg_ref.shape[1]]
        ckvn = c_kv * lax.rsqrt(jnp.mean(c_kv * c_kv, axis=-1, keepdims=True) + RMS_EPS) * kvg_ref[...]
        ckvn_bf = ckvn.astype(BF16)
        kf = jnp.dot(ckvn_bf, w_k_ref[...], preferred_element_type=F32)
        vt_out[:, rows] = lax.dot_general(w_vt_ref[...], ckvn_bf, (((1,), (1,)), ((), ())),
                                          preferred_element_type=F32).astype(BF16)
        k_rope = rope(proj[:, off["krope"]:off["krope"] + LANES]).astype(BF16)
        for hd in range(MLA_HEADS):
            k_out[rows, hd * QK_PAD:hd * QK_PAD + QK_NOPE_DIM] = (
                kf[:, hd * QK_NOPE_DIM:(hd + 1) * QK_NOPE_DIM].astype(BF16))
            k_out[rows, hd * QK_PAD + QK_NOPE_DIM:(hd + 1) * QK_PAD] = k_rope

        gza_out[rows, :] = _silu(proj[:, off["za"]:off["za"] + gza_out.shape[1]]).astype(BF16)
        gzs_out[rows, :] = _silu(proj[:, off["zs"]:off["zs"] + gzs_out.shape[1]]).astype(BF16)
        dt_out[rows, :] = _softplus(proj[:, off["dt"]:off["dt"] + LANES] + dtb_ref[...])

        r1 = SUBLANES + h * sub
        xbuf[r1:r1 + sub, :] = proj[:, off["xbc"]:off["xbc"] + nch]
        acc = conv_b_ref[...] + conv_w_ref[nconv - 1:nconv, :] * xbuf[r1:r1 + sub, :]
        for j in range(nconv - 1):
            r0 = r1 - (nconv - 1) + j
            acc = acc + conv_w_ref[j:j + 1, :] * xbuf[r0:r0 + sub, :]
        xc = _silu(acc)
        xs_out[rows, :] = xc[:, :nx].astype(BF16)
        bm_out[rows, :] = xc[:, nx:nx + nb].astype(BF16)
        cm_out[rows, :] = xc[:, nx + nb:nx + 2 * nb].astype(BF16)

    xbuf[0:SUBLANES, :] = xbuf[tm:tm + SUBLANES, :]


def _in_proj(x2, mod3, pos_row, rope_tab, w_in_p, qg, w_qb_p, kvg, w_k, w_vt, conv_w, conv_b, dtb,
             *, off, tm, sub, q_scale):
    s, d = x2.shape
    nq = MLA_HEADS * QK_PAD
    nv = MLA_HEADS * V_HEAD_DIM
    nch = conv_w.shape[1]
    nbc = SSM_GROUPS * SSM_STATE
    nx = nch - 2 * nbc
    row = lambda w: pl.BlockSpec((tm, w), lambda i: (i, 0))
    full = lambda a: pl.BlockSpec(a.shape, lambda i: (0,) * a.ndim, pipeline_mode=pl.Buffered(1))
    outs = [
        jax.ShapeDtypeStruct((s, nq), BF16),
        jax.ShapeDtypeStruct((s, nq), BF16),
        jax.ShapeDtypeStruct((nv, s), BF16),
        jax.ShapeDtypeStruct((s, nv), BF16),
        jax.ShapeDtypeStruct((s, nx), BF16),
        jax.ShapeDtypeStruct((s, nbc), BF16),
        jax.ShapeDtypeStruct((s, nbc), BF16),
        jax.ShapeDtypeStruct((s, LANES), F32),
        jax.ShapeDtypeStruct((s, nx), BF16),
    ]
    return pl.pallas_call(
        functools.partial(_in_kernel, tm=tm, sub=sub, off=off, q_scale=q_scale),
        grid=(s // tm,),
        in_specs=[row(d), full(mod3), pl.BlockSpec((1, tm), lambda i: (0, i)), full(rope_tab),
                  full(w_in_p), full(qg), full(w_qb_p), full(kvg), full(w_k), full(w_vt),
                  full(conv_w), full(conv_b), full(dtb)],
        out_specs=[pl.BlockSpec((nv, tm), lambda i: (0, i)) if n == 2 else row(o.shape[1])
                   for n, o in enumerate(outs)],
        out_shape=outs,
        scratch_shapes=[pltpu.VMEM((tm + 2 * SUBLANES, nch), F32)],
        compiler_params=pltpu.CompilerParams(dimension_semantics=("arbitrary",),
                                             vmem_limit_bytes=VMEM_LIMIT),
        name="in_proj",
    )(x2, mod3, pos_row, rope_tab, w_in_p, qg, w_qb_p, kvg, w_k, w_vt, conv_w, conv_b, dtb)


def _sublane_allmax(x):
    shift = SUBLANES // 2
    while shift:
        x = jnp.maximum(x, pltpu.roll(x, shift, axis=0))
        shift //= 2
    return x


def _sublane_allsum(x):
    shift = SUBLANES // 2
    while shift:
        x = x + pltpu.roll(x, shift, axis=0)
        shift //= 2
    return x


BF16_ROWS = 16
ACC_ROWS = V_HEAD_DIM + BF16_ROWS
EXP_ROWS = 32


def _attn_kernel(q_ref, k_ref, vt_ref, o_ref, m_a, acc_a, m_b, acc_b,
                 s_a, s_b, cmax_a, cmax_b, p_a, p_b, al_a, al_b, *, tq, tk):
    assert tq == tk
    g = pl.program_id(1)
    nsub = tk // SUBLANES
    nacc = ACC_ROWS // SUBLANES
    buf_a = (s_a, cmax_a, p_a, al_a)
    buf_b = (s_b, cmax_b, p_b, al_b)
    rows_a, rows_b = slice(0, tq), slice(tq, 2 * tq)
    ones_rows = jnp.ones((ACC_ROWS - V_HEAD_DIM, tk), BF16)

    def reset(stats):
        m_scr, acc_scr = stats
        m_scr[...] = jnp.full(m_scr.shape, NEG_BIG, F32)
        acc_scr[...] = jnp.zeros(acc_scr.shape, F32)

    def qk_stage(kv, q_rows, buf, diag=False):
        s_out, cmax_out = buf[0], buf[1]
        start = pl.multiple_of(kv * tk, tk)
        nt = (((1,), (1,)), ((), ()))
        colmax = lambda x: _sublane_allmax(jnp.max(x.reshape(x.shape[0] // SUBLANES, SUBLANES, x.shape[1]), axis=0))
        if not diag:
            st = lax.dot_general(k_ref[pl.ds(start, tk), :], q_ref[q_rows, :], nt,
                                 preferred_element_type=F32)
            s_out[...] = st
            cmax_out[...] = colmax(st)
            return
        hk = tk // 2
        q_lo = q_rows.start
        mask = (lax.broadcasted_iota(jnp.int32, (hk, tq), 0) <= lax.broadcasted_iota(jnp.int32, (hk, tq), 1))
        top = lax.dot_general(k_ref[pl.ds(start, hk), :], q_ref[q_rows, :], nt, preferred_element_type=F32)
        top = jnp.where(mask, top, NEG_BIG)
        bot = lax.dot_general(k_ref[pl.ds(start + hk, hk), :], q_ref[q_lo + hk:q_lo + tq, :], nt,
                              preferred_element_type=F32)
        bot = jnp.where(mask[:, :tq - hk], bot, NEG_BIG)
        s_out[0:hk, :] = top
        s_out[hk:tk, 0:hk] = jnp.full((tk - hk, hk), NEG_BIG, F32)
        s_out[hk:tk, hk:tq] = bot
        cm_top = colmax(top)
        cmax_out[:, 0:hk] = cm_top[:, 0:hk]
        cmax_out[:, hk:tq] = jnp.maximum(cm_top[:, hk:tq], colmax(bot))

    def softmax_stage(buf, stats):
        s_in, cmax_in, p_out, al_out = buf
        m_scr = stats[0]
        m_old = m_scr[...]
        m_new = jnp.maximum(m_old, cmax_in[...])
        al_out[...] = jnp.exp2(m_old - m_new)
        m_scr[...] = m_new
        for c in range(0, tq, MXU_DIM):
            m_c = m_new[:, c:c + MXU_DIM][None]
            for r in range(0, tk, EXP_ROWS):
                sc = s_in[r:r + EXP_ROWS, c:c + MXU_DIM].reshape(EXP_ROWS // SUBLANES, SUBLANES, MXU_DIM)
                p_out[r:r + EXP_ROWS, c:c + MXU_DIM] = (
                    jnp.exp2(sc - m_c).reshape(EXP_ROWS, MXU_DIM).astype(BF16))

    def pv_stage(kv, buf, stats):
        p_in, al_in = buf[2], buf[3]
        acc_scr = stats[1]
        start = pl.multiple_of(jnp.maximum(kv, 0) * tk, tk)
        vt_aug = jnp.concatenate([vt_ref[:, pl.ds(start, tk)], ones_rows], axis=0)
        pv = jnp.dot(vt_aug, p_in[...], preferred_element_type=F32)
        acc3 = acc_scr[...].reshape(nacc, SUBLANES, tq) * al_in[...][None]
        acc_scr[...] = acc3.reshape(ACC_ROWS, tq) + pv

    def finalize(q_rows, stats):
        acc_scr = stats[1]
        inv_l = 1.0 / acc_scr[V_HEAD_DIM:V_HEAD_DIM + SUBLANES, :]
        o_t = acc_scr[0:V_HEAD_DIM, :].reshape(V_HEAD_DIM // SUBLANES, SUBLANES, tq) * inv_l[None]
        o_ref[q_rows, :] = o_t.reshape(V_HEAD_DIM, tq).T.astype(o_ref.dtype)

    def make_step(qi, q_rows, stats):
        kv_of = lambda pos: jnp.where(pos == 0, qi, pos - 1)

        def step(pos, buf, other):
            qk_stage(kv_of(pos), q_rows, buf)
            pv_stage(kv_of(pos - 2), buf, stats)
            softmax_stage(other, stats)

        return kv_of, step

    def run_pairs(lo, hi, pair):
        def body(pp, carry):
            pair(pp)
            return carry

        lax.fori_loop(lo, hi, body, 0)

    qa = 2 * g
    qb = qa + 1
    st_a, st_b = (m_a, acc_a), (m_b, acc_b)
    kv_a, step_a = make_step(qa, rows_a, st_a)
    kv_b, step_b = make_step(qb, rows_b, st_b)
    reset(st_a)
    reset(st_b)
    qk_stage(qa, rows_a, buf_a, diag=True)

    @pl.when(g == 0)
    def _():
        p_b[...] = jnp.zeros(p_b.shape, BF16)
        al_b[...] = jnp.ones(al_b.shape, F32)

    @pl.when(g > 0)
    def _():
        qk_stage(kv_a(1), rows_a, buf_b)
        softmax_stage(buf_a, st_a)
        step_a(2, buf_a, buf_b)

        def pair_a(pp):
            step_a(2 * pp + 1, buf_b, buf_a)
            step_a(2 * pp + 2, buf_a, buf_b)

        run_pairs(1, g, pair_a)

    pv_stage(kv_a(qa - 1), buf_b, st_a)
    qk_stage(qb, rows_b, buf_b, diag=True)
    softmax_stage(buf_a, st_a)
    pv_stage(kv_a(qa), buf_a, st_a)
    qk_stage(kv_b(1), rows_b, buf_a)
    softmax_stage(buf_b, st_b)
    finalize(rows_a, st_a)

    def pair_b(pp):
        step_b(2 * pp, buf_b, buf_a)
        step_b(2 * pp + 1, buf_a, buf_b)

    run_pairs(1, g + 1, pair_b)
    pv_stage(kv_b(qb - 1), buf_b, st_b)
    softmax_stage(buf_a, st_b)
    pv_stage(kv_b(qb), buf_a, st_b)
    finalize(rows_b, st_b)


def _attention(q, k, vt, *, tq, tk):
    s = q.shape[0]
    assert s % (2 * tq) == 0, "sequence length must be a multiple of two query tiles"
    stat = pltpu.VMEM((SUBLANES, tq), F32)
    return pl.pallas_call(
        functools.partial(_attn_kernel, tq=tq, tk=tk),
        grid=(MLA_HEADS, s // (2 * tq)),
        in_specs=[pl.BlockSpec((2 * tq, QK_PAD), lambda h, i: (i, h)),
                  pl.BlockSpec((s, QK_PAD), lambda h, i: (0, h)),
                  pl.BlockSpec((V_HEAD_DIM, s), lambda h, i: (h, 0))],
        out_specs=pl.BlockSpec((2 * tq, V_HEAD_DIM), lambda h, i: (i, h)),
        out_shape=jax.ShapeDtypeStruct((s, MLA_HEADS * V_HEAD_DIM), BF16),
        scratch_shapes=[stat, pltpu.VMEM((ACC_ROWS, tq), F32), stat, pltpu.VMEM((ACC_ROWS, tq), F32),
                        pltpu.VMEM((tk, tq), F32), pltpu.VMEM((tk, tq), F32), stat, stat,
                        pltpu.VMEM((tk, tq), BF16), pltpu.VMEM((tk, tq), BF16), stat, stat],
        compiler_params=pltpu.CompilerParams(dimension_semantics=("arbitrary", "arbitrary"),
                                             vmem_limit_bytes=VMEM_LIMIT),
        name="attention",
    )(q, k, vt)


def _ssd_kernel(xs_ref, bm_ref, cm_ref, dt_ref, gz_ref, alog_ref, dskip_ref, g_ref, e2_ref,
                o_ref, state, y_scr, *, rows, nheads):
    i = pl.program_id(0)
    gw = SSM_STATE
    hw = state.shape[2]

    @pl.when(i == 0)
    def _():
        state[...] = jnp.zeros(state.shape, F32)

    a_neg = -jnp.exp(alog_ref[...])
    r_i = lax.broadcasted_iota(jnp.int32, (CHUNK, CHUNK), 0)
    c_i = lax.broadcasted_iota(jnp.int32, (CHUNK, CHUNK), 1)
    tri = c_i <= r_i
    tri_bf = tri.astype(BF16)
    lane = lax.broadcasted_iota(jnp.int32, (CHUNK, LANES), 1)
    first_copy = lane < nheads
    head_lo = lane < SSM_HEAD_DIM
    e2 = e2_ref[...]

    def expand(v):
        hi = v.astype(BF16)
        lo = (v - hi.astype(F32)).astype(BF16)
        return jnp.dot(jnp.where(first_copy, hi, lo), e2, preferred_element_type=F32)

    for cidx in range(rows // CHUNK):
        sl = pl.ds(cidx * CHUNK, CHUNK)
        dt = dt_ref[sl, :]
        da = dt * a_neg
        a_cum = jnp.zeros((CHUNK, LANES), F32)
        for part in _split_bf16(da, 3):
            a_cum = a_cum + jnp.dot(tri_bf, part, preferred_element_type=F32)
        a_cum_t = a_cum.T
        a_last = a_cum[CHUNK - 1:CHUNK, :]
        ea = jnp.exp(a_cum)
        dte = jnp.exp(a_last - a_cum)

        xs = xs_ref[sl, :].astype(F32)
        xd = xs * expand(dt)
        ea_x = expand(ea)
        xdd = (xd * expand(dte)).astype(BF16)
        xd_bf = xd.astype(BF16)

        for g in range(SSM_GROUPS):
            bg = bm_ref[sl, g * gw:(g + 1) * gw]
            cg = cm_ref[sl, g * gw:(g + 1) * gw]
            cb = lax.dot_general(cg, bg, (((1,), (1,)), ((), ())), preferred_element_type=F32)
            prev = state[g]
            y_off = jnp.dot(cg, prev.astype(BF16), preferred_element_type=F32)
            new = lax.dot_general(bg, xdd[:, g * hw:(g + 1) * hw], (((0,), (0,)), ((), ())),
                                  preferred_element_type=F32)
            state[g] = prev * ea_x[CHUNK - 1:CHUNK, g * hw:(g + 1) * hw] + new
            y_scr[:, g * hw:(g + 1) * hw] = y_off * ea_x[:, g * hw:(g + 1) * hw]

            hpg = hw // SSM_HEAD_DIM
            for pair in range(hpg // 2):
                h0 = g * hpg + 2 * pair
                c0 = h0 * SSM_HEAD_DIM
                xp = xd_bf[:, c0:c0 + LANES]
                yp = jnp.zeros((CHUNK, LANES), F32)
                for k, keep in ((0, head_lo), (1, ~head_lo)):
                    h = h0 + k
                    seg = a_cum[:, h:h + 1] - a_cum_t[h:h + 1, :]
                    m_h = (cb * jnp.exp(jnp.where(tri, seg, NEG_BIG))).astype(BF16)
                    yp = yp + jnp.dot(m_h, jnp.where(keep, xp, jnp.zeros_like(xp)),
                                      preferred_element_type=F32)
                y_scr[:, c0:c0 + LANES] = y_scr[:, c0:c0 + LANES] + yp

        y = y_scr[...] + xs * dskip_ref[...]
        hf = y * gz_ref[sl, :].astype(F32)
        for g in range(SSM_GROUPS):
            hg = hf[:, g * hw:(g + 1) * hw]
            ms = jnp.mean(hg * hg, axis=-1, keepdims=True)
            o_ref[sl, g * hw:(g + 1) * hw] = (hg * lax.rsqrt(ms + RMS_EPS)
                                               * g_ref[:, g * hw:(g + 1) * hw]).astype(o_ref.dtype)


def _ssd(xs, bm, cm, dt, gzs, alog, dskip_x, norm_g, e2, *, rows, nheads):
    s, nx = xs.shape
    hw = nx // SSM_GROUPS
    row = lambda a: pl.BlockSpec((rows, a.shape[1]), lambda i: (i, 0))
    full = lambda a: pl.BlockSpec(a.shape, lambda i: (0,) * a.ndim)
    return pl.pallas_call(
        functools.partial(_ssd_kernel, rows=rows, nheads=nheads),
        grid=(s // rows,),
        in_specs=[row(xs), row(bm), row(cm), row(dt), row(gzs),
                  full(alog), full(dskip_x), full(norm_g), full(e2)],
        out_specs=pl.BlockSpec((rows, nx), lambda i: (i, 0)),
        out_shape=jax.ShapeDtypeStruct((s, nx), BF16),
        scratch_shapes=[pltpu.VMEM((SSM_GROUPS, SSM_STATE, hw), F32),
                        pltpu.VMEM((CHUNK, nx), F32)],
        compiler_params=pltpu.CompilerParams(dimension_semantics=("arbitrary",),
                                             vmem_limit_bytes=VMEM_LIMIT),
        name="ssd",
    )(xs, bm, cm, dt, gzs, alog, dskip_x, norm_g, e2)


def _out_kernel(oa_ref, gza_ref, os_ref, x_ref, w32_ref, mod_ref, lng_ref, lnb_ref, o_ref, w_ref,
                *, alpha, sub):
    na = oa_ref.shape[1]

    @pl.when(pl.program_id(0) == 0)
    def _():
        for r in range(0, w_ref.shape[0], sub):
            w_ref[r:r + sub, :] = w32_ref[r:r + sub, :].astype(BF16)

    mixes = []
    for r in range(0, o_ref.shape[0], sub):
        rows = slice(r, r + sub)
        oa = (oa_ref[rows, :].astype(F32) * gza_ref[rows, :].astype(F32)).astype(BF16)
        mixed = jnp.dot(oa, w_ref[0:na, :], preferred_element_type=F32)
        mixes.append(mixed + jnp.dot(os_ref[rows, :], w_ref[na:, :], preferred_element_type=F32))
    for n, mixed in enumerate(mixes):
        rows = slice(n * sub, (n + 1) * sub)
        y = alpha * x_ref[rows, :] + mod_ref[2:3, :] * mixed
        mu = jnp.mean(y, axis=-1, keepdims=True)
        yc = y - mu
        var = jnp.mean(yc * yc, axis=-1, keepdims=True)
        o_ref[rows, :] = yc * lax.rsqrt(var + LN_EPS) * lng_ref[...] + lnb_ref[...]


def _out_proj(oa, gza, o_ssm, x2, w_out, mod3, ln_g, ln_b, *, tm, sub, alpha):
    s, d = x2.shape
    row = lambda a: pl.BlockSpec((tm, a.shape[1]), lambda i: (i, 0))
    full = lambda a: pl.BlockSpec(a.shape, lambda i: (0,) * a.ndim)
    return pl.pallas_call(
        functools.partial(_out_kernel, alpha=alpha, sub=sub),
        grid=(s // tm,),
        in_specs=[row(oa), row(gza), row(o_ssm), row(x2),
                  pl.BlockSpec(w_out.shape, lambda i: (0, 0), pipeline_mode=pl.Buffered(1)),
                  full(mod3), full(ln_g), full(ln_b)],
        out_specs=pl.BlockSpec((tm, d), lambda i: (i, 0)),
        out_shape=jax.ShapeDtypeStruct((s, d), F32),
        scratch_shapes=[pltpu.VMEM(w_out.shape, BF16)],
        compiler_params=pltpu.CompilerParams(dimension_semantics=("arbitrary",),
                                             vmem_limit_bytes=VMEM_LIMIT),
        name="out_proj",
    )(oa, gza, o_ssm, x2, w_out, mod3, ln_g, ln_b)


def _pad_cols(a, width):
    return jnp.pad(a, ((0, 0), (0, width - a.shape[1])))


def _regroup_kernel(wt_ref, o_ref, *, segments):
    wt = wt_ref[...]
    parts = [jnp.zeros((b, wt.shape[1]), wt.dtype) if a is None else wt[a:b, :] for a, b in segments]
    o_ref[...] = jnp.concatenate(parts, axis=0).T.astype(o_ref.dtype)


def _regroup_columns(w, segments, width):
    assert all(b % SUBLANES == 0 and (a or 0) % SUBLANES == 0 for a, b in segments)
    rows = w.shape[0]
    br = 256
    return pl.pallas_call(
        functools.partial(_regroup_kernel, segments=segments),
        grid=(rows // br,),
        in_specs=[pl.BlockSpec((w.shape[1], br), lambda i: (0, i))],
        out_specs=pl.BlockSpec((br, width), lambda i: (i, 0)),
        out_shape=jax.ShapeDtypeStruct((rows, width), BF16),
        compiler_params=pltpu.CompilerParams(vmem_limit_bytes=VMEM_LIMIT),
        name="regroup_w_in",
    )(w.T)


def _layer(x2, c, pos_row, w_ada, b_ada, w_in, q_norm_g, w_qb, kv_norm_g, w_kvb,
           conv_w, conv_b, dt_bias, a_log, d_skip, ssm_norm_g, w_out, ln_g, ln_b, *, depth):
    s, d = x2.shape
    q_rank = q_norm_g.shape[0]
    kv_rank = kv_norm_g.shape[0]
    nheads = dt_bias.shape[0]
    nch = conv_w.shape[1]
    nv = MLA_HEADS * V_HEAD_DIM
    nx = nheads * SSM_HEAD_DIM
    half = QK_ROPE_DIM // 2

    mod = _adaln_mod(c.reshape(d, 1), w_ada, b_ada.reshape(1, -1))
    mod3 = mod.reshape(3, d)

    o_q, o_ckv = 0, q_rank
    o_kr = o_ckv + kv_rank
    o_za = o_kr + QK_ROPE_DIM
    o_xbc = o_za + nv
    o_dt = o_xbc + nch
    o_zs = o_dt + nheads
    dt_pad = LANES - 2 * nheads
    groups = [("q", [(o_q, o_ckv)]), ("ckv", [(o_ckv, o_kr)]),
              ("krope", [(o_kr, o_za), (o_kr + half, o_za), (o_kr, o_kr + half)]),
              ("za", [(o_za, o_xbc)]), ("xbc", [(o_xbc, o_dt)]), ("zs", [(o_zs, o_zs + nx)]),
              ("dt", [(o_dt, o_zs), (o_dt, o_zs), (None, dt_pad)])]
    off, cur, segments = {}, 0, []
    for name, segs in groups:
        assert cur % LANES == 0
        off[name] = cur
        for a, b in segs:
            cur += b if a is None else b - a
            if segments and a is not None and segments[-1][0] is not None and segments[-1][1] == a:
                segments[-1] = (segments[-1][0], b)
            else:
                segments.append((a, b))
    assert cur % LANES == 0
    w_in_p = _regroup_columns(w_in, tuple(segments), cur)

    w3 = w_qb.reshape(q_rank, MLA_HEADS, QK_NOPE_DIM + QK_ROPE_DIM)
    w_rope = w3[:, :, QK_NOPE_DIM:]
    w_qb_p = jnp.concatenate([w3, w_rope[:, :, half:], w_rope[:, :, :half]], axis=2)
    w_qb_p = w_qb_p.reshape(q_rank, MLA_HEADS * QK_PAD).astype(BF16)

    inv_freq = 1.0 / (ROPE_THETA ** (jnp.arange(half, dtype=F32) / half))
    freq = jnp.tile(inv_freq, LANES // half)
    phase = jnp.concatenate([jnp.zeros((2 * half,), F32), jnp.full((half,), 0.5 * math.pi, F32),
                             jnp.full((half,), -0.5 * math.pi, F32)])
    rope_tab = jnp.broadcast_to(jnp.stack([freq, phase])[:, :, None], (2, LANES, IN_SUB))

    dtb = _pad_cols(jnp.concatenate([dt_bias, dt_bias]).reshape(1, -1), LANES)
    q_scale = (QK_NOPE_DIM + QK_ROPE_DIM) ** -0.5 * LOG2E

    wkv3 = w_kvb.reshape(kv_rank, MLA_HEADS, QK_NOPE_DIM + V_HEAD_DIM)
    w_k = wkv3[:, :, :QK_NOPE_DIM].reshape(kv_rank, MLA_HEADS * QK_NOPE_DIM).astype(BF16)
    w_vt = wkv3[:, :, QK_NOPE_DIM:].reshape(kv_rank, nv).T.astype(BF16)

    q, k, vt, gza, xs, bm, cm, dt, gzs = _in_proj(
        x2, mod3, pos_row, rope_tab, w_in_p, q_norm_g.reshape(1, -1), w_qb_p,
        kv_norm_g.reshape(1, -1), w_k, w_vt, conv_w, conv_b.reshape(1, -1), dtb,
        off=off, tm=IN_ROWS, sub=IN_SUB, q_scale=q_scale)

    o_attn = _attention(q, k, vt, tq=1024, tk=1024)

    alog = _pad_cols(jnp.concatenate([a_log, a_log]).reshape(1, -1), LANES)
    e_head = jnp.repeat(jnp.eye(nheads, dtype=BF16), SSM_HEAD_DIM, axis=1)
    e2 = jnp.pad(jnp.concatenate([e_head, e_head], axis=0), ((0, LANES - 2 * nheads), (0, 0)))
    dskip_x = jnp.repeat(d_skip, SSM_HEAD_DIM).reshape(1, -1)
    o_ssm = _ssd(xs, bm, cm, dt, gzs, alog, dskip_x, ssm_norm_g.reshape(1, -1), e2,
                 rows=512, nheads=nheads)

    alpha = (2.0 * depth) ** 0.25
    return _out_proj(o_attn, gza, o_ssm, x2, w_out, mod3,
                     ln_g.reshape(1, -1), ln_b.reshape(1, -1), tm=OUT_ROWS, sub=OUT_SUB, alpha=alpha)


def kernel(x, c, positions, w_ada, b_ada, w_in, q_norm_g, w_qb, kv_norm_g, w_kvb, conv_w, conv_b,
           dt_bias, a_log, d_skip, ssm_norm_g, w_out, ln_g, ln_b):
    b, s, d = x.shape
    depth = w_in.shape[0]
    assert b == 1, "one sequence per call"
    h = x.reshape(s, d)
    pos_row = positions.reshape(1, s)
    for l in range(depth):
        h = _layer(h, c, pos_row, w_ada[l], b_ada[l], w_in[l], q_norm_g[l], w_qb[l], kv_norm_g[l],
                   w_kvb[l], conv_w[l], conv_b[l], dt_bias[l], a_log[l], d_skip[l], ssm_norm_g[l],
                   w_out[l], ln_g[l], ln_b[l], depth=depth)
    return h.reshape(b, s, d)
```

```python
import functools
import math

import jax
import jax.numpy as jnp
from jax import lax
from jax.experimental import pallas as pl
from jax.experimental.pallas import tpu as pltpu

F32 = jnp.float32
BF16 = jnp.bfloat16

MLA_HEADS = 8
QK_NOPE_DIM = 128
QK_ROPE_DIM = 64
V_HEAD_DIM = 128
ROPE_THETA = 10000.0
SSM_HEAD_DIM = 64
SSM_GROUPS = 2
SSM_STATE = 128
CHUNK = 128
RMS_EPS = 1e-6
LN_EPS = 1e-5

LANES = 128
SUBLANES = 8
MXU_DIM = 256
QK_PAD = MXU_DIM
VMEM_LIMIT = 56 * 1024 * 1024

IN_ROWS = 512
IN_SUB = 256
OUT_ROWS = 1024
OUT_SUB = 256

LOG2E = 1.4426950408889634
NEG_BIG = -1e30


def _silu(z):
    h = 0.5 * z
    return h * jnp.tanh(h) + h


def _softplus(z):
    return jnp.maximum(z, 0.0) + jnp.log1p(jnp.exp(-jnp.abs(z)))


def _split_bf16(x, parts):
    out, rem = [], x
    for _ in range(parts):
        hi = rem.astype(BF16)
        out.append(hi)
        rem = rem - hi.astype(F32)
    return out


def _mod_kernel(c_ref, w_ref, b_ref, o_ref):
    o_ref[...] = jnp.sum(w_ref[...] * c_ref[...], axis=0, keepdims=True) + b_ref[...]


def _adaln_mod(c_col, w_ada, b_ada):
    d, n = w_ada.shape
    bn = 1024
    return pl.pallas_call(
        _mod_kernel,
        grid=(n // bn,),
        in_specs=[pl.BlockSpec((d, 1), lambda j: (0, 0)),
                  pl.BlockSpec((d, bn), lambda j: (0, j)),
                  pl.BlockSpec((1, bn), lambda j: (0, j))],
        out_specs=pl.BlockSpec((1, bn), lambda j: (0, j)),
        out_shape=jax.ShapeDtypeStruct((1, n), F32),
        name="adaln_mod",
    )(c_col, w_ada, b_ada)


def _in_kernel(x_ref, mod_ref, pos_ref, rope_ref, w_in_ref, qg_ref, w_qb_ref, kvg_ref, w_k_ref, w_vt_ref,
               conv_w_ref, conv_b_ref, dtb_ref,
               q_out, k_out, vt_out, gza_out, xs_out, bm_out, cm_out, dt_out, gzs_out,
               xbuf, *, tm, sub, off, q_scale):
    i = pl.program_id(0)
    shift = mod_ref[0:1, :]
    scale1 = 1.0 + mod_ref[1:2, :]
    lane = lax.broadcasted_iota(jnp.int32, (sub, LANES), 1)
    low_half = lane < QK_ROPE_DIM
    nconv, nch = conv_w_ref.shape
    nx = xs_out.shape[1]
    nb = bm_out.shape[1]

    @pl.when(i == 0)
    def _():
        xbuf[0:SUBLANES, :] = jnp.zeros((SUBLANES, nch), F32)

    css = []
    for h in range(tm // sub):
        ang_t = pos_ref[:, h * sub:(h + 1) * sub].astype(F32) * rope_ref[...]
        cos_t, sin_t = jnp.cos(ang_t), jnp.sin(ang_t)
        css.append(jnp.concatenate([cos_t, cos_t, -sin_t, sin_t], axis=0).T)

    projs = []
    for h in range(tm // sub):
        u = x_ref[h * sub:(h + 1) * sub, :] * scale1 + shift
        projs.append(jnp.dot(u.astype(BF16), w_in_ref[...], preferred_element_type=F32))

    for h, (proj, cs) in enumerate(zip(projs, css)):
        rows = slice(h * sub, (h + 1) * sub)

        def rope(rr, cs=cs):
            t = rr * cs
            return jnp.where(low_half, t + pltpu.roll(t, QK_ROPE_DIM, axis=1), 0.0)

        q_lat = proj[:, off["q"]:off["q"] + qg_ref.shape[1]]
        qn = q_lat * lax.rsqrt(jnp.mean(q_lat * q_lat, axis=-1, keepdims=True) + RMS_EPS) * qg_ref[...]
        qf = jnp.dot(qn.astype(BF16), w_qb_ref[...], preferred_element_type=F32)
        for hd in range(MLA_HEADS):
            c0 = hd * QK_PAD
            q_out[rows, c0:c0 + QK_NOPE_DIM] = (qf[:, c0:c0 + QK_NOPE_DIM] * q_scale).astype(BF16)
            q_out[rows, c0 + QK_NOPE_DIM:c0 + QK_PAD] = (
                rope(qf[:, c0 + QK_NOPE_DIM:c0 + QK_PAD]) * q_scale).astype(BF16)

        c_kv = proj[:, off["ckv"]:off["ckv"] + kvg_ref.shape[1]]
        ckvn = c_kv * lax.rsqrt(jnp.mean(c_kv * c_kv, axis=-1, keepdims=True) + RMS_EPS) * kvg_ref[...]
        ckvn_bf = ckvn.astype(BF16)
        kf = jnp.dot(ckvn_bf, w_k_ref[...], preferred_element_type=F32)
        vt_out[:, rows] = lax.dot_general(w_vt_ref[...], ckvn_bf, (((1,), (1,)), ((), ())),
                                          preferred_element_type=F32).astype(BF16)
        k_rope = rope(proj[:, off["krope"]:off["krope"] + LANES]).astype(BF16)
        for hd in range(MLA_HEADS):
            k_out[rows, hd * QK_PAD:hd * QK_PAD + QK_NOPE_DIM] = (
                kf[:, hd * QK_NOPE_DIM:(hd + 1) * QK_NOPE_DIM].astype(BF16))
            k_out[rows, hd * QK_PAD + QK_NOPE_DIM:(hd + 1) * QK_PAD] = k_rope

        gza_out[rows, :] = _silu(proj[:, off["za"]:off["za"] + gza_out.shape[1]]).astype(BF16)
        gzs_out[rows, :] = _silu(proj[:, off["zs"]:off["zs"] + gzs_out.shape[1]]).astype(BF16)
        dt_out[rows, :] = _softplus(proj[:, off["dt"]:off["dt"] + LANES] + dtb_ref[...])

        r1 = SUBLANES + h * sub
        xbuf[r1:r1 + sub, :] = proj[:, off["xbc"]:off["xbc"] + nch]
        xfull = xbuf[r1 - SUBLANES:r1 + sub, :]
        acc = conv_b_ref[...] + conv_w_ref[nconv - 1:nconv, :] * xfull[SUBLANES:, :]
        for k in range(1, nconv):
            shifted = pltpu.roll(xfull, k, axis=0)[SUBLANES:, :]
            acc = acc + conv_w_ref[nconv - 1 - k:nconv - k, :] * shifted
        xc = _silu(acc)
        xs_out[rows, :] = xc[:, :nx].astype(BF16)
        bm_out[rows, :] = xc[:, nx:nx + nb].astype(BF16)
        cm_out[rows, :] = xc[:, nx + nb:nx + 2 * nb].astype(BF16)

    xbuf[0:SUBLANES, :] = xbuf[tm:tm + SUBLANES, :]


def _in_proj(x2, mod3, pos_row, rope_tab, w_in_p, qg, w_qb_p, kvg, w_k, w_vt, conv_w, conv_b, dtb,
             *, off, tm, sub, q_scale):
    s, d = x2.shape
    nq = MLA_HEADS * QK_PAD
    nv = MLA_HEADS * V_HEAD_DIM
    nch = conv_w.shape[1]
    nbc = SSM_GROUPS * SSM_STATE
    nx = nch - 2 * nbc
    row = lambda w: pl.BlockSpec((tm, w), lambda i: (i, 0))
    full = lambda a: pl.BlockSpec(a.shape, lambda i: (0,) * a.ndim, pipeline_mode=pl.Buffered(1))
    outs = [
        jax.ShapeDtypeStruct((s, nq), BF16),
        jax.ShapeDtypeStruct((s, nq), BF16),
        jax.ShapeDtypeStruct((nv, s), BF16),
        jax.ShapeDtypeStruct((s, nv), BF16),
        jax.ShapeDtypeStruct((s, nx), BF16),
        jax.ShapeDtypeStruct((s, nbc), BF16),
        jax.ShapeDtypeStruct((s, nbc), BF16),
        jax.ShapeDtypeStruct((s, LANES), F32),
        jax.ShapeDtypeStruct((s, nx), BF16),
    ]
    return pl.pallas_call(
        functools.partial(_in_kernel, tm=tm, sub=sub, off=off, q_scale=q_scale),
        grid=(s // tm,),
        in_specs=[row(d), full(mod3), pl.BlockSpec((1, tm), lambda i: (0, i)), full(rope_tab),
                  full(w_in_p), full(qg), full(w_qb_p), full(kvg), full(w_k), full(w_vt),
                  full(conv_w), full(conv_b), full(dtb)],
        out_specs=[pl.BlockSpec((nv, tm), lambda i: (0, i)) if n == 2 else row(o.shape[1])
                   for n, o in enumerate(outs)],
        out_shape=outs,
        scratch_shapes=[pltpu.VMEM((tm + 2 * SUBLANES, nch), F32)],
        compiler_params=pltpu.CompilerParams(dimension_semantics=("arbitrary",),
                                             vmem_limit_bytes=VMEM_LIMIT),
        name="in_proj",
    )(x2, mod3, pos_row, rope_tab, w_in_p, qg, w_qb_p, kvg, w_k, w_vt, conv_w, conv_b, dtb)


def _sublane_allmax(x):
    shift = SUBLANES // 2
    while shift:
        x = jnp.maximum(x, pltpu.roll(x, shift, axis=0))
        shift //= 2
    return x


def _sublane_allsum(x):
    shift = SUBLANES // 2
    while shift:
        x = x + pltpu.roll(x, shift, axis=0)
        shift //= 2
    return x


BF16_ROWS = 16
ACC_ROWS = V_HEAD_DIM + BF16_ROWS
EXP_ROWS = 32


def _attn_kernel(q_ref, k_ref, vt_ref, o_ref, m_a, acc_a, m_b, acc_b,
                 s_a, s_b, cmax_a, cmax_b, p_a, p_b, al_a, al_b, *, tq, tk):
    assert tq == tk
    g = pl.program_id(1)
    nsub = tk // SUBLANES
    nacc = ACC_ROWS // SUBLANES
    buf_a = (s_a, cmax_a, p_a, al_a)
    buf_b = (s_b, cmax_b, p_b, al_b)
    rows_a, rows_b = slice(0, tq), slice(tq, 2 * tq)
    ones_rows = jnp.ones((ACC_ROWS - V_HEAD_DIM, tk), BF16)

    def reset(stats):
        m_scr, acc_scr = stats
        m_scr[...] = jnp.full(m_scr.shape, NEG_BIG, F32)
        acc_scr[...] = jnp.zeros(acc_scr.shape, F32)

    def qk_stage(kv, q_rows, buf, diag=False):
        s_out, cmax_out = buf[0], buf[1]
        start = pl.multiple_of(kv * tk, tk)
        nt = (((1,), (1,)), ((), ()))
        colmax = lambda x: _sublane_allmax(jnp.max(x.reshape(x.shape[0] // SUBLANES, SUBLANES, x.shape[1]), axis=0))
        if not diag:
            st = lax.dot_general(k_ref[pl.ds(start, tk), :], q_ref[q_rows, :], nt,
                                 preferred_element_type=F32)
            s_out[...] = st
            cmax_out[...] = colmax(st)
            return
        hk = tk // 2
        q_lo = q_rows.start
        mask = (lax.broadcasted_iota(jnp.int32, (hk, tq), 0) <= lax.broadcasted_iota(jnp.int32, (hk, tq), 1))
        top = lax.dot_general(k_ref[pl.ds(start, hk), :], q_ref[q_rows, :], nt, preferred_element_type=F32)
        top = jnp.where(mask, top, NEG_BIG)
        bot = lax.dot_general(k_ref[pl.ds(start + hk, hk), :], q_ref[q_lo + hk:q_lo + tq, :], nt,
                              preferred_element_type=F32)
        bot = jnp.where(mask[:, :tq - hk], bot, NEG_BIG)
        s_out[0:hk, :] = top
        s_out[hk:tk, 0:hk] = jnp.full((tk - hk, hk), NEG_BIG, F32)
        s_out[hk:tk, hk:tq] = bot
        cm_top = colmax(top)
        cmax_out[:, 0:hk] = cm_top[:, 0:hk]
        cmax_out[:, hk:tq] = jnp.maximum(cm_top[:, hk:tq], colmax(bot))

    def softmax_stage(buf, stats):
        s_in, cmax_in, p_out, al_out = buf
        m_scr = stats[0]
        m_old = m_scr[...]
        m_new = jnp.maximum(m_old, cmax_in[...])
        al_out[...] = jnp.exp2(m_old - m_new)
        m_scr[...] = m_new
        for c in range(0, tq, MXU_DIM):
            m_c = m_new[:, c:c + MXU_DIM][None]
            for r in range(0, tk, EXP_ROWS):
                sc = s_in[r:r + EXP_ROWS, c:c + MXU_DIM].reshape(EXP_ROWS // SUBLANES, SUBLANES, MXU_DIM)
                p_out[r:r + EXP_ROWS, c:c + MXU_DIM] = (
                    jnp.exp2(sc - m_c).reshape(EXP_ROWS, MXU_DIM).astype(BF16))

    def pv_stage(kv, buf, stats):
        p_in, al_in = buf[2], buf[3]
        acc_scr = stats[1]
        start = pl.multiple_of(jnp.maximum(kv, 0) * tk, tk)
        vt_aug = jnp.concatenate([vt_ref[:, pl.ds(start, tk)], ones_rows], axis=0)
        pv = jnp.dot(vt_aug, p_in[...], preferred_element_type=F32)
        acc3 = acc_scr[...].reshape(nacc, SUBLANES, tq) * al_in[...][None]
        acc_scr[...] = acc3.reshape(ACC_ROWS, tq) + pv

    def finalize(q_rows, stats):
        acc_scr = stats[1]
        inv_l = 1.0 / acc_scr[V_HEAD_DIM:V_HEAD_DIM + SUBLANES, :]
        o_t = acc_scr[0:V_HEAD_DIM, :].reshape(V_HEAD_DIM // SUBLANES, SUBLANES, tq) * inv_l[None]
        o_ref[q_rows, :] = o_t.reshape(V_HEAD_DIM, tq).T.astype(o_ref.dtype)

    def make_step(qi, q_rows, stats):
        kv_of = lambda pos: jnp.where(pos == 0, qi, pos - 1)

        def step(pos, buf, other):
            qk_stage(kv_of(pos), q_rows, buf)
            pv_stage(kv_of(pos - 2), buf, stats)
            softmax_stage(other, stats)

        return kv_of, step

    def run_pairs(lo, hi, pair):
        def body(pp, carry):
            pair(pp)
            return carry

        lax.fori_loop(lo, hi, body, 0)

    qa = 2 * g
    qb = qa + 1
    st_a, st_b = (m_a, acc_a), (m_b, acc_b)
    kv_a, step_a = make_step(qa, rows_a, st_a)
    kv_b, step_b = make_step(qb, rows_b, st_b)
    reset(st_a)
    reset(st_b)
    qk_stage(qa, rows_a, buf_a, diag=True)

    @pl.when(g == 0)
    def _():
        p_b[...] = jnp.zeros(p_b.shape, BF16)
        al_b[...] = jnp.ones(al_b.shape, F32)

    @pl.when(g > 0)
    def _():
        qk_stage(kv_a(1), rows_a, buf_b)
        softmax_stage(buf_a, st_a)
        step_a(2, buf_a, buf_b)

        def pair_a(pp):
            step_a(2 * pp + 1, buf_b, buf_a)
            step_a(2 * pp + 2, buf_a, buf_b)

        run_pairs(1, g, pair_a)

    pv_stage(kv_a(qa - 1), buf_b, st_a)
    qk_stage(qb, rows_b, buf_b, diag=True)
    softmax_stage(buf_a, st_a)
    pv_stage(kv_a(qa), buf_a, st_a)
    qk_stage(kv_b(1), rows_b, buf_a)
    softmax_stage(buf_b, st_b)
    finalize(rows_a, st_a)

    def pair_b(pp):
        step_b(2 * pp, buf_b, buf_a)
        step_b(2 * pp + 1, buf_a, buf_b)

    run_pairs(1, g + 1, pair_b)
    pv_stage(kv_b(qb - 1), buf_b, st_b)
    softmax_stage(buf_a, st_b)
    pv_stage(kv_b(qb), buf_a, st_b)
    finalize(rows_b, st_b)


def _attention(q, k, vt, *, tq, tk):
    s = q.shape[0]
    assert s % (2 * tq) == 0, "sequence length must be a multiple of two query tiles"
    stat = pltpu.VMEM((SUBLANES, tq), F32)
    return pl.pallas_call(
        functools.partial(_attn_kernel, tq=tq, tk=tk),
        grid=(MLA_HEADS, s // (2 * tq)),
        in_specs=[pl.BlockSpec((2 * tq, QK_PAD), lambda h, i: (i, h)),
                  pl.BlockSpec((s, QK_PAD), lambda h, i: (0, h)),
                  pl.BlockSpec((V_HEAD_DIM, s), lambda h, i: (h, 0))],
        out_specs=pl.BlockSpec((2 * tq, V_HEAD_DIM), lambda h, i: (i, h)),
        out_shape=jax.ShapeDtypeStruct((s, MLA_HEADS * V_HEAD_DIM), BF16),
        scratch_shapes=[stat, pltpu.VMEM((ACC_ROWS, tq), F32), stat, pltpu.VMEM((ACC_ROWS, tq), F32),
                        pltpu.VMEM((tk, tq), F32), pltpu.VMEM((tk, tq), F32), stat, stat,
                        pltpu.VMEM((tk, tq), BF16), pltpu.VMEM((tk, tq), BF16), stat, stat],
        compiler_params=pltpu.CompilerParams(dimension_semantics=("arbitrary", "arbitrary"),
                                             vmem_limit_bytes=VMEM_LIMIT),
        name="attention",
    )(q, k, vt)


def _ssd_kernel(xs_ref, bm_ref, cm_ref, dt_ref, gz_ref, alog_ref, dskip_ref, g_ref, e2_ref,
                o_ref, state, y_scr, *, rows, nheads):
    i = pl.program_id(0)
    gw = SSM_STATE
    hw = state.shape[2]

    @pl.when(i == 0)
    def _():
        state[...] = jnp.zeros(state.shape, F32)

    a_neg = -jnp.exp(alog_ref[...])
    r_i = lax.broadcasted_iota(jnp.int32, (CHUNK, CHUNK), 0)
    c_i = lax.broadcasted_iota(jnp.int32, (CHUNK, CHUNK), 1)
    tri = c_i <= r_i
    tri_bf = tri.astype(BF16)
    lane = lax.broadcasted_iota(jnp.int32, (CHUNK, LANES), 1)
    first_copy = lane < nheads
    head_lo = lane < SSM_HEAD_DIM
    e2 = e2_ref[...]

    def expand(v):
        hi = v.astype(BF16)
        lo = (v - hi.astype(F32)).astype(BF16)
        return jnp.dot(jnp.where(first_copy, hi, lo), e2, preferred_element_type=F32)

    for cidx in range(rows // CHUNK):
        sl = pl.ds(cidx * CHUNK, CHUNK)
        dt = dt_ref[sl, :]
        da = dt * a_neg
        a_cum = jnp.zeros((CHUNK, LANES), F32)
        for part in _split_bf16(da, 3):
            a_cum = a_cum + jnp.dot(tri_bf, part, preferred_element_type=F32)
        a_cum_t = a_cum.T
        a_last = a_cum[CHUNK - 1:CHUNK, :]
        ea = jnp.exp(a_cum)
        dte = jnp.exp(a_last - a_cum)

        xs = xs_ref[sl, :].astype(F32)
        xd = xs * expand(dt)
        ea_x = expand(ea)
        xdd = (xd * expand(dte)).astype(BF16)
        xd_bf = xd.astype(BF16)

        for g in range(SSM_GROUPS):
            bg = bm_ref[sl, g * gw:(g + 1) * gw]
            cg = cm_ref[sl, g * gw:(g + 1) * gw]
            cb = lax.dot_general(cg, bg, (((1,), (1,)), ((), ())), preferred_element_type=F32)
            prev = state[g]
            y_off = jnp.dot(cg, prev.astype(BF16), preferred_element_type=F32)
            new = lax.dot_general(bg, xdd[:, g * hw:(g + 1) * hw], (((0,), (0,)), ((), ())),
                                  preferred_element_type=F32)
            state[g] = prev * ea_x[CHUNK - 1:CHUNK, g * hw:(g + 1) * hw] + new
            y_scr[:, g * hw:(g + 1) * hw] = y_off * ea_x[:, g * hw:(g + 1) * hw]

            hpg = hw // SSM_HEAD_DIM
            for pair in range(hpg // 2):
                h0 = g * hpg + 2 * pair
                c0 = h0 * SSM_HEAD_DIM
                xp = xd_bf[:, c0:c0 + LANES]
                yp = jnp.zeros((CHUNK, LANES), F32)
                for k, keep in ((0, head_lo), (1, ~head_lo)):
                    h = h0 + k
                    seg = a_cum[:, h:h + 1] - a_cum_t[h:h + 1, :]
                    m_h = (cb * jnp.exp(jnp.where(tri, seg, NEG_BIG))).astype(BF16)
                    yp = yp + jnp.dot(m_h, jnp.where(keep, xp, jnp.zeros_like(xp)),
                                      preferred_element_type=F32)
                y_scr[:, c0:c0 + LANES] = y_scr[:, c0:c0 + LANES] + yp

        y = y_scr[...] + xs * dskip_ref[...]
        hf = y * gz_ref[sl, :].astype(F32)
        for g in range(SSM_GROUPS):
            hg = hf[:, g * hw:(g + 1) * hw]
            ms = jnp.mean(hg * hg, axis=-1, keepdims=True)
            o_ref[sl, g * hw:(g + 1) * hw] = (hg * lax.rsqrt(ms + RMS_EPS)
                                               * g_ref[:, g * hw:(g + 1) * hw]).astype(o_ref.dtype)


def _ssd(xs, bm, cm, dt, gzs, alog, dskip_x, norm_g, e2, *, rows, nheads):
    s, nx = xs.shape
    hw = nx // SSM_GROUPS
    row = lambda a: pl.BlockSpec((rows, a.shape[1]), lambda i: (i, 0))
    full = lambda a: pl.BlockSpec(a.shape, lambda i: (0,) * a.ndim)
    return pl.pallas_call(
        functools.partial(_ssd_kernel, rows=rows, nheads=nheads),
        grid=(s // rows,),
        in_specs=[row(xs), row(bm), row(cm), row(dt), row(gzs),
                  full(alog), full(dskip_x), full(norm_g), full(e2)],
        out_specs=pl.BlockSpec((rows, nx), lambda i: (i, 0)),
        out_shape=jax.ShapeDtypeStruct((s, nx), BF16),
        scratch_shapes=[pltpu.VMEM((SSM_GROUPS, SSM_STATE, hw), F32),
                        pltpu.VMEM((CHUNK, nx), F32)],
        compiler_params=pltpu.CompilerParams(dimension_semantics=("arbitrary",),
                                             vmem_limit_bytes=VMEM_LIMIT),
        name="ssd",
    )(xs, bm, cm, dt, gzs, alog, dskip_x, norm_g, e2)


def _out_kernel(oa_ref, gza_ref, os_ref, x_ref, w32_ref, mod_ref, lng_ref, lnb_ref, o_ref, w_ref,
                *, alpha, sub):
    na = oa_ref.shape[1]

    @pl.when(pl.program_id(0) == 0)
    def _():
        for r in range(0, w_ref.shape[0], sub):
            w_ref[r:r + sub, :] = w32_ref[r:r + sub, :].astype(BF16)

    mixes = []
    for r in range(0, o_ref.shape[0], sub):
        rows = slice(r, r + sub)
        oa = (oa_ref[rows, :].astype(F32) * gza_ref[rows, :].astype(F32)).astype(BF16)
        mixed = jnp.dot(oa, w_ref[0:na, :], preferred_element_type=F32)
        mixes.append(mixed + jnp.dot(os_ref[rows, :], w_ref[na:, :], preferred_element_type=F32))
    for n, mixed in enumerate(mixes):
        rows = slice(n * sub, (n + 1) * sub)
        y = alpha * x_ref[rows, :] + mod_ref[2:3, :] * mixed
        mu = jnp.mean(y, axis=-1, keepdims=True)
        yc = y - mu
        var = jnp.mean(yc * yc, axis=-1, keepdims=True)
        o_ref[rows, :] = yc * lax.rsqrt(var + LN_EPS) * lng_ref[...] + lnb_ref[...]


def _out_proj(oa, gza, o_ssm, x2, w_out, mod3, ln_g, ln_b, *, tm, sub, alpha):
    s, d = x2.shape
    row = lambda a: pl.BlockSpec((tm, a.shape[1]), lambda i: (i, 0))
    full = lambda a: pl.BlockSpec(a.shape, lambda i: (0,) * a.ndim)
    return pl.pallas_call(
        functools.partial(_out_kernel, alpha=alpha, sub=sub),
        grid=(s // tm,),
        in_specs=[row(oa), row(gza), row(o_ssm), row(x2),
                  pl.BlockSpec(w_out.shape, lambda i: (0, 0), pipeline_mode=pl.Buffered(1)),
                  full(mod3), full(ln_g), full(ln_b)],
        out_specs=pl.BlockSpec((tm, d), lambda i: (i, 0)),
        out_shape=jax.ShapeDtypeStruct((s, d), F32),
        scratch_shapes=[pltpu.VMEM(w_out.shape, BF16)],
        compiler_params=pltpu.CompilerParams(dimension_semantics=("arbitrary",),
                                             vmem_limit_bytes=VMEM_LIMIT),
        name="out_proj",
    )(oa, gza, o_ssm, x2, w_out, mod3, ln_g, ln_b)


def _pad_cols(a, width):
    return jnp.pad(a, ((0, 0), (0, width - a.shape[1])))


def _regroup_kernel(wt_ref, o_ref, *, segments):
    wt = wt_ref[...]
    parts = [jnp.zeros((b, wt.shape[1]), wt.dtype) if a is None else wt[a:b, :] for a, b in segments]
    o_ref[...] = jnp.concatenate(parts, axis=0).T.astype(o_ref.dtype)


def _regroup_columns(w, segments, width):
    assert all(b % SUBLANES == 0 and (a or 0) % SUBLANES == 0 for a, b in segments)
    rows = w.shape[0]
    br = 256
    return pl.pallas_call(
        functools.partial(_regroup_kernel, segments=segments),
        grid=(rows // br,),
        in_specs=[pl.BlockSpec((w.shape[1], br), lambda i: (0, i))],
        out_specs=pl.BlockSpec((br, width), lambda i: (i, 0)),
        out_shape=jax.ShapeDtypeStruct((rows, width), BF16),
        compiler_params=pltpu.CompilerParams(vmem_limit_bytes=VMEM_LIMIT),
        name="regroup_w_in",
    )(w.T)


def _layer(x2, c, pos_row, w_ada, b_ada, w_in, q_norm_g, w_qb, kv_norm_g, w_kvb,
           conv_w, conv_b, dt_bias, a_log, d_skip, ssm_norm_g, w_out, ln_g, ln_b, *, depth):
    s, d = x2.shape
    q_rank = q_norm_g.shape[0]
    kv_rank = kv_norm_g.shape[0]
    nheads = dt_bias.shape[0]
    nch = conv_w.shape[1]
    nv = MLA_HEADS * V_HEAD_DIM
    nx = nheads * SSM_HEAD_DIM
    half = QK_ROPE_DIM // 2

    mod = _adaln_mod(c.reshape(d, 1), w_ada, b_ada.reshape(1, -1))
    mod3 = mod.reshape(3, d)

    o_q, o_ckv = 0, q_rank
    o_kr = o_ckv + kv_rank
    o_za = o_kr + QK_ROPE_DIM
    o_xbc = o_za + nv
    o_dt = o_xbc + nch
    o_zs = o_dt + nheads
    dt_pad = LANES - 2 * nheads
    groups = [("q", [(o_q, o_ckv)]), ("ckv", [(o_ckv, o_kr)]),
              ("krope", [(o_kr, o_za), (o_kr + half, o_za), (o_kr, o_kr + half)]),
              ("za", [(o_za, o_xbc)]), ("xbc", [(o_xbc, o_dt)]), ("zs", [(o_zs, o_zs + nx)]),
              ("dt", [(o_dt, o_zs), (o_dt, o_zs), (None, dt_pad)])]
    off, cur, segments = {}, 0, []
    for name, segs in groups:
        assert cur % LANES == 0
        off[name] = cur
        for a, b in segs:
            cur += b if a is None else b - a
            if segments and a is not None and segments[-1][0] is not None and segments[-1][1] == a:
                segments[-1] = (segments[-1][0], b)
            else:
                segments.append((a, b))
    assert cur % LANES == 0
    w_in_p = _regroup_columns(w_in, tuple(segments), cur)

    w3 = w_qb.reshape(q_rank, MLA_HEADS, QK_NOPE_DIM + QK_ROPE_DIM)
    w_rope = w3[:, :, QK_NOPE_DIM:]
    w_qb_p = jnp.concatenate([w3, w_rope[:, :, half:], w_rope[:, :, :half]], axis=2)
    w_qb_p = w_qb_p.reshape(q_rank, MLA_HEADS * QK_PAD).astype(BF16)

    inv_freq = 1.0 / (ROPE_THETA ** (jnp.arange(half, dtype=F32) / half))
    rope_tab = jnp.broadcast_to(inv_freq[:, None], (half, IN_SUB))

    dtb = _pad_cols(jnp.concatenate([dt_bias, dt_bias]).reshape(1, -1), LANES)
    q_scale = (QK_NOPE_DIM + QK_ROPE_DIM) ** -0.5 * LOG2E

    wkv3 = w_kvb.reshape(kv_rank, MLA_HEADS, QK_NOPE_DIM + V_HEAD_DIM)
    w_k = wkv3[:, :, :QK_NOPE_DIM].reshape(kv_rank, MLA_HEADS * QK_NOPE_DIM).astype(BF16)
    w_vt = wkv3[:, :, QK_NOPE_DIM:].reshape(kv_rank, nv).T.astype(BF16)

    q, k, vt, gza, xs, bm, cm, dt, gzs = _in_proj(
        x2, mod3, pos_row, rope_tab, w_in_p, q_norm_g.reshape(1, -1), w_qb_p,
        kv_norm_g.reshape(1, -1), w_k, w_vt, conv_w, conv_b.reshape(1, -1), dtb,
        off=off, tm=IN_ROWS, sub=IN_SUB, q_scale=q_scale)

    o_attn = _attention(q, k, vt, tq=1024, tk=1024)

    alog = _pad_cols(jnp.concatenate([a_log, a_log]).reshape(1, -1), LANES)
    e_head = jnp.repeat(jnp.eye(nheads, dtype=BF16), SSM_HEAD_DIM, axis=1)
    e2 = jnp.pad(jnp.concatenate([e_head, e_head], axis=0), ((0, LANES - 2 * nheads), (0, 0)))
    dskip_x = jnp.repeat(d_skip, SSM_HEAD_DIM).reshape(1, -1)
    o_ssm = _ssd(xs, bm, cm, dt, gzs, alog, dskip_x, ssm_norm_g.reshape(1, -1), e2,
                 rows=512, nheads=nheads)

    alpha = (2.0 * depth) ** 0.25
    return _out_proj(o_attn, gza, o_ssm, x2, w_out, mod3,
                     ln_g.reshape(1, -1), ln_b.reshape(1, -1), tm=OUT_ROWS, sub=OUT_SUB, alpha=alpha)


def kernel(x, c, positions, w_ada, b_ada, w_in, q_norm_g, w_qb, kv_norm_g, w_kvb, conv_w, conv_b,
           dt_bias, a_log, d_skip, ssm_norm_g, w_out, ln_g, ln_b):
    b, s, d = x.shape
    depth = w_in.shape[0]
    assert b == 1, "one sequence per call"
    h = x.reshape(s, d)
    pos_row = positions.reshape(1, s)
    for l in range(depth):
        h = _layer(h, c, pos_row, w_ada[l], b_ada[l], w_in[l], q_norm_g[l], w_qb[l], kv_norm_g[l],
                   w_kvb[l], conv_w[l], conv_b[l], dt_bias[l], a_log[l], d_skip[l], ssm_norm_g[l],
                   w_out[l], ln_g[l], ln_b[l], depth=depth)
    return h.reshape(b, s, d)
```

```python
import functools
import math

import jax
import jax.numpy as jnp
from jax import lax
from jax.experimental import pallas as pl
from jax.experimental.pallas import tpu as pltpu

F32 = jnp.float32
BF16 = jnp.bfloat16

MLA_HEADS = 8
QK_NOPE_DIM = 128
QK_ROPE_DIM = 64
V_HEAD_DIM = 128
ROPE_THETA = 10000.0
SSM_HEAD_DIM = 64
SSM_GROUPS = 2
SSM_STATE = 128
CHUNK = 128
RMS_EPS = 1e-6
LN_EPS = 1e-5

LANES = 128
SUBLANES = 8
MXU_DIM = 256
QK_PAD = MXU_DIM
VMEM_LIMIT = 56 * 1024 * 1024

IN_ROWS = 512
IN_SUB = 256
OUT_ROWS = 1024
OUT_SUB = 256
ATTN_TILE = 1024
ATTN_TILES_PER_STEP = 4
SSD_ROWS = 512

LOG2E = 1.4426950408889634
NEG_BIG = -1e30


def _silu(z):
    h = 0.5 * z
    return h * jnp.tanh(h) + h


def _softplus(z):
    return jnp.maximum(z, 0.0) + jnp.log1p(jnp.exp(-jnp.abs(z)))


def _split_bf16(x, parts):
    out, rem = [], x
    for _ in range(parts):
        hi = rem.astype(BF16)
        out.append(hi)
        rem = rem - hi.astype(F32)
    return out


def _mod_kernel(c_ref, w_ref, b_ref, o_ref):
    o_ref[...] = jnp.sum(w_ref[...] * c_ref[...], axis=0, keepdims=True) + b_ref[...]


def _adaln_mod(c_col, w_ada, b_ada):
    d, n = w_ada.shape
    bn = 1024
    return pl.pallas_call(
        _mod_kernel,
        grid=(n // bn,),
        in_specs=[pl.BlockSpec((d, 1), lambda j: (0, 0)),
                  pl.BlockSpec((d, bn), lambda j: (0, j)),
                  pl.BlockSpec((1, bn), lambda j: (0, j))],
        out_specs=pl.BlockSpec((1, bn), lambda j: (0, j)),
        out_shape=jax.ShapeDtypeStruct((1, n), F32),
        name="adaln_mod",
    )(c_col, w_ada, b_ada)


def _in_kernel(x_ref, mod_ref, pos_ref, rope_ref, w_in_ref, qg_ref, w_qb_ref, kvg_ref, w_k_ref, w_vt_ref,
               conv_w_ref, conv_b_ref, dtb_ref,
               q_out, k_out, vt_out, gza_out, xs_out, bm_out, cm_out, dt_out, gzs_out,
               xbuf, *, tm, sub, off, q_scale):
    i = pl.program_id(0)
    shift = mod_ref[0:1, :]
    scale1 = 1.0 + mod_ref[1:2, :]
    lane = lax.broadcasted_iota(jnp.int32, (sub, LANES), 1)
    low_half = lane < QK_ROPE_DIM
    nconv, nch = conv_w_ref.shape
    nx = xs_out.shape[1]
    nb = bm_out.shape[1]

    @pl.when(i == 0)
    def _():
        xbuf[0:SUBLANES, :] = jnp.zeros((SUBLANES, nch), F32)

    css = []
    for h in range(tm // sub):
        ang_t = pos_ref[:, h * sub:(h + 1) * sub].astype(F32) * rope_ref[...]
        cos_t, sin_t = jnp.cos(ang_t), jnp.sin(ang_t)
        css.append(jnp.concatenate([cos_t, cos_t, -sin_t, sin_t], axis=0).T)

    projs = []
    for h in range(tm // sub):
        u = x_ref[h * sub:(h + 1) * sub, :] * scale1 + shift
        projs.append(jnp.dot(u.astype(BF16), w_in_ref[...], preferred_element_type=F32))

    for h, (proj, cs) in enumerate(zip(projs, css)):
        rows = slice(h * sub, (h + 1) * sub)

        def rope(rr, cs=cs):
            t = rr * cs
            return jnp.where(low_half, t + pltpu.roll(t, QK_ROPE_DIM, axis=1), 0.0)

        q_lat = proj[:, off["q"]:off["q"] + qg_ref.shape[1]]
        qn = q_lat * lax.rsqrt(jnp.mean(q_lat * q_lat, axis=-1, keepdims=True) + RMS_EPS) * qg_ref[...]
        qf = jnp.dot(qn.astype(BF16), w_qb_ref[...], preferred_element_type=F32)
        for hd in range(MLA_HEADS):
            c0 = hd * QK_PAD
            q_out[rows, c0:c0 + QK_NOPE_DIM] = (qf[:, c0:c0 + QK_NOPE_DIM] * q_scale).astype(BF16)
            q_out[rows, c0 + QK_NOPE_DIM:c0 + QK_PAD] = (
                rope(qf[:, c0 + QK_NOPE_DIM:c0 + QK_PAD]) * q_scale).astype(BF16)

        c_kv = proj[:, off["ckv"]:off["ckv"] + kvg_ref.shape[1]]
        ckvn = c_kv * lax.rsqrt(jnp.mean(c_kv * c_kv, axis=-1, keepdims=True) + RMS_EPS) * kvg_ref[...]
        ckvn_bf = ckvn.astype(BF16)
        kf = jnp.dot(ckvn_bf, w_k_ref[...], preferred_element_type=F32)
        vt_out[:, rows] = lax.dot_general(w_vt_ref[...], ckvn_bf, (((1,), (1,)), ((), ())),
                                          preferred_element_type=F32).astype(BF16)
        k_rope = rope(proj[:, off["krope"]:off["krope"] + LANES]).astype(BF16)
        for hd in range(MLA_HEADS):
            k_out[rows, hd * QK_PAD:hd * QK_PAD + QK_NOPE_DIM] = (
                kf[:, hd * QK_NOPE_DIM:(hd + 1) * QK_NOPE_DIM].astype(BF16))
            k_out[rows, hd * QK_PAD + QK_NOPE_DIM:(hd + 1) * QK_PAD] = k_rope

        gza_out[rows, :] = _silu(proj[:, off["za"]:off["za"] + gza_out.shape[1]]).astype(BF16)
        gzs_out[rows, :] = _silu(proj[:, off["zs"]:off["zs"] + gzs_out.shape[1]]).astype(BF16)
        dt_out[rows, :] = _softplus(proj[:, off["dt"]:off["dt"] + LANES] + dtb_ref[...])

        r1 = SUBLANES + h * sub
        xbuf[r1:r1 + sub, :] = proj[:, off["xbc"]:off["xbc"] + nch]
        xfull = xbuf[r1 - SUBLANES:r1 + sub, :]
        acc = conv_b_ref[...] + conv_w_ref[nconv - 1:nconv, :] * xfull[SUBLANES:, :]
        for k in range(1, nconv):
            shifted = pltpu.roll(xfull, k, axis=0)[SUBLANES:, :]
            acc = acc + conv_w_ref[nconv - 1 - k:nconv - k, :] * shifted
        xc = _silu(acc)
        xs_out[rows, :] = xc[:, :nx].astype(BF16)
        bm_out[rows, :] = xc[:, nx:nx + nb].astype(BF16)
        cm_out[rows, :] = xc[:, nx + nb:nx + 2 * nb].astype(BF16)

    xbuf[0:SUBLANES, :] = xbuf[tm:tm + SUBLANES, :]


def _in_proj(x2, mod3, pos_row, rope_tab, w_in_p, qg, w_qb_p, kvg, w_k, w_vt, conv_w, conv_b, dtb,
             *, off, tm, sub, q_scale):
    s, d = x2.shape
    nq = MLA_HEADS * QK_PAD
    nv = MLA_HEADS * V_HEAD_DIM
    nch = conv_w.shape[1]
    nbc = SSM_GROUPS * SSM_STATE
    nx = nch - 2 * nbc
    row = lambda w: pl.BlockSpec((tm, w), lambda i: (i, 0))
    full = lambda a: pl.BlockSpec(a.shape, lambda i: (0,) * a.ndim, pipeline_mode=pl.Buffered(1))
    outs = [
        jax.ShapeDtypeStruct((s, nq), BF16),
        jax.ShapeDtypeStruct((s, nq), BF16),
        jax.ShapeDtypeStruct((nv, s), BF16),
        jax.ShapeDtypeStruct((s, nv), BF16),
        jax.ShapeDtypeStruct((s, nx), BF16),
        jax.ShapeDtypeStruct((s, nbc), BF16),
        jax.ShapeDtypeStruct((s, nbc), BF16),
        jax.ShapeDtypeStruct((s, LANES), F32),
        jax.ShapeDtypeStruct((s, nx), BF16),
    ]
    return pl.pallas_call(
        functools.partial(_in_kernel, tm=tm, sub=sub, off=off, q_scale=q_scale),
        grid=(s // tm,),
        in_specs=[row(d), full(mod3), pl.BlockSpec((1, tm), lambda i: (0, i)), full(rope_tab),
                  full(w_in_p), full(qg), full(w_qb_p), full(kvg), full(w_k), full(w_vt),
                  full(conv_w), full(conv_b), full(dtb)],
        out_specs=[pl.BlockSpec((nv, tm), lambda i: (0, i)) if n == 2 else row(o.shape[1])
                   for n, o in enumerate(outs)],
        out_shape=outs,
        scratch_shapes=[pltpu.VMEM((tm + 2 * SUBLANES, nch), F32)],
        compiler_params=pltpu.CompilerParams(dimension_semantics=("arbitrary",),
                                             vmem_limit_bytes=VMEM_LIMIT),
        name="in_proj",
    )(x2, mod3, pos_row, rope_tab, w_in_p, qg, w_qb_p, kvg, w_k, w_vt, conv_w, conv_b, dtb)


def _sublane_allmax(x):
    shift = SUBLANES // 2
    while shift:
        x = jnp.maximum(x, pltpu.roll(x, shift, axis=0))
        shift //= 2
    return x


def _sublane_allsum(x):
    shift = SUBLANES // 2
    while shift:
        x = x + pltpu.roll(x, shift, axis=0)
        shift //= 2
    return x


BF16_ROWS = 16
ACC_ROWS = V_HEAD_DIM + BF16_ROWS
EXP_ROWS = 32


def _attn_kernel(q_ref, k_ref, vt_ref, o_ref, m_a, acc_a, m_b, acc_b,
                 s_a, s_b, cmax_a, cmax_b, p_a, p_b, al_a, al_b, *, tq, tk, ntile):
    assert tq == tk
    g = pl.program_id(1)
    nsub = tk // SUBLANES
    nacc = ACC_ROWS // SUBLANES
    buf_a = (s_a, cmax_a, p_a, al_a)
    buf_b = (s_b, cmax_b, p_b, al_b)
    ones_rows = jnp.ones((ACC_ROWS - V_HEAD_DIM, tk), BF16)

    def reset(stats):
        m_scr, acc_scr = stats
        m_scr[...] = jnp.full(m_scr.shape, NEG_BIG, F32)
        acc_scr[...] = jnp.zeros(acc_scr.shape, F32)

    def qk_stage(kv, q_rows, buf, diag=False):
        s_out, cmax_out = buf[0], buf[1]
        start = pl.multiple_of(kv * tk, tk)
        nt = (((1,), (1,)), ((), ()))
        colmax = lambda x: _sublane_allmax(jnp.max(x.reshape(x.shape[0] // SUBLANES, SUBLANES, x.shape[1]), axis=0))
        if not diag:
            st = lax.dot_general(k_ref[pl.ds(start, tk), :], q_ref[q_rows, :], nt,
                                 preferred_element_type=F32)
            s_out[...] = st
            cmax_out[...] = colmax(st)
            return
        hk = tk // 2
        q_lo = q_rows.start
        mask = (lax.broadcasted_iota(jnp.int32, (hk, tq), 0) <= lax.broadcasted_iota(jnp.int32, (hk, tq), 1))
        top = lax.dot_general(k_ref[pl.ds(start, hk), :], q_ref[q_rows, :], nt, preferred_element_type=F32)
        top = jnp.where(mask, top, NEG_BIG)
        bot = lax.dot_general(k_ref[pl.ds(start + hk, hk), :], q_ref[q_lo + hk:q_lo + tq, :], nt,
                              preferred_element_type=F32)
        bot = jnp.where(mask[:, :tq - hk], bot, NEG_BIG)
        s_out[0:hk, :] = top
        s_out[hk:tk, 0:hk] = jnp.full((tk - hk, hk), NEG_BIG, F32)
        s_out[hk:tk, hk:tq] = bot
        cm_top = colmax(top)
        cmax_out[:, 0:hk] = cm_top[:, 0:hk]
        cmax_out[:, hk:tq] = jnp.maximum(cm_top[:, hk:tq], colmax(bot))

    def softmax_stage(buf, stats):
        s_in, cmax_in, p_out, al_out = buf
        m_scr = stats[0]
        m_old = m_scr[...]
        m_new = jnp.maximum(m_old, cmax_in[...])
        al_out[...] = jnp.exp2(m_old - m_new)
        m_scr[...] = m_new
        for c in range(0, tq, MXU_DIM):
            m_c = m_new[:, c:c + MXU_DIM][None]
            for r in range(0, tk, EXP_ROWS):
                sc = s_in[r:r + EXP_ROWS, c:c + MXU_DIM].reshape(EXP_ROWS // SUBLANES, SUBLANES, MXU_DIM)
                p_out[r:r + EXP_ROWS, c:c + MXU_DIM] = (
                    jnp.exp2(sc - m_c).reshape(EXP_ROWS, MXU_DIM).astype(BF16))

    def pv_stage(kv, buf, stats):
        p_in, al_in = buf[2], buf[3]
        acc_scr = stats[1]
        start = pl.multiple_of(jnp.maximum(kv, 0) * tk, tk)
        vt_aug = jnp.concatenate([vt_ref[:, pl.ds(start, tk)], ones_rows], axis=0)
        pv = jnp.dot(vt_aug, p_in[...], preferred_element_type=F32)
        acc3 = acc_scr[...].reshape(nacc, SUBLANES, tq) * al_in[...][None]
        acc_scr[...] = acc3.reshape(ACC_ROWS, tq) + pv

    def finalize(q_rows, stats):
        acc_scr = stats[1]
        inv_l = 1.0 / acc_scr[V_HEAD_DIM:V_HEAD_DIM + SUBLANES, :]
        o_t = acc_scr[0:V_HEAD_DIM, :].reshape(V_HEAD_DIM // SUBLANES, SUBLANES, tq) * inv_l[None]
        o_ref[q_rows, :] = o_t.reshape(V_HEAD_DIM, tq).T.astype(o_ref.dtype)

    def make_step(qi, q_rows, stats):
        kv_of = lambda pos: jnp.where(pos == 0, qi, pos - 1)

        def step(pos, buf, other):
            qk_stage(kv_of(pos), q_rows, buf)
            pv_stage(kv_of(pos - 2), buf, stats)
            softmax_stage(other, stats)

        return kv_of, step

    def run_pairs(lo, hi, pair):
        def body(pp, carry):
            pair(pp)
            return carry

        lax.fori_loop(lo, hi, body, 0)

    bufs = (buf_a, buf_b)
    stats2 = ((m_a, acc_a), (m_b, acc_b))
    start = [0]
    for j in range(ntile - 1):
        last = start[j] if j % 2 == 0 else 1 - start[j]
        start.append(1 - last)

    def tile_ctx(j):
        qi = ntile * g + j
        rows = slice(j * tq, (j + 1) * tq)
        stats = stats2[j % 2]
        x, y = bufs[start[j]], bufs[1 - start[j]]
        kv_of, step = make_step(qi, rows, stats)
        return qi, rows, stats, x, y, kv_of, step

    def run_steps(j, first_pos, n_pairs):
        _, _, _, x, y, _, step = tile_ctx(j)
        even_buf, odd_buf = x, y

        def pair(pp):
            p0 = first_pos + 2 * pp
            if first_pos % 2 == 0:
                step(p0, even_buf, odd_buf)
                step(p0 + 1, odd_buf, even_buf)
            else:
                step(p0, odd_buf, even_buf)
                step(p0 + 1, even_buf, odd_buf)

        run_pairs(0, n_pairs, pair)

    qi0, rows0, st0, x0, y0, kv0, step0 = tile_ctx(0)
    reset(st0)
    qk_stage(qi0, rows0, x0, diag=True)

    @pl.when(g == 0)
    def _():
        y0[2][...] = jnp.zeros(y0[2].shape, BF16)
        y0[3][...] = jnp.ones(y0[3].shape, F32)

    @pl.when(g > 0)
    def _():
        qk_stage(kv0(1), rows0, y0)
        softmax_stage(x0, st0)
        step0(2, x0, y0)
        run_steps(0, 3, (ntile * g - 2) // 2)

    for j in range(ntile):
        qi, rows, stats, x, y, kv_of, step = tile_ctx(j)
        if j > 0:
            if j % 2 == 0:
                step(2, x, y)
                run_steps(j, 3, (ntile * g + j - 2) // 2)
            else:
                run_steps(j, 2, (ntile * g + j - 1) // 2)
        last_buf, other = (x, y) if j % 2 == 0 else (y, x)
        pv_stage(kv_of(qi - 1), other, stats)
        if j + 1 < ntile:
            qi_n, rows_n, st_n, x_n, y_n, kv_n, _ = tile_ctx(j + 1)
            assert x_n is other and y_n is last_buf
            reset(st_n)
            qk_stage(qi_n, rows_n, x_n, diag=True)
            softmax_stage(last_buf, stats)
            pv_stage(kv_of(qi), last_buf, stats)
            qk_stage(kv_n(1), rows_n, y_n)
            softmax_stage(x_n, st_n)
        else:
            softmax_stage(last_buf, stats)
            pv_stage(kv_of(qi), last_buf, stats)
        finalize(rows, stats)


def _attention(q, k, vt, *, tq, tk, ntile):
    s = q.shape[0]
    assert ntile % 2 == 0 and s % (ntile * tq) == 0, "sequence length must be a multiple of ntile query tiles"
    stat = pltpu.VMEM((SUBLANES, tq), F32)
    return pl.pallas_call(
        functools.partial(_attn_kernel, tq=tq, tk=tk, ntile=ntile),
        grid=(MLA_HEADS, s // (ntile * tq)),
        in_specs=[pl.BlockSpec((ntile * tq, QK_PAD), lambda h, i: (i, h)),
                  pl.BlockSpec((s, QK_PAD), lambda h, i: (0, h)),
                  pl.BlockSpec((V_HEAD_DIM, s), lambda h, i: (h, 0))],
        out_specs=pl.BlockSpec((ntile * tq, V_HEAD_DIM), lambda h, i: (i, h)),
        out_shape=jax.ShapeDtypeStruct((s, MLA_HEADS * V_HEAD_DIM), BF16),
        scratch_shapes=[stat, pltpu.VMEM((ACC_ROWS, tq), F32), stat, pltpu.VMEM((ACC_ROWS, tq), F32),
                        pltpu.VMEM((tk, tq), F32), pltpu.VMEM((tk, tq), F32), stat, stat,
                        pltpu.VMEM((tk, tq), BF16), pltpu.VMEM((tk, tq), BF16), stat, stat],
        compiler_params=pltpu.CompilerParams(dimension_semantics=("arbitrary", "arbitrary"),
                                             vmem_limit_bytes=VMEM_LIMIT),
        name="attention",
    )(q, k, vt)


def _ssd_kernel(xs_ref, bm_ref, cm_ref, dt_ref, gz_ref, alog_ref, dskip_ref, g_ref, e2_ref,
                o_ref, state, y_scr, *, rows, nheads):
    i = pl.program_id(0)
    gw = SSM_STATE
    hw = state.shape[2]

    @pl.when(i == 0)
    def _():
        state[...] = jnp.zeros(state.shape, F32)

    a_neg = -jnp.exp(alog_ref[...])
    r_i = lax.broadcasted_iota(jnp.int32, (CHUNK, CHUNK), 0)
    c_i = lax.broadcasted_iota(jnp.int32, (CHUNK, CHUNK), 1)
    tri = c_i <= r_i
    tri_bf = tri.astype(BF16)
    lane = lax.broadcasted_iota(jnp.int32, (CHUNK, LANES), 1)
    first_copy = lane < nheads
    head_lo = lane < SSM_HEAD_DIM
    e2 = e2_ref[...]

    def expand(v):
        hi = v.astype(BF16)
        lo = (v - hi.astype(F32)).astype(BF16)
        return jnp.dot(jnp.where(first_copy, hi, lo), e2, preferred_element_type=F32)

    for cidx in range(rows // CHUNK):
        sl = pl.ds(cidx * CHUNK, CHUNK)
        dt = dt_ref[sl, :]
        da = dt * a_neg
        a_cum = jnp.zeros((CHUNK, LANES), F32)
        for part in _split_bf16(da, 3):
            a_cum = a_cum + jnp.dot(tri_bf, part, preferred_element_type=F32)
        a_cum_t = a_cum.T
        a_last = a_cum[CHUNK - 1:CHUNK, :]
        ea = jnp.exp(a_cum)
        dte = jnp.exp(a_last - a_cum)

        xs = xs_ref[sl, :].astype(F32)
        xd = xs * expand(dt)
        ea_x = expand(ea)
        xdd = (xd * expand(dte)).astype(BF16)
        xd_bf = xd.astype(BF16)

        for g in range(SSM_GROUPS):
            bg = bm_ref[sl, g * gw:(g + 1) * gw]
            cg = cm_ref[sl, g * gw:(g + 1) * gw]
            cb = lax.dot_general(cg, bg, (((1,), (1,)), ((), ())), preferred_element_type=F32)
            prev = state[g]
            y_off = jnp.dot(cg, prev.astype(BF16), preferred_element_type=F32)
            new = lax.dot_general(bg, xdd[:, g * hw:(g + 1) * hw], (((0,), (0,)), ((), ())),
                                  preferred_element_type=F32)
            state[g] = prev * ea_x[CHUNK - 1:CHUNK, g * hw:(g + 1) * hw] + new
            y_scr[:, g * hw:(g + 1) * hw] = y_off * ea_x[:, g * hw:(g + 1) * hw]

            hpg = hw // SSM_HEAD_DIM
            for pair in range(hpg // 2):
                h0 = g * hpg + 2 * pair
                c0 = h0 * SSM_HEAD_DIM
                xp = xd_bf[:, c0:c0 + LANES]
                yp = jnp.zeros((CHUNK, LANES), F32)
                for k, keep in ((0, head_lo), (1, ~head_lo)):
                    h = h0 + k
                    seg = a_cum[:, h:h + 1] - a_cum_t[h:h + 1, :]
                    m_h = (cb * jnp.exp(jnp.where(tri, seg, NEG_BIG))).astype(BF16)
                    yp = yp + jnp.dot(m_h, jnp.where(keep, xp, jnp.zeros_like(xp)),
                                      preferred_element_type=F32)
                y_scr[:, c0:c0 + LANES] = y_scr[:, c0:c0 + LANES] + yp

        y = y_scr[...] + xs * dskip_ref[...]
        hf = y * gz_ref[sl, :].astype(F32)
        for g in range(SSM_GROUPS):
            hg = hf[:, g * hw:(g + 1) * hw]
            ms = jnp.mean(hg * hg, axis=-1, keepdims=True)
            o_ref[sl, g * hw:(g + 1) * hw] = (hg * lax.rsqrt(ms + RMS_EPS)
                                               * g_ref[:, g * hw:(g + 1) * hw]).astype(o_ref.dtype)


def _ssd(xs, bm, cm, dt, gzs, alog, dskip_x, norm_g, e2, *, rows, nheads):
    s, nx = xs.shape
    hw = nx // SSM_GROUPS
    row = lambda a: pl.BlockSpec((rows, a.shape[1]), lambda i: (i, 0))
    full = lambda a: pl.BlockSpec(a.shape, lambda i: (0,) * a.ndim)
    return pl.pallas_call(
        functools.partial(_ssd_kernel, rows=rows, nheads=nheads),
        grid=(s // rows,),
        in_specs=[row(xs), row(bm), row(cm), row(dt), row(gzs),
                  full(alog), full(dskip_x), full(norm_g), full(e2)],
        out_specs=pl.BlockSpec((rows, nx), lambda i: (i, 0)),
        out_shape=jax.ShapeDtypeStruct((s, nx), BF16),
        scratch_shapes=[pltpu.VMEM((SSM_GROUPS, SSM_STATE, hw), F32),
                        pltpu.VMEM((CHUNK, nx), F32)],
        compiler_params=pltpu.CompilerParams(dimension_semantics=("arbitrary",),
                                             vmem_limit_bytes=VMEM_LIMIT),
        name="ssd",
    )(xs, bm, cm, dt, gzs, alog, dskip_x, norm_g, e2)


def _out_kernel(oa_ref, gza_ref, os_ref, x_ref, w32_ref, mod_ref, lng_ref, lnb_ref, o_ref, w_ref,
                *, alpha, sub):
    na = oa_ref.shape[1]

    @pl.when(pl.program_id(0) == 0)
    def _():
        for r in range(0, w_ref.shape[0], sub):
            w_ref[r:r + sub, :] = w32_ref[r:r + sub, :].astype(BF16)

    mixes = []
    for r in range(0, o_ref.shape[0], sub):
        rows = slice(r, r + sub)
        oa = (oa_ref[rows, :].astype(F32) * gza_ref[rows, :].astype(F32)).astype(BF16)
        mixed = jnp.dot(oa, w_ref[0:na, :], preferred_element_type=F32)
        mixes.append(mixed + jnp.dot(os_ref[rows, :], w_ref[na:, :], preferred_element_type=F32))
    for n, mixed in enumerate(mixes):
        rows = slice(n * sub, (n + 1) * sub)
        y = alpha * x_ref[rows, :] + mod_ref[2:3, :] * mixed
        mu = jnp.mean(y, axis=-1, keepdims=True)
        yc = y - mu
        var = jnp.mean(yc * yc, axis=-1, keepdims=True)
        o_ref[rows, :] = yc * lax.rsqrt(var + LN_EPS) * lng_ref[...] + lnb_ref[...]


def _out_proj(oa, gza, o_ssm, x2, w_out, mod3, ln_g, ln_b, *, tm, sub, alpha):
    s, d = x2.shape
    row = lambda a: pl.BlockSpec((tm, a.shape[1]), lambda i: (i, 0))
    full = lambda a: pl.BlockSpec(a.shape, lambda i: (0,) * a.ndim)
    return pl.pallas_call(
        functools.partial(_out_kernel, alpha=alpha, sub=sub),
        grid=(s // tm,),
        in_specs=[row(oa), row(gza), row(o_ssm), row(x2),
                  pl.BlockSpec(w_out.shape, lambda i: (0, 0), pipeline_mode=pl.Buffered(1)),
                  full(mod3), full(ln_g), full(ln_b)],
        out_specs=pl.BlockSpec((tm, d), lambda i: (i, 0)),
        out_shape=jax.ShapeDtypeStruct((s, d), F32),
        scratch_shapes=[pltpu.VMEM(w_out.shape, BF16)],
        compiler_params=pltpu.CompilerParams(dimension_semantics=("arbitrary",),
                                             vmem_limit_bytes=VMEM_LIMIT),
        name="out_proj",
    )(oa, gza, o_ssm, x2, w_out, mod3, ln_g, ln_b)


def _pad_cols(a, width):
    return jnp.pad(a, ((0, 0), (0, width - a.shape[1])))


def _regroup_kernel(wt_ref, o_ref, *, segments):
    wt = wt_ref[...]
    parts = [jnp.zeros((b, wt.shape[1]), wt.dtype) if a is None else wt[a:b, :] for a, b in segments]
    o_ref[...] = jnp.concatenate(parts, axis=0).T.astype(o_ref.dtype)


def _regroup_columns(w, segments, width):
    assert all(b % SUBLANES == 0 and (a or 0) % SUBLANES == 0 for a, b in segments)
    rows = w.shape[0]
    br = 256
    return pl.pallas_call(
        functools.partial(_regroup_kernel, segments=segments),
        grid=(rows // br,),
        in_specs=[pl.BlockSpec((w.shape[1], br), lambda i: (0, i))],
        out_specs=pl.BlockSpec((br, width), lambda i: (i, 0)),
        out_shape=jax.ShapeDtypeStruct((rows, width), BF16),
        compiler_params=pltpu.CompilerParams(vmem_limit_bytes=VMEM_LIMIT),
        name="regroup_w_in",
    )(w.T)


def _layer(x2, c, pos_row, w_ada, b_ada, w_in, q_norm_g, w_qb, kv_norm_g, w_kvb,
           conv_w, conv_b, dt_bias, a_log, d_skip, ssm_norm_g, w_out, ln_g, ln_b, *, depth):
    s, d = x2.shape
    q_rank = q_norm_g.shape[0]
    kv_rank = kv_norm_g.shape[0]
    nheads = dt_bias.shape[0]
    nch = conv_w.shape[1]
    nv = MLA_HEADS * V_HEAD_DIM
    nx = nheads * SSM_HEAD_DIM
    half = QK_ROPE_DIM // 2

    mod = _adaln_mod(c.reshape(d, 1), w_ada, b_ada.reshape(1, -1))
    mod3 = mod.reshape(3, d)

    o_q, o_ckv = 0, q_rank
    o_kr = o_ckv + kv_rank
    o_za = o_kr + QK_ROPE_DIM
    o_xbc = o_za + nv
    o_dt = o_xbc + nch
    o_zs = o_dt + nheads
    dt_pad = LANES - 2 * nheads
    groups = [("q", [(o_q, o_ckv)]), ("ckv", [(o_ckv, o_kr)]),
              ("krope", [(o_kr, o_za), (o_kr + half, o_za), (o_kr, o_kr + half)]),
              ("za", [(o_za, o_xbc)]), ("xbc", [(o_xbc, o_dt)]), ("zs", [(o_zs, o_zs + nx)]),
              ("dt", [(o_dt, o_zs), (o_dt, o_zs), (None, dt_pad)])]
    off, cur, segments = {}, 0, []
    for name, segs in groups:
        assert cur % LANES == 0
        off[name] = cur
        for a, b in segs:
            cur += b if a is None else b - a
            if segments and a is not None and segments[-1][0] is not None and segments[-1][1] == a:
                segments[-1] = (segments[-1][0], b)
            else:
                segments.append((a, b))
    assert cur % LANES == 0
    w_in_p = _regroup_columns(w_in, tuple(segments), cur)

    w3 = w_qb.reshape(q_rank, MLA_HEADS, QK_NOPE_DIM + QK_ROPE_DIM)
    w_rope = w3[:, :, QK_NOPE_DIM:]
    w_qb_p = jnp.concatenate([w3, w_rope[:, :, half:], w_rope[:, :, :half]], axis=2)
    w_qb_p = w_qb_p.reshape(q_rank, MLA_HEADS * QK_PAD).astype(BF16)

    inv_freq = 1.0 / (ROPE_THETA ** (jnp.arange(half, dtype=F32) / half))
    rope_tab = jnp.broadcast_to(inv_freq[:, None], (half, IN_SUB))

    dtb = _pad_cols(jnp.concatenate([dt_bias, dt_bias]).reshape(1, -1), LANES)
    q_scale = (QK_NOPE_DIM + QK_ROPE_DIM) ** -0.5 * LOG2E

    wkv3 = w_kvb.reshape(kv_rank, MLA_HEADS, QK_NOPE_DIM + V_HEAD_DIM)
    w_k = wkv3[:, :, :QK_NOPE_DIM].reshape(kv_rank, MLA_HEADS * QK_NOPE_DIM).astype(BF16)
    w_vt = wkv3[:, :, QK_NOPE_DIM:].reshape(kv_rank, nv).T.astype(BF16)

    q, k, vt, gza, xs, bm, cm, dt, gzs = _in_proj(
        x2, mod3, pos_row, rope_tab, w_in_p, q_norm_g.reshape(1, -1), w_qb_p,
        kv_norm_g.reshape(1, -1), w_k, w_vt, conv_w, conv_b.reshape(1, -1), dtb,
        off=off, tm=IN_ROWS, sub=IN_SUB, q_scale=q_scale)

    o_attn = _attention(q, k, vt, tq=ATTN_TILE, tk=ATTN_TILE, ntile=ATTN_TILES_PER_STEP)

    alog = _pad_cols(jnp.concatenate([a_log, a_log]).reshape(1, -1), LANES)
    e_head = jnp.repeat(jnp.eye(nheads, dtype=BF16), SSM_HEAD_DIM, axis=1)
    e2 = jnp.pad(jnp.concatenate([e_head, e_head], axis=0), ((0, LANES - 2 * nheads), (0, 0)))
    dskip_x = jnp.repeat(d_skip, SSM_HEAD_DIM).reshape(1, -1)
    o_ssm = _ssd(xs, bm, cm, dt, gzs, alog, dskip_x, ssm_norm_g.reshape(1, -1), e2,
                 rows=SSD_ROWS, nheads=nheads)

    alpha = (2.0 * depth) ** 0.25
    return _out_proj(o_attn, gza, o_ssm, x2, w_out, mod3,
                     ln_g.reshape(1, -1), ln_b.reshape(1, -1), tm=OUT_ROWS, sub=OUT_SUB, alpha=alpha)


def kernel(x, c, positions, w_ada, b_ada, w_in, q_norm_g, w_qb, kv_norm_g, w_kvb, conv_w, conv_b,
           dt_bias, a_log, d_skip, ssm_norm_g, w_out, ln_g, ln_b):
    b, s, d = x.shape
    depth = w_in.shape[0]
    assert b == 1, "one sequence per call"
    h = x.reshape(s, d)
    pos_row = positions.reshape(1, s)
    for l in range(depth):
        h = _layer(h, c, pos_row, w_ada[l], b_ada[l], w_in[l], q_norm_g[l], w_qb[l], kv_norm_g[l],
                   w_kvb[l], conv_w[l], conv_b[l], dt_bias[l], a_log[l], d_skip[l], ssm_norm_g[l],
                   w_out[l], ln_g[l], ln_b[l], depth=depth)
    return h.reshape(b, s, d)
```

```python
import functools
import math

import jax
import jax.numpy as jnp
from jax import lax
from jax.experimental import pallas as pl
from jax.experimental.pallas import tpu as pltpu

F32 = jnp.float32
BF16 = jnp.bfloat16

MLA_HEADS = 8
QK_NOPE_DIM = 128
QK_ROPE_DIM = 64
V_HEAD_DIM = 128
ROPE_THETA = 10000.0
SSM_HEAD_DIM = 64
SSM_GROUPS = 2
SSM_STATE = 128
CHUNK = 128
RMS_EPS = 1e-6
LN_EPS = 1e-5

LANES = 128
SUBLANES = 8
MXU_DIM = 256
QK_PAD = MXU_DIM
VMEM_LIMIT = 56 * 1024 * 1024

IN_ROWS = 512
IN_SUB = 256
OUT_ROWS = 1024
OUT_SUB = 256
ATTN_TILE = 1024
ATTN_TILES_PER_STEP = 2
SSD_ROWS = 512

LOG2E = 1.4426950408889634
NEG_BIG = -1e30


def _silu(z):
    h = 0.5 * z
    return h * jnp.tanh(h) + h


def _softplus(z):
    return jnp.maximum(z, 0.0) + jnp.log1p(jnp.exp(-jnp.abs(z)))


def _split_bf16(x, parts):
    out, rem = [], x
    for _ in range(parts):
        hi = rem.astype(BF16)
        out.append(hi)
        rem = rem - hi.astype(F32)
    return out


def _mod_kernel(c_ref, w_ref, b_ref, o_ref):
    o_ref[...] = jnp.sum(w_ref[...] * c_ref[...], axis=0, keepdims=True) + b_ref[...]


def _adaln_mod(c_col, w_ada, b_ada):
    d, n = w_ada.shape
    bn = 1024
    return pl.pallas_call(
        _mod_kernel,
        grid=(n // bn,),
        in_specs=[pl.BlockSpec((d, 1), lambda j: (0, 0)),
                  pl.BlockSpec((d, bn), lambda j: (0, j)),
                  pl.BlockSpec((1, bn), lambda j: (0, j))],
        out_specs=pl.BlockSpec((1, bn), lambda j: (0, j)),
        out_shape=jax.ShapeDtypeStruct((1, n), F32),
        name="adaln_mod",
    )(c_col, w_ada, b_ada)


def _in_kernel(x_ref, mod_ref, pos_ref, rope_ref, w_in_ref, qg_ref, w_qb_ref, kvg_ref, w_k_ref, w_vt_ref,
               conv_w_ref, conv_b_ref, dtb_ref,
               q_out, k_out, vt_out, gza_out, xs_out, bm_out, cm_out, dt_out, gzs_out,
               xbuf, *, tm, sub, off, q_scale):
    i = pl.program_id(0)
    shift = mod_ref[0:1, :]
    scale1 = 1.0 + mod_ref[1:2, :]
    lane = lax.broadcasted_iota(jnp.int32, (sub, LANES), 1)
    low_half = lane < QK_ROPE_DIM
    nconv, nch = conv_w_ref.shape
    nx = xs_out.shape[1]
    nb = bm_out.shape[1]

    @pl.when(i == 0)
    def _():
        xbuf[0:SUBLANES, :] = jnp.zeros((SUBLANES, nch), F32)

    css = []
    for h in range(tm // sub):
        ang_t = pos_ref[:, h * sub:(h + 1) * sub].astype(F32) * rope_ref[...]
        cos_t, sin_t = jnp.cos(ang_t), jnp.sin(ang_t)
        css.append(jnp.concatenate([cos_t, cos_t, -sin_t, sin_t], axis=0).T)

    projs = []
    for h in range(tm // sub):
        u = x_ref[h * sub:(h + 1) * sub, :] * scale1 + shift
        projs.append(jnp.dot(u.astype(BF16), w_in_ref[...], preferred_element_type=F32))

    for h, (proj, cs) in enumerate(zip(projs, css)):
        rows = slice(h * sub, (h + 1) * sub)

        def rope(rr, cs=cs):
            t = rr * cs
            return jnp.where(low_half, t + pltpu.roll(t, QK_ROPE_DIM, axis=1), 0.0)

        q_lat = proj[:, off["q"]:off["q"] + qg_ref.shape[1]]
        qn = q_lat * lax.rsqrt(jnp.mean(q_lat * q_lat, axis=-1, keepdims=True) + RMS_EPS) * qg_ref[...]
        qf = jnp.dot(qn.astype(BF16), w_qb_ref[...], preferred_element_type=F32)
        for hd in range(MLA_HEADS):
            c0 = hd * QK_PAD
            q_out[rows, c0:c0 + QK_NOPE_DIM] = (qf[:, c0:c0 + QK_NOPE_DIM] * q_scale).astype(BF16)
            q_out[rows, c0 + QK_NOPE_DIM:c0 + QK_PAD] = (
                rope(qf[:, c0 + QK_NOPE_DIM:c0 + QK_PAD]) * q_scale).astype(BF16)

        c_kv = proj[:, off["ckv"]:off["ckv"] + kvg_ref.shape[1]]
        ckvn = c_kv * lax.rsqrt(jnp.mean(c_kv * c_kv, axis=-1, keepdims=True) + RMS_EPS) * kvg_ref[...]
        ckvn_bf = ckvn.astype(BF16)
        kf = jnp.dot(ckvn_bf, w_k_ref[...], preferred_element_type=F32)
        vt_out[:, rows] = lax.dot_general(w_vt_ref[...], ckvn_bf, (((1,), (1,)), ((), ())),
                                          preferred_element_type=F32).astype(BF16)
        k_rope = rope(proj[:, off["krope"]:off["krope"] + LANES]).astype(BF16)
        for hd in range(MLA_HEADS):
            k_out[rows, hd * QK_PAD:hd * QK_PAD + QK_NOPE_DIM] = (
                kf[:, hd * QK_NOPE_DIM:(hd + 1) * QK_NOPE_DIM].astype(BF16))
            k_out[rows, hd * QK_PAD + QK_NOPE_DIM:(hd + 1) * QK_PAD] = k_rope

        gza_out[rows, :] = _silu(proj[:, off["za"]:off["za"] + gza_out.shape[1]]).astype(BF16)
        gzs_out[rows, :] = _silu(proj[:, off["zs"]:off["zs"] + gzs_out.shape[1]]).astype(BF16)
        dt_out[rows, :] = _softplus(proj[:, off["dt"]:off["dt"] + LANES] + dtb_ref[...])

        r1 = SUBLANES + h * sub
        xbuf[r1:r1 + sub, :] = proj[:, off["xbc"]:off["xbc"] + nch]
        xfull = xbuf[r1 - SUBLANES:r1 + sub, :]
        acc = conv_b_ref[...] + conv_w_ref[nconv - 1:nconv, :] * xfull[SUBLANES:, :]
        for k in range(1, nconv):
            shifted = pltpu.roll(xfull, k, axis=0)[SUBLANES:, :]
            acc = acc + conv_w_ref[nconv - 1 - k:nconv - k, :] * shifted
        xc = _silu(acc)
        xs_out[rows, :] = xc[:, :nx].astype(BF16)
        bm_out[rows, :] = xc[:, nx:nx + nb].astype(BF16)
        cm_out[rows, :] = xc[:, nx + nb:nx + 2 * nb].astype(BF16)

    xbuf[0:SUBLANES, :] = xbuf[tm:tm + SUBLANES, :]


def _in_proj(x2, mod3, pos_row, rope_tab, w_in_p, qg, w_qb_p, kvg, w_k, w_vt, conv_w, conv_b, dtb,
             *, off, tm, sub, q_scale):
    s, d = x2.shape
    nq = MLA_HEADS * QK_PAD
    nv = MLA_HEADS * V_HEAD_DIM
    nch = conv_w.shape[1]
    nbc = SSM_GROUPS * SSM_STATE
    nx = nch - 2 * nbc
    row = lambda w: pl.BlockSpec((tm, w), lambda i: (i, 0))
    full = lambda a: pl.BlockSpec(a.shape, lambda i: (0,) * a.ndim, pipeline_mode=pl.Buffered(1))
    outs = [
        jax.ShapeDtypeStruct((s, nq), BF16),
        jax.ShapeDtypeStruct((s, nq), BF16),
        jax.ShapeDtypeStruct((nv, s), BF16),
        jax.ShapeDtypeStruct((s, nv), BF16),
        jax.ShapeDtypeStruct((s, nx), BF16),
        jax.ShapeDtypeStruct((s, nbc), BF16),
        jax.ShapeDtypeStruct((s, nbc), BF16),
        jax.ShapeDtypeStruct((s, LANES), F32),
        jax.ShapeDtypeStruct((s, nx), BF16),
    ]
    return pl.pallas_call(
        functools.partial(_in_kernel, tm=tm, sub=sub, off=off, q_scale=q_scale),
        grid=(s // tm,),
        in_specs=[row(d), full(mod3), pl.BlockSpec((1, tm), lambda i: (0, i)), full(rope_tab),
                  full(w_in_p), full(qg), full(w_qb_p), full(kvg), full(w_k), full(w_vt),
                  full(conv_w), full(conv_b), full(dtb)],
        out_specs=[pl.BlockSpec((nv, tm), lambda i: (0, i)) if n == 2 else row(o.shape[1])
                   for n, o in enumerate(outs)],
        out_shape=outs,
        scratch_shapes=[pltpu.VMEM((tm + 2 * SUBLANES, nch), F32)],
        compiler_params=pltpu.CompilerParams(dimension_semantics=("arbitrary",),
                                             vmem_limit_bytes=VMEM_LIMIT),
        name="in_proj",
    )(x2, mod3, pos_row, rope_tab, w_in_p, qg, w_qb_p, kvg, w_k, w_vt, conv_w, conv_b, dtb)


def _sublane_allmax(x):
    shift = SUBLANES // 2
    while shift:
        x = jnp.maximum(x, pltpu.roll(x, shift, axis=0))
        shift //= 2
    return x


def _sublane_allsum(x):
    shift = SUBLANES // 2
    while shift:
        x = x + pltpu.roll(x, shift, axis=0)
        shift //= 2
    return x


BF16_ROWS = 16
ACC_ROWS = V_HEAD_DIM + BF16_ROWS
EXP_ROWS = 32


def _attn_kernel(q_ref, k_ref, vt_ref, o_ref, m_a, acc_a, m_b, acc_b,
                 s_a, s_b, cmax_a, cmax_b, p_a, p_b, al_a, al_b, *, tq, tk, ntile):
    assert tq == tk
    g = pl.program_id(1)
    nsub = tk // SUBLANES
    nacc = ACC_ROWS // SUBLANES
    buf_a = (s_a, cmax_a, p_a, al_a)
    buf_b = (s_b, cmax_b, p_b, al_b)
    ones_rows = jnp.ones((ACC_ROWS - V_HEAD_DIM, tk), BF16)

    def reset(stats):
        m_scr, acc_scr = stats
        m_scr[...] = jnp.full(m_scr.shape, NEG_BIG, F32)
        acc_scr[...] = jnp.zeros(acc_scr.shape, F32)

    def qk_stage(kv, q_rows, buf, diag=False):
        s_out, cmax_out = buf[0], buf[1]
        start = pl.multiple_of(kv * tk, tk)
        nt = (((1,), (1,)), ((), ()))
        colmax = lambda x: _sublane_allmax(jnp.max(x.reshape(x.shape[0] // SUBLANES, SUBLANES, x.shape[1]), axis=0))
        if not diag:
            st = lax.dot_general(k_ref[pl.ds(start, tk), :], q_ref[q_rows, :], nt,
                                 preferred_element_type=F32)
            s_out[...] = st
            cmax_out[...] = colmax(st)
            return
        hk = tk // 2
        q_lo = q_rows.start
        mask = (lax.broadcasted_iota(jnp.int32, (hk, tq), 0) <= lax.broadcasted_iota(jnp.int32, (hk, tq), 1))
        top = lax.dot_general(k_ref[pl.ds(start, hk), :], q_ref[q_rows, :], nt, preferred_element_type=F32)
        top = jnp.where(mask, top, NEG_BIG)
        bot = lax.dot_general(k_ref[pl.ds(start + hk, hk), :], q_ref[q_lo + hk:q_lo + tq, :], nt,
                              preferred_element_type=F32)
        bot = jnp.where(mask[:, :tq - hk], bot, NEG_BIG)
        s_out[0:hk, :] = top
        s_out[hk:tk, 0:hk] = jnp.full((tk - hk, hk), NEG_BIG, F32)
        s_out[hk:tk, hk:tq] = bot
        cm_top = colmax(top)
        cmax_out[:, 0:hk] = cm_top[:, 0:hk]
        cmax_out[:, hk:tq] = jnp.maximum(cm_top[:, hk:tq], colmax(bot))

    def softmax_stage(buf, stats):
        s_in, cmax_in, p_out, al_out = buf
        m_scr = stats[0]
        m_old = m_scr[...]
        m_new = jnp.maximum(m_old, cmax_in[...])
        al_out[...] = jnp.exp2(m_old - m_new)
        m_scr[...] = m_new
        for c in range(0, tq, MXU_DIM):
            m_c = m_new[:, c:c + MXU_DIM][None]
            for r in range(0, tk, EXP_ROWS):
                sc = s_in[r:r + EXP_ROWS, c:c + MXU_DIM].reshape(EXP_ROWS // SUBLANES, SUBLANES, MXU_DIM)
                p_out[r:r + EXP_ROWS, c:c + MXU_DIM] = (
                    jnp.exp2(sc - m_c).reshape(EXP_ROWS, MXU_DIM).astype(BF16))

    def pv_stage(kv, buf, stats):
        p_in, al_in = buf[2], buf[3]
        acc_scr = stats[1]
        start = pl.multiple_of(jnp.maximum(kv, 0) * tk, tk)
        vt_aug = jnp.concatenate([vt_ref[:, pl.ds(start, tk)], ones_rows], axis=0)
        pv = jnp.dot(vt_aug, p_in[...], preferred_element_type=F32)
        acc3 = acc_scr[...].reshape(nacc, SUBLANES, tq) * al_in[...][None]
        acc_scr[...] = acc3.reshape(ACC_ROWS, tq) + pv

    def finalize(q_rows, stats):
        acc_scr = stats[1]
        inv_l = 1.0 / acc_scr[V_HEAD_DIM:V_HEAD_DIM + SUBLANES, :]
        o_t = acc_scr[0:V_HEAD_DIM, :].reshape(V_HEAD_DIM // SUBLANES, SUBLANES, tq) * inv_l[None]
        o_ref[q_rows, :] = o_t.reshape(V_HEAD_DIM, tq).T.astype(o_ref.dtype)

    def make_step(qi, q_rows, stats):
        kv_of = lambda pos: jnp.where(pos == 0, qi, pos - 1)

        def step(pos, buf, other):
            qk_stage(kv_of(pos), q_rows, buf)
            pv_stage(kv_of(pos - 2), buf, stats)
            softmax_stage(other, stats)

        return kv_of, step

    def run_pairs(lo, hi, pair):
        def body(pp, carry):
            pair(pp)
            return carry

        lax.fori_loop(lo, hi, body, 0)

    bufs = (buf_a, buf_b)
    stats2 = ((m_a, acc_a), (m_b, acc_b))
    start = [0]
    for j in range(ntile - 1):
        last = start[j] if j % 2 == 0 else 1 - start[j]
        start.append(1 - last)

    def tile_ctx(j):
        qi = ntile * g + j
        rows = slice(j * tq, (j + 1) * tq)
        stats = stats2[j % 2]
        x, y = bufs[start[j]], bufs[1 - start[j]]
        kv_of, step = make_step(qi, rows, stats)
        return qi, rows, stats, x, y, kv_of, step

    def run_steps(j, first_pos, n_pairs):
        _, _, _, x, y, _, step = tile_ctx(j)
        even_buf, odd_buf = x, y

        def pair(pp):
            p0 = first_pos + 2 * pp
            if first_pos % 2 == 0:
                step(p0, even_buf, odd_buf)
                step(p0 + 1, odd_buf, even_buf)
            else:
                step(p0, odd_buf, even_buf)
                step(p0 + 1, even_buf, odd_buf)

        run_pairs(0, n_pairs, pair)

    qi0, rows0, st0, x0, y0, kv0, step0 = tile_ctx(0)
    reset(st0)
    qk_stage(qi0, rows0, x0, diag=True)

    @pl.when(g == 0)
    def _():
        y0[2][...] = jnp.zeros(y0[2].shape, BF16)
        y0[3][...] = jnp.ones(y0[3].shape, F32)

    @pl.when(g > 0)
    def _():
        qk_stage(kv0(1), rows0, y0)
        softmax_stage(x0, st0)
        step0(2, x0, y0)
        run_steps(0, 3, (ntile * g - 2) // 2)

    for j in range(ntile):
        qi, rows, stats, x, y, kv_of, step = tile_ctx(j)
        if j > 0:
            if j % 2 == 0:
                step(2, x, y)
                run_steps(j, 3, (ntile * g + j - 2) // 2)
            else:
                run_steps(j, 2, (ntile * g + j - 1) // 2)
        last_buf, other = (x, y) if j % 2 == 0 else (y, x)
        pv_stage(kv_of(qi - 1), other, stats)
        if j + 1 < ntile:
            qi_n, rows_n, st_n, x_n, y_n, kv_n, _ = tile_ctx(j + 1)
            assert x_n is other and y_n is last_buf
            reset(st_n)
            qk_stage(qi_n, rows_n, x_n, diag=True)
            softmax_stage(last_buf, stats)
            pv_stage(kv_of(qi), last_buf, stats)
            qk_stage(kv_n(1), rows_n, y_n)
            softmax_stage(x_n, st_n)
        else:
            softmax_stage(last_buf, stats)
            pv_stage(kv_of(qi), last_buf, stats)
        finalize(rows, stats)


def _attention(q, k, vt, *, tq, tk, ntile):
    s = q.shape[0]
    assert ntile % 2 == 0 and s % (ntile * tq) == 0, "sequence length must be a multiple of ntile query tiles"
    stat = pltpu.VMEM((SUBLANES, tq), F32)
    return pl.pallas_call(
        functools.partial(_attn_kernel, tq=tq, tk=tk, ntile=ntile),
        grid=(MLA_HEADS, s // (ntile * tq)),
        in_specs=[pl.BlockSpec((ntile * tq, QK_PAD), lambda h, i: (i, h)),
                  pl.BlockSpec((s, QK_PAD), lambda h, i: (0, h)),
                  pl.BlockSpec((V_HEAD_DIM, s), lambda h, i: (h, 0))],
        out_specs=pl.BlockSpec((ntile * tq, V_HEAD_DIM), lambda h, i: (i, h)),
        out_shape=jax.ShapeDtypeStruct((s, MLA_HEADS * V_HEAD_DIM), BF16),
        scratch_shapes=[stat, pltpu.VMEM((ACC_ROWS, tq), F32), stat, pltpu.VMEM((ACC_ROWS, tq), F32),
                        pltpu.VMEM((tk, tq), F32), pltpu.VMEM((tk, tq), F32), stat, stat,
                        pltpu.VMEM((tk, tq), BF16), pltpu.VMEM((tk, tq), BF16), stat, stat],
        compiler_params=pltpu.CompilerParams(dimension_semantics=("arbitrary", "arbitrary"),
                                             vmem_limit_bytes=VMEM_LIMIT),
        name="attention",
    )(q, k, vt)


def _ssd_kernel(xs_ref, bm_ref, cm_ref, dt_ref, gz_ref, alog_ref, dskip_ref, g_ref, e2_ref,
                o_ref, state, y_scr, *, rows, nheads):
    i = pl.program_id(0)
    gw = SSM_STATE
    hw = state.shape[2]

    @pl.when(i == 0)
    def _():
        state[...] = jnp.zeros(state.shape, F32)

    a_neg = -jnp.exp(alog_ref[...])
    r_i = lax.broadcasted_iota(jnp.int32, (CHUNK, CHUNK), 0)
    c_i = lax.broadcasted_iota(jnp.int32, (CHUNK, CHUNK), 1)
    tri = c_i <= r_i
    tri_bf = tri.astype(BF16)
    lane = lax.broadcasted_iota(jnp.int32, (CHUNK, LANES), 1)
    first_copy = lane < nheads
    head_lo = lane < SSM_HEAD_DIM
    e2 = e2_ref[...]

    def expand(v):
        hi = v.astype(BF16)
        lo = (v - hi.astype(F32)).astype(BF16)
        return jnp.dot(jnp.where(first_copy, hi, lo), e2, preferred_element_type=F32)

    for cidx in range(rows // CHUNK):
        sl = pl.ds(cidx * CHUNK, CHUNK)
        dt = dt_ref[sl, :]
        da = dt * (a_neg * LOG2E)
        a_cum = jnp.zeros((CHUNK, LANES), F32)
        for part in _split_bf16(da, 3):
            a_cum = a_cum + jnp.dot(tri_bf, part, preferred_element_type=F32)
        a_cum_t = a_cum.T
        a_last = a_cum[CHUNK - 1:CHUNK, :]
        ea = jnp.exp2(a_cum)
        dte = jnp.exp2(a_last - a_cum)

        xs = xs_ref[sl, :].astype(F32)
        xd = xs * expand(dt)
        ea_x = expand(ea)
        xdd = (xd * expand(dte)).astype(BF16)
        xd_bf = xd.astype(BF16)

        for g in range(SSM_GROUPS):
            bg = bm_ref[sl, g * gw:(g + 1) * gw]
            cg = cm_ref[sl, g * gw:(g + 1) * gw]
            cb = lax.dot_general(cg, bg, (((1,), (1,)), ((), ())), preferred_element_type=F32)
            prev = state[g]
            y_off = jnp.dot(cg, prev.astype(BF16), preferred_element_type=F32)
            new = lax.dot_general(bg, xdd[:, g * hw:(g + 1) * hw], (((0,), (0,)), ((), ())),
                                  preferred_element_type=F32)
            state[g] = prev * ea_x[CHUNK - 1:CHUNK, g * hw:(g + 1) * hw] + new
            y_scr[:, g * hw:(g + 1) * hw] = y_off * ea_x[:, g * hw:(g + 1) * hw]

            hpg = hw // SSM_HEAD_DIM
            for pair in range(hpg // 2):
                h0 = g * hpg + 2 * pair
                c0 = h0 * SSM_HEAD_DIM
                xp = xd_bf[:, c0:c0 + LANES]
                yp = jnp.zeros((CHUNK, LANES), F32)
                for k, keep in ((0, head_lo), (1, ~head_lo)):
                    h = h0 + k
                    seg = a_cum[:, h:h + 1] - a_cum_t[h:h + 1, :]
                    m_h = (cb * jnp.exp2(jnp.where(tri, seg, NEG_BIG))).astype(BF16)
                    yp = yp + jnp.dot(m_h, jnp.where(keep, xp, jnp.zeros_like(xp)),
                                      preferred_element_type=F32)
                y_scr[:, c0:c0 + LANES] = y_scr[:, c0:c0 + LANES] + yp

        y = y_scr[...] + xs * dskip_ref[...]
        hf = y * gz_ref[sl, :].astype(F32)
        for g in range(SSM_GROUPS):
            hg = hf[:, g * hw:(g + 1) * hw]
            ms = jnp.mean(hg * hg, axis=-1, keepdims=True)
            o_ref[sl, g * hw:(g + 1) * hw] = (hg * lax.rsqrt(ms + RMS_EPS)
                                               * g_ref[:, g * hw:(g + 1) * hw]).astype(o_ref.dtype)


def _ssd(xs, bm, cm, dt, gzs, alog, dskip_x, norm_g, e2, *, rows, nheads):
    s, nx = xs.shape
    hw = nx // SSM_GROUPS
    row = lambda a: pl.BlockSpec((rows, a.shape[1]), lambda i: (i, 0))
    full = lambda a: pl.BlockSpec(a.shape, lambda i: (0,) * a.ndim)
    return pl.pallas_call(
        functools.partial(_ssd_kernel, rows=rows, nheads=nheads),
        grid=(s // rows,),
        in_specs=[row(xs), row(bm), row(cm), row(dt), row(gzs),
                  full(alog), full(dskip_x), full(norm_g), full(e2)],
        out_specs=pl.BlockSpec((rows, nx), lambda i: (i, 0)),
        out_shape=jax.ShapeDtypeStruct((s, nx), BF16),
        scratch_shapes=[pltpu.VMEM((SSM_GROUPS, SSM_STATE, hw), F32),
                        pltpu.VMEM((CHUNK, nx), F32)],
        compiler_params=pltpu.CompilerParams(dimension_semantics=("arbitrary",),
                                             vmem_limit_bytes=VMEM_LIMIT),
        name="ssd",
    )(xs, bm, cm, dt, gzs, alog, dskip_x, norm_g, e2)


def _out_kernel(oa_ref, gza_ref, os_ref, x_ref, w32_ref, mod_ref, lng_ref, lnb_ref, o_ref, w_ref,
                *, alpha, sub):
    na = oa_ref.shape[1]

    @pl.when(pl.program_id(0) == 0)
    def _():
        for r in range(0, w_ref.shape[0], sub):
            w_ref[r:r + sub, :] = w32_ref[r:r + sub, :].astype(BF16)

    mixes = []
    for r in range(0, o_ref.shape[0], sub):
        rows = slice(r, r + sub)
        oa = (oa_ref[rows, :].astype(F32) * gza_ref[rows, :].astype(F32)).astype(BF16)
        mixed = jnp.dot(oa, w_ref[0:na, :], preferred_element_type=F32)
        mixes.append(mixed + jnp.dot(os_ref[rows, :], w_ref[na:, :], preferred_element_type=F32))
    for n, mixed in enumerate(mixes):
        rows = slice(n * sub, (n + 1) * sub)
        y = alpha * x_ref[rows, :] + mod_ref[2:3, :] * mixed
        mu = jnp.mean(y, axis=-1, keepdims=True)
        yc = y - mu
        var = jnp.mean(yc * yc, axis=-1, keepdims=True)
        o_ref[rows, :] = yc * lax.rsqrt(var + LN_EPS) * lng_ref[...] + lnb_ref[...]


def _out_proj(oa, gza, o_ssm, x2, w_out, mod3, ln_g, ln_b, *, tm, sub, alpha):
    s, d = x2.shape
    row = lambda a: pl.BlockSpec((tm, a.shape[1]), lambda i: (i, 0))
    full = lambda a: pl.BlockSpec(a.shape, lambda i: (0,) * a.ndim)
    return pl.pallas_call(
        functools.partial(_out_kernel, alpha=alpha, sub=sub),
        grid=(s // tm,),
        in_specs=[row(oa), row(gza), row(o_ssm), row(x2),
                  pl.BlockSpec(w_out.shape, lambda i: (0, 0), pipeline_mode=pl.Buffered(1)),
                  full(mod3), full(ln_g), full(ln_b)],
        out_specs=pl.BlockSpec((tm, d), lambda i: (i, 0)),
        out_shape=jax.ShapeDtypeStruct((s, d), F32),
        scratch_shapes=[pltpu.VMEM(w_out.shape, BF16)],
        compiler_params=pltpu.CompilerParams(dimension_semantics=("arbitrary",),
                                             vmem_limit_bytes=VMEM_LIMIT),
        name="out_proj",
    )(oa, gza, o_ssm, x2, w_out, mod3, ln_g, ln_b)


def _pad_cols(a, width):
    return jnp.pad(a, ((0, 0), (0, width - a.shape[1])))


def _regroup_kernel(wt_ref, o_ref, *, segments):
    wt = wt_ref[...]
    parts = [jnp.zeros((b, wt.shape[1]), wt.dtype) if a is None else wt[a:b, :] for a, b in segments]
    o_ref[...] = jnp.concatenate(parts, axis=0).T.astype(o_ref.dtype)


def _regroup_columns(w, segments, width):
    assert all(b % SUBLANES == 0 and (a or 0) % SUBLANES == 0 for a, b in segments)
    rows = w.shape[0]
    br = 256
    return pl.pallas_call(
        functools.partial(_regroup_kernel, segments=segments),
        grid=(rows // br,),
        in_specs=[pl.BlockSpec((w.shape[1], br), lambda i: (0, i))],
        out_specs=pl.BlockSpec((br, width), lambda i: (i, 0)),
        out_shape=jax.ShapeDtypeStruct((rows, width), BF16),
        compiler_params=pltpu.CompilerParams(vmem_limit_bytes=VMEM_LIMIT),
        name="regroup_w_in",
    )(w.T)


def _layer(x2, c, pos_row, w_ada, b_ada, w_in, q_norm_g, w_qb, kv_norm_g, w_kvb,
           conv_w, conv_b, dt_bias, a_log, d_skip, ssm_norm_g, w_out, ln_g, ln_b, *, depth):
    s, d = x2.shape
    q_rank = q_norm_g.shape[0]
    kv_rank = kv_norm_g.shape[0]
    nheads = dt_bias.shape[0]
    nch = conv_w.shape[1]
    nv = MLA_HEADS * V_HEAD_DIM
    nx = nheads * SSM_HEAD_DIM
    half = QK_ROPE_DIM // 2

    mod = _adaln_mod(c.reshape(d, 1), w_ada, b_ada.reshape(1, -1))
    mod3 = mod.reshape(3, d)

    o_q, o_ckv = 0, q_rank
    o_kr = o_ckv + kv_rank
    o_za = o_kr + QK_ROPE_DIM
    o_xbc = o_za + nv
    o_dt = o_xbc + nch
    o_zs = o_dt + nheads
    dt_pad = LANES - 2 * nheads
    groups = [("q", [(o_q, o_ckv)]), ("ckv", [(o_ckv, o_kr)]),
              ("krope", [(o_kr, o_za), (o_kr + half, o_za), (o_kr, o_kr + half)]),
              ("za", [(o_za, o_xbc)]), ("xbc", [(o_xbc, o_dt)]), ("zs", [(o_zs, o_zs + nx)]),
              ("dt", [(o_dt, o_zs), (o_dt, o_zs), (None, dt_pad)])]
    off, cur, segments = {}, 0, []
    for name, segs in groups:
        assert cur % LANES == 0
        off[name] = cur
        for a, b in segs:
            cur += b if a is None else b - a
            if segments and a is not None and segments[-1][0] is not None and segments[-1][1] == a:
                segments[-1] = (segments[-1][0], b)
            else:
                segments.append((a, b))
    assert cur % LANES == 0
    w_in_p = _regroup_columns(w_in, tuple(segments), cur)

    w3 = w_qb.reshape(q_rank, MLA_HEADS, QK_NOPE_DIM + QK_ROPE_DIM)
    w_rope = w3[:, :, QK_NOPE_DIM:]
    w_qb_p = jnp.concatenate([w3, w_rope[:, :, half:], w_rope[:, :, :half]], axis=2)
    w_qb_p = w_qb_p.reshape(q_rank, MLA_HEADS * QK_PAD).astype(BF16)

    inv_freq = 1.0 / (ROPE_THETA ** (jnp.arange(half, dtype=F32) / half))
    rope_tab = jnp.broadcast_to(inv_freq[:, None], (half, IN_SUB))

    dtb = _pad_cols(jnp.concatenate([dt_bias, dt_bias]).reshape(1, -1), LANES)
    q_scale = (QK_NOPE_DIM + QK_ROPE_DIM) ** -0.5 * LOG2E

    wkv3 = w_kvb.reshape(kv_rank, MLA_HEADS, QK_NOPE_DIM + V_HEAD_DIM)
    w_k = wkv3[:, :, :QK_NOPE_DIM].reshape(kv_rank, MLA_HEADS * QK_NOPE_DIM).astype(BF16)
    w_vt = wkv3[:, :, QK_NOPE_DIM:].reshape(kv_rank, nv).T.astype(BF16)

    q, k, vt, gza, xs, bm, cm, dt, gzs = _in_proj(
        x2, mod3, pos_row, rope_tab, w_in_p, q_norm_g.reshape(1, -1), w_qb_p,
        kv_norm_g.reshape(1, -1), w_k, w_vt, conv_w, conv_b.reshape(1, -1), dtb,
        off=off, tm=IN_ROWS, sub=IN_SUB, q_scale=q_scale)

    o_attn = _attention(q, k, vt, tq=ATTN_TILE, tk=ATTN_TILE, ntile=ATTN_TILES_PER_STEP)

    alog = _pad_cols(jnp.concatenate([a_log, a_log]).reshape(1, -1), LANES)
    e_head = jnp.repeat(jnp.eye(nheads, dtype=BF16), SSM_HEAD_DIM, axis=1)
    e2 = jnp.pad(jnp.concatenate([e_head, e_head], axis=0), ((0, LANES - 2 * nheads), (0, 0)))
    dskip_x = jnp.repeat(d_skip, SSM_HEAD_DIM).reshape(1, -1)
    o_ssm = _ssd(xs, bm, cm, dt, gzs, alog, dskip_x, ssm_norm_g.reshape(1, -1), e2,
                 rows=SSD_ROWS, nheads=nheads)

    alpha = (2.0 * depth) ** 0.25
    return _out_proj(o_attn, gza, o_ssm, x2, w_out, mod3,
                     ln_g.reshape(1, -1), ln_b.reshape(1, -1), tm=OUT_ROWS, sub=OUT_SUB, alpha=alpha)


def kernel(x, c, positions, w_ada, b_ada, w_in, q_norm_g, w_qb, kv_norm_g, w_kvb, conv_w, conv_b,
           dt_bias, a_log, d_skip, ssm_norm_g, w_out, ln_g, ln_b):
    b, s, d = x.shape
    depth = w_in.shape[0]
    assert b == 1, "one sequence per call"
    h = x.reshape(s, d)
    pos_row = positions.reshape(1, s)
    for l in range(depth):
        h = _layer(h, c, pos_row, w_ada[l], b_ada[l], w_in[l], q_norm_g[l], w_qb[l], kv_norm_g[l],
                   w_kvb[l], conv_w[l], conv_b[l], dt_bias[l], a_log[l], d_skip[l], ssm_norm_g[l],
                   w_out[l], ln_g[l], ln_b[l], depth=depth)
    return h.reshape(b, s, d)
```

```python
import functools
import math

import jax
import jax.numpy as jnp
from jax import lax
from jax.experimental import pallas as pl
from jax.experimental.pallas import tpu as pltpu

F32 = jnp.float32
BF16 = jnp.bfloat16

MLA_HEADS = 8
QK_NOPE_DIM = 128
QK_ROPE_DIM = 64
V_HEAD_DIM = 128
ROPE_THETA = 10000.0
SSM_HEAD_DIM = 64
SSM_GROUPS = 2
SSM_STATE = 128
CHUNK = 128
RMS_EPS = 1e-6
LN_EPS = 1e-5

LANES = 128
SUBLANES = 8
MXU_DIM = 256
QK_PAD = MXU_DIM
VMEM_LIMIT = 56 * 1024 * 1024

IN_ROWS = 512
IN_SUB = 256
OUT_SUB = 256
ATTN_TILE = 1024
ATTN_TILES_PER_STEP = 2
SSD_ROWS = 512

LOG2E = 1.4426950408889634
NEG_BIG = -1e30


def _silu(z):
    h = 0.5 * z
    return h * jnp.tanh(h) + h


def _softplus(z):
    return jnp.maximum(z, 0.0) + jnp.log1p(jnp.exp(-jnp.abs(z)))


def _split_bf16(x, parts):
    out, rem = [], x
    for _ in range(parts):
        hi = rem.astype(BF16)
        out.append(hi)
        rem = rem - hi.astype(F32)
    return out


def _mod_kernel(c_ref, w_ref, b_ref, o_ref):
    o_ref[...] = jnp.sum(w_ref[...] * c_ref[...], axis=0, keepdims=True) + b_ref[...]


def _adaln_mod(c_col, w_ada, b_ada):
    d, n = w_ada.shape
    bn = 1024
    return pl.pallas_call(
        _mod_kernel,
        grid=(n // bn,),
        in_specs=[pl.BlockSpec((d, 1), lambda j: (0, 0)),
                  pl.BlockSpec((d, bn), lambda j: (0, j)),
                  pl.BlockSpec((1, bn), lambda j: (0, j))],
        out_specs=pl.BlockSpec((1, bn), lambda j: (0, j)),
        out_shape=jax.ShapeDtypeStruct((1, n), F32),
        name="adaln_mod",
    )(c_col, w_ada, b_ada)


def _in_kernel(x_ref, mod_ref, pos_ref, rope_ref, w_in_ref, qg_ref, w_qb_ref, kvg_ref, w_k_ref, w_vt_ref,
               conv_w_ref, conv_b_ref, dtb_ref,
               q_out, k_out, vt_out, gza_out, xs_out, bm_out, cm_out, dt_out, gzs_out,
               xbuf, *, tm, sub, off, q_scale):
    i = pl.program_id(0)
    shift = mod_ref[0:1, :]
    scale1 = 1.0 + mod_ref[1:2, :]
    lane = lax.broadcasted_iota(jnp.int32, (sub, LANES), 1)
    low_half = lane < QK_ROPE_DIM
    nconv, nch = conv_w_ref.shape
    nx = xs_out.shape[1]
    nb = bm_out.shape[1]

    @pl.when(i == 0)
    def _():
        xbuf[0:SUBLANES, :] = jnp.zeros((SUBLANES, nch), F32)

    css = []
    for h in range(tm // sub):
        ang_t = pos_ref[:, h * sub:(h + 1) * sub].astype(F32) * rope_ref[...]
        cos_t, sin_t = jnp.cos(ang_t), jnp.sin(ang_t)
        css.append(jnp.concatenate([cos_t, cos_t, -sin_t, sin_t], axis=0).T)

    projs = []
    for h in range(tm // sub):
        u = x_ref[h * sub:(h + 1) * sub, :] * scale1 + shift
        projs.append(jnp.dot(u.astype(BF16), w_in_ref[...], preferred_element_type=F32))

    for h, (proj, cs) in enumerate(zip(projs, css)):
        rows = slice(h * sub, (h + 1) * sub)

        def rope(rr, cs=cs):
            t = rr * cs
            return jnp.where(low_half, t + pltpu.roll(t, QK_ROPE_DIM, axis=1), 0.0)

        q_lat = proj[:, off["q"]:off["q"] + qg_ref.shape[1]]
        qn = q_lat * lax.rsqrt(jnp.mean(q_lat * q_lat, axis=-1, keepdims=True) + RMS_EPS) * qg_ref[...]
        qf = jnp.dot(qn.astype(BF16), w_qb_ref[...], preferred_element_type=F32)
        for hd in range(MLA_HEADS):
            c0 = hd * QK_PAD
            q_out[rows, c0:c0 + QK_NOPE_DIM] = (qf[:, c0:c0 + QK_NOPE_DIM] * q_scale).astype(BF16)
            q_out[rows, c0 + QK_NOPE_DIM:c0 + QK_PAD] = (
                rope(qf[:, c0 + QK_NOPE_DIM:c0 + QK_PAD]) * q_scale).astype(BF16)

        c_kv = proj[:, off["ckv"]:off["ckv"] + kvg_ref.shape[1]]
        ckvn = c_kv * lax.rsqrt(jnp.mean(c_kv * c_kv, axis=-1, keepdims=True) + RMS_EPS) * kvg_ref[...]
        ckvn_bf = ckvn.astype(BF16)
        kf = jnp.dot(ckvn_bf, w_k_ref[...], preferred_element_type=F32)
        vt_out[:, rows] = lax.dot_general(w_vt_ref[...], ckvn_bf, (((1,), (1,)), ((), ())),
                                          preferred_element_type=F32).astype(BF16)
        k_rope = rope(proj[:, off["krope"]:off["krope"] + LANES]).astype(BF16)
        for hd in range(MLA_HEADS):
            k_out[rows, hd * QK_PAD:hd * QK_PAD + QK_NOPE_DIM] = (
                kf[:, hd * QK_NOPE_DIM:(hd + 1) * QK_NOPE_DIM].astype(BF16))
            k_out[rows, hd * QK_PAD + QK_NOPE_DIM:(hd + 1) * QK_PAD] = k_rope

        gza_out[rows, :] = _silu(proj[:, off["za"]:off["za"] + gza_out.shape[1]]).astype(BF16)
        gzs_out[rows, :] = _silu(proj[:, off["zs"]:off["zs"] + gzs_out.shape[1]]).astype(BF16)
        dt_out[rows, :] = _softplus(proj[:, off["dt"]:off["dt"] + LANES] + dtb_ref[...])

        r1 = SUBLANES + h * sub
        xbuf[r1:r1 + sub, :] = proj[:, off["xbc"]:off["xbc"] + nch]
        xfull = xbuf[r1 - SUBLANES:r1 + sub, :]
        acc = conv_b_ref[...] + conv_w_ref[nconv - 1:nconv, :] * xfull[SUBLANES:, :]
        for k in range(1, nconv):
            shifted = pltpu.roll(xfull, k, axis=0)[SUBLANES:, :]
            acc = acc + conv_w_ref[nconv - 1 - k:nconv - k, :] * shifted
        xc = _silu(acc)
        xs_out[rows, :] = xc[:, :nx].astype(BF16)
        bm_out[rows, :] = xc[:, nx:nx + nb].astype(BF16)
        cm_out[rows, :] = xc[:, nx + nb:nx + 2 * nb].astype(BF16)

    xbuf[0:SUBLANES, :] = xbuf[tm:tm + SUBLANES, :]


def _in_proj(x2, mod3, pos_row, rope_tab, w_in_p, qg, w_qb_p, kvg, w_k, w_vt, conv_w, conv_b, dtb,
             *, off, tm, sub, q_scale):
    s, d = x2.shape
    nq = MLA_HEADS * QK_PAD
    nv = MLA_HEADS * V_HEAD_DIM
    nch = conv_w.shape[1]
    nbc = SSM_GROUPS * SSM_STATE
    nx = nch - 2 * nbc
    row = lambda w: pl.BlockSpec((tm, w), lambda i: (i, 0))
    full = lambda a: pl.BlockSpec(a.shape, lambda i: (0,) * a.ndim, pipeline_mode=pl.Buffered(1))
    outs = [
        jax.ShapeDtypeStruct((s, nq), BF16),
        jax.ShapeDtypeStruct((s, nq), BF16),
        jax.ShapeDtypeStruct((nv, s), BF16),
        jax.ShapeDtypeStruct((s, nv), BF16),
        jax.ShapeDtypeStruct((s, nx), BF16),
        jax.ShapeDtypeStruct((s, nbc), BF16),
        jax.ShapeDtypeStruct((s, nbc), BF16),
        jax.ShapeDtypeStruct((s, LANES), F32),
        jax.ShapeDtypeStruct((s, nx), BF16),
    ]
    return pl.pallas_call(
        functools.partial(_in_kernel, tm=tm, sub=sub, off=off, q_scale=q_scale),
        grid=(s // tm,),
        in_specs=[row(d), full(mod3), pl.BlockSpec((1, tm), lambda i: (0, i)), full(rope_tab),
                  full(w_in_p), full(qg), full(w_qb_p), full(kvg), full(w_k), full(w_vt),
                  full(conv_w), full(conv_b), full(dtb)],
        out_specs=[pl.BlockSpec((nv, tm), lambda i: (0, i)) if n == 2 else row(o.shape[1])
                   for n, o in enumerate(outs)],
        out_shape=outs,
        scratch_shapes=[pltpu.VMEM((tm + 2 * SUBLANES, nch), F32)],
        compiler_params=pltpu.CompilerParams(dimension_semantics=("arbitrary",),
                                             vmem_limit_bytes=VMEM_LIMIT),
        name="in_proj",
    )(x2, mod3, pos_row, rope_tab, w_in_p, qg, w_qb_p, kvg, w_k, w_vt, conv_w, conv_b, dtb)


def _sublane_allmax(x):
    shift = SUBLANES // 2
    while shift:
        x = jnp.maximum(x, pltpu.roll(x, shift, axis=0))
        shift //= 2
    return x


def _sublane_allsum(x):
    shift = SUBLANES // 2
    while shift:
        x = x + pltpu.roll(x, shift, axis=0)
        shift //= 2
    return x


BF16_ROWS = 16
ACC_ROWS = V_HEAD_DIM + BF16_ROWS
EXP_ROWS = 32


def _attn_kernel(q_ref, k_ref, vt_ref, o_ref, m_a, acc_a, m_b, acc_b,
                 s_a, s_b, cmax_a, cmax_b, p_a, p_b, al_a, al_b, *, tq, tk, ntile):
    assert tq == tk
    g = pl.program_id(1)
    nsub = tk // SUBLANES
    nacc = ACC_ROWS // SUBLANES
    buf_a = (s_a, cmax_a, p_a, al_a)
    buf_b = (s_b, cmax_b, p_b, al_b)
    ones_rows = jnp.ones((ACC_ROWS - V_HEAD_DIM, tk), BF16)

    def reset(stats):
        m_scr, acc_scr = stats
        m_scr[...] = jnp.full(m_scr.shape, NEG_BIG, F32)
        acc_scr[...] = jnp.zeros(acc_scr.shape, F32)

    def qk_stage(kv, q_rows, buf, diag=False):
        s_out, cmax_out = buf[0], buf[1]
        start = pl.multiple_of(kv * tk, tk)
        nt = (((1,), (1,)), ((), ()))
        colmax = lambda x: _sublane_allmax(jnp.max(x.reshape(x.shape[0] // SUBLANES, SUBLANES, x.shape[1]), axis=0))
        if not diag:
            st = lax.dot_general(k_ref[pl.ds(start, tk), :], q_ref[q_rows, :], nt,
                                 preferred_element_type=F32)
            s_out[...] = st
            cmax_out[...] = colmax(st)
            return
        hk = tk // 2
        q_lo = q_rows.start
        mask = (lax.broadcasted_iota(jnp.int32, (hk, tq), 0) <= lax.broadcasted_iota(jnp.int32, (hk, tq), 1))
        top = lax.dot_general(k_ref[pl.ds(start, hk), :], q_ref[q_rows, :], nt, preferred_element_type=F32)
        top = jnp.where(mask, top, NEG_BIG)
        bot = lax.dot_general(k_ref[pl.ds(start + hk, hk), :], q_ref[q_lo + hk:q_lo + tq, :], nt,
                              preferred_element_type=F32)
        bot = jnp.where(mask[:, :tq - hk], bot, NEG_BIG)
        s_out[0:hk, :] = top
        s_out[hk:tk, 0:hk] = jnp.full((tk - hk, hk), NEG_BIG, F32)
        s_out[hk:tk, hk:tq] = bot
        cm_top = colmax(top)
        cmax_out[:, 0:hk] = cm_top[:, 0:hk]
        cmax_out[:, hk:tq] = jnp.maximum(cm_top[:, hk:tq], colmax(bot))

    def softmax_stage(buf, stats):
        s_in, cmax_in, p_out, al_out = buf
        m_scr = stats[0]
        m_old = m_scr[...]
        m_new = jnp.maximum(m_old, cmax_in[...])
        al_out[...] = jnp.exp2(m_old - m_new)
        m_scr[...] = m_new
        for c in range(0, tq, MXU_DIM):
            m_c = m_new[:, c:c + MXU_DIM][None]
            for r in range(0, tk, EXP_ROWS):
                sc = s_in[r:r + EXP_ROWS, c:c + MXU_DIM].reshape(EXP_ROWS // SUBLANES, SUBLANES, MXU_DIM)
                p_out[r:r + EXP_ROWS, c:c + MXU_DIM] = (
                    jnp.exp2(sc - m_c).reshape(EXP_ROWS, MXU_DIM).astype(BF16))

    def pv_stage(kv, buf, stats):
        p_in, al_in = buf[2], buf[3]
        acc_scr = stats[1]
        start = pl.multiple_of(jnp.maximum(kv, 0) * tk, tk)
        vt_aug = jnp.concatenate([vt_ref[:, pl.ds(start, tk)], ones_rows], axis=0)
        pv = jnp.dot(vt_aug, p_in[...], preferred_element_type=F32)
        acc3 = acc_scr[...].reshape(nacc, SUBLANES, tq) * al_in[...][None]
        acc_scr[...] = acc3.reshape(ACC_ROWS, tq) + pv

    def finalize(q_rows, stats):
        acc_scr = stats[1]
        inv_l = 1.0 / acc_scr[V_HEAD_DIM:V_HEAD_DIM + SUBLANES, :]
        o_t = acc_scr[0:V_HEAD_DIM, :].reshape(V_HEAD_DIM // SUBLANES, SUBLANES, tq) * inv_l[None]
        o_ref[q_rows, :] = o_t.reshape(V_HEAD_DIM, tq).T.astype(o_ref.dtype)

    def make_step(qi, q_rows, stats):
        kv_of = lambda pos: jnp.where(pos == 0, qi, pos - 1)

        def step(pos, buf, other):
            qk_stage(kv_of(pos), q_rows, buf)
            pv_stage(kv_of(pos - 2), buf, stats)
            softmax_stage(other, stats)

        return kv_of, step

    def run_pairs(lo, hi, pair):
        def body(pp, carry):
            pair(pp)
            return carry

        lax.fori_loop(lo, hi, body, 0)

    bufs = (buf_a, buf_b)
    stats2 = ((m_a, acc_a), (m_b, acc_b))
    start = [0]
    for j in range(ntile - 1):
        last = start[j] if j % 2 == 0 else 1 - start[j]
        start.append(1 - last)

    def tile_ctx(j):
        qi = ntile * g + j
        rows = slice(j * tq, (j + 1) * tq)
        stats = stats2[j % 2]
        x, y = bufs[start[j]], bufs[1 - start[j]]
        kv_of, step = make_step(qi, rows, stats)
        return qi, rows, stats, x, y, kv_of, step

    def run_steps(j, first_pos, n_pairs):
        _, _, _, x, y, _, step = tile_ctx(j)
        even_buf, odd_buf = x, y

        def pair(pp):
            p0 = first_pos + 2 * pp
            if first_pos % 2 == 0:
                step(p0, even_buf, odd_buf)
                step(p0 + 1, odd_buf, even_buf)
            else:
                step(p0, odd_buf, even_buf)
                step(p0 + 1, even_buf, odd_buf)

        run_pairs(0, n_pairs, pair)

    qi0, rows0, st0, x0, y0, kv0, step0 = tile_ctx(0)
    reset(st0)
    qk_stage(qi0, rows0, x0, diag=True)

    @pl.when(g == 0)
    def _():
        y0[2][...] = jnp.zeros(y0[2].shape, BF16)
        y0[3][...] = jnp.ones(y0[3].shape, F32)

    @pl.when(g > 0)
    def _():
        qk_stage(kv0(1), rows0, y0)
        softmax_stage(x0, st0)
        step0(2, x0, y0)
        run_steps(0, 3, (ntile * g - 2) // 2)

    for j in range(ntile):
        qi, rows, stats, x, y, kv_of, step = tile_ctx(j)
        if j > 0:
            if j % 2 == 0:
                step(2, x, y)
                run_steps(j, 3, (ntile * g + j - 2) // 2)
            else:
                run_steps(j, 2, (ntile * g + j - 1) // 2)
        last_buf, other = (x, y) if j % 2 == 0 else (y, x)
        pv_stage(kv_of(qi - 1), other, stats)
        if j + 1 < ntile:
            qi_n, rows_n, st_n, x_n, y_n, kv_n, _ = tile_ctx(j + 1)
            assert x_n is other and y_n is last_buf
            reset(st_n)
            qk_stage(qi_n, rows_n, x_n, diag=True)
            softmax_stage(last_buf, stats)
            pv_stage(kv_of(qi), last_buf, stats)
            qk_stage(kv_n(1), rows_n, y_n)
            softmax_stage(x_n, st_n)
        else:
            softmax_stage(last_buf, stats)
            pv_stage(kv_of(qi), last_buf, stats)
        finalize(rows, stats)


def _attention(q, k, vt, *, tq, tk, ntile):
    s = q.shape[0]
    assert ntile % 2 == 0 and s % (ntile * tq) == 0, "sequence length must be a multiple of ntile query tiles"
    stat = pltpu.VMEM((SUBLANES, tq), F32)
    return pl.pallas_call(
        functools.partial(_attn_kernel, tq=tq, tk=tk, ntile=ntile),
        grid=(MLA_HEADS, s // (ntile * tq)),
        in_specs=[pl.BlockSpec((ntile * tq, QK_PAD), lambda h, i: (i, h)),
                  pl.BlockSpec((s, QK_PAD), lambda h, i: (0, h)),
                  pl.BlockSpec((V_HEAD_DIM, s), lambda h, i: (h, 0))],
        out_specs=pl.BlockSpec((ntile * tq, V_HEAD_DIM), lambda h, i: (i, h)),
        out_shape=jax.ShapeDtypeStruct((s, MLA_HEADS * V_HEAD_DIM), BF16),
        scratch_shapes=[stat, pltpu.VMEM((ACC_ROWS, tq), F32), stat, pltpu.VMEM((ACC_ROWS, tq), F32),
                        pltpu.VMEM((tk, tq), F32), pltpu.VMEM((tk, tq), F32), stat, stat,
                        pltpu.VMEM((tk, tq), BF16), pltpu.VMEM((tk, tq), BF16), stat, stat],
        compiler_params=pltpu.CompilerParams(dimension_semantics=("arbitrary", "arbitrary"),
                                             vmem_limit_bytes=VMEM_LIMIT),
        name="attention",
    )(q, k, vt)


def _ssd_out_kernel(xs_ref, bm_ref, cm_ref, dt_ref, gz_ref, alog_ref, dskip_ref, g_ref, e2_ref,
                    oa_ref, gza_ref, x_ref, w32_ref, mod_ref, lng_ref, lnb_ref,
                    o_ref, state, y_scr, w_ref, os_scr, *, rows, nheads, alpha, sub):
    i = pl.program_id(0)
    gw = SSM_STATE
    hw = state.shape[2]
    na = oa_ref.shape[1]

    @pl.when(i == 0)
    def _():
        state[...] = jnp.zeros(state.shape, F32)
        for r in range(0, w_ref.shape[0], sub):
            w_ref[r:r + sub, :] = w32_ref[r:r + sub, :].astype(BF16)

    mixes = []
    for r in range(0, rows, sub):
        oa = (oa_ref[r:r + sub, :].astype(F32) * gza_ref[r:r + sub, :].astype(F32)).astype(BF16)
        mixes.append(jnp.dot(oa, w_ref[0:na, :], preferred_element_type=F32))

    a_neg = -jnp.exp(alog_ref[...])
    r_i = lax.broadcasted_iota(jnp.int32, (CHUNK, CHUNK), 0)
    c_i = lax.broadcasted_iota(jnp.int32, (CHUNK, CHUNK), 1)
    tri = c_i <= r_i
    tri_bf = tri.astype(BF16)
    lane = lax.broadcasted_iota(jnp.int32, (CHUNK, LANES), 1)
    first_copy = lane < nheads
    head_lo = lane < SSM_HEAD_DIM
    e2 = e2_ref[...]

    def expand(v):
        hi = v.astype(BF16)
        lo = (v - hi.astype(F32)).astype(BF16)
        return jnp.dot(jnp.where(first_copy, hi, lo), e2, preferred_element_type=F32)

    for cidx in range(rows // CHUNK):
        sl = pl.ds(cidx * CHUNK, CHUNK)
        dt = dt_ref[sl, :]
        da = dt * (a_neg * LOG2E)
        a_cum = jnp.zeros((CHUNK, LANES), F32)
        for part in _split_bf16(da, 3):
            a_cum = a_cum + jnp.dot(tri_bf, part, preferred_element_type=F32)
        a_cum_t = a_cum.T
        a_last = a_cum[CHUNK - 1:CHUNK, :]
        ea = jnp.exp2(a_cum)
        dte = jnp.exp2(a_last - a_cum)

        xs = xs_ref[sl, :].astype(F32)
        xd = xs * expand(dt)
        ea_x = expand(ea)
        xdd = (xd * expand(dte)).astype(BF16)
        xd_bf = xd.astype(BF16)

        for g in range(SSM_GROUPS):
            bg = bm_ref[sl, g * gw:(g + 1) * gw]
            cg = cm_ref[sl, g * gw:(g + 1) * gw]
            cb = lax.dot_general(cg, bg, (((1,), (1,)), ((), ())), preferred_element_type=F32)
            prev = state[g]
            y_off = jnp.dot(cg, prev.astype(BF16), preferred_element_type=F32)
            new = lax.dot_general(bg, xdd[:, g * hw:(g + 1) * hw], (((0,), (0,)), ((), ())),
                                  preferred_element_type=F32)
            state[g] = prev * ea_x[CHUNK - 1:CHUNK, g * hw:(g + 1) * hw] + new
            y_scr[:, g * hw:(g + 1) * hw] = y_off * ea_x[:, g * hw:(g + 1) * hw]

            hpg = hw // SSM_HEAD_DIM
            for pair in range(hpg // 2):
                h0 = g * hpg + 2 * pair
                c0 = h0 * SSM_HEAD_DIM
                xp = xd_bf[:, c0:c0 + LANES]
                yp = jnp.zeros((CHUNK, LANES), F32)
                for k, keep in ((0, head_lo), (1, ~head_lo)):
                    h = h0 + k
                    seg = a_cum[:, h:h + 1] - a_cum_t[h:h + 1, :]
                    m_h = (cb * jnp.exp2(jnp.where(tri, seg, NEG_BIG))).astype(BF16)
                    yp = yp + jnp.dot(m_h, jnp.where(keep, xp, jnp.zeros_like(xp)),
                                      preferred_element_type=F32)
                y_scr[:, c0:c0 + LANES] = y_scr[:, c0:c0 + LANES] + yp

        y = y_scr[...] + xs * dskip_ref[...]
        hf = y * gz_ref[sl, :].astype(F32)
        for g in range(SSM_GROUPS):
            hg = hf[:, g * hw:(g + 1) * hw]
            ms = jnp.mean(hg * hg, axis=-1, keepdims=True)
            os_scr[sl, g * hw:(g + 1) * hw] = (hg * lax.rsqrt(ms + RMS_EPS)
                                                * g_ref[:, g * hw:(g + 1) * hw]).astype(os_scr.dtype)

    for n, mixed in enumerate(mixes):
        rws = slice(n * sub, (n + 1) * sub)
        mixed = mixed + jnp.dot(os_scr[rws, :], w_ref[na:, :], preferred_element_type=F32)
        y = alpha * x_ref[rws, :] + mod_ref[2:3, :] * mixed
        mu = jnp.mean(y, axis=-1, keepdims=True)
        yc = y - mu
        var = jnp.mean(yc * yc, axis=-1, keepdims=True)
        o_ref[rws, :] = yc * lax.rsqrt(var + LN_EPS) * lng_ref[...] + lnb_ref[...]


def _ssd_out(xs, bm, cm, dt, gzs, alog, dskip_x, norm_g, e2, oa, gza, x2, w_out, mod3, ln_g, ln_b,
             *, rows, sub, nheads, alpha):
    s, nx = xs.shape
    d = x2.shape[1]
    hw = nx // SSM_GROUPS
    row = lambda a: pl.BlockSpec((rows, a.shape[1]), lambda i: (i, 0))
    full = lambda a: pl.BlockSpec(a.shape, lambda i: (0,) * a.ndim)
    return pl.pallas_call(
        functools.partial(_ssd_out_kernel, rows=rows, nheads=nheads, alpha=alpha, sub=sub),
        grid=(s // rows,),
        in_specs=[row(xs), row(bm), row(cm), row(dt), row(gzs),
                  full(alog), full(dskip_x), full(norm_g), full(e2),
                  row(oa), row(gza), row(x2),
                  pl.BlockSpec(w_out.shape, lambda i: (0, 0), pipeline_mode=pl.Buffered(1)),
                  full(mod3), full(ln_g), full(ln_b)],
        out_specs=pl.BlockSpec((rows, d), lambda i: (i, 0)),
        out_shape=jax.ShapeDtypeStruct((s, d), F32),
        scratch_shapes=[pltpu.VMEM((SSM_GROUPS, SSM_STATE, hw), F32),
                        pltpu.VMEM((CHUNK, nx), F32),
                        pltpu.VMEM(w_out.shape, BF16),
                        pltpu.VMEM((rows, nx), BF16)],
        compiler_params=pltpu.CompilerParams(dimension_semantics=("arbitrary",),
                                             vmem_limit_bytes=VMEM_LIMIT),
        name="ssd_out",
    )(xs, bm, cm, dt, gzs, alog, dskip_x, norm_g, e2, oa, gza, x2, w_out, mod3, ln_g, ln_b)


def _pad_cols(a, width):
    return jnp.pad(a, ((0, 0), (0, width - a.shape[1])))


def _regroup_kernel(wt_ref, o_ref, *, segments):
    wt = wt_ref[...]
    parts = [jnp.zeros((b, wt.shape[1]), wt.dtype) if a is None else wt[a:b, :] for a, b in segments]
    o_ref[...] = jnp.concatenate(parts, axis=0).T.astype(o_ref.dtype)


def _regroup_columns(w, segments, width):
    assert all(b % SUBLANES == 0 and (a or 0) % SUBLANES == 0 for a, b in segments)
    rows = w.shape[0]
    br = 256
    return pl.pallas_call(
        functools.partial(_regroup_kernel, segments=segments),
        grid=(rows // br,),
        in_specs=[pl.BlockSpec((w.shape[1], br), lambda i: (0, i))],
        out_specs=pl.BlockSpec((br, width), lambda i: (i, 0)),
        out_shape=jax.ShapeDtypeStruct((rows, width), BF16),
        compiler_params=pltpu.CompilerParams(vmem_limit_bytes=VMEM_LIMIT),
        name="regroup_w_in",
    )(w.T)


def _layer(x2, c, pos_row, w_ada, b_ada, w_in, q_norm_g, w_qb, kv_norm_g, w_kvb,
           conv_w, conv_b, dt_bias, a_log, d_skip, ssm_norm_g, w_out, ln_g, ln_b, *, depth):
    s, d = x2.shape
    q_rank = q_norm_g.shape[0]
    kv_rank = kv_norm_g.shape[0]
    nheads = dt_bias.shape[0]
    nch = conv_w.shape[1]
    nv = MLA_HEADS * V_HEAD_DIM
    nx = nheads * SSM_HEAD_DIM
    half = QK_ROPE_DIM // 2

    mod = _adaln_mod(c.reshape(d, 1), w_ada, b_ada.reshape(1, -1))
    mod3 = mod.reshape(3, d)

    o_q, o_ckv = 0, q_rank
    o_kr = o_ckv + kv_rank
    o_za = o_kr + QK_ROPE_DIM
    o_xbc = o_za + nv
    o_dt = o_xbc + nch
    o_zs = o_dt + nheads
    dt_pad = LANES - 2 * nheads
    groups = [("q", [(o_q, o_ckv)]), ("ckv", [(o_ckv, o_kr)]),
              ("krope", [(o_kr, o_za), (o_kr + half, o_za), (o_kr, o_kr + half)]),
              ("za", [(o_za, o_xbc)]), ("xbc", [(o_xbc, o_dt)]), ("zs", [(o_zs, o_zs + nx)]),
              ("dt", [(o_dt, o_zs), (o_dt, o_zs), (None, dt_pad)])]
    off, cur, segments = {}, 0, []
    for name, segs in groups:
        assert cur % LANES == 0
        off[name] = cur
        for a, b in segs:
            cur += b if a is None else b - a
            if segments and a is not None and segments[-1][0] is not None and segments[-1][1] == a:
                segments[-1] = (segments[-1][0], b)
            else:
                segments.append((a, b))
    assert cur % LANES == 0
    w_in_p = _regroup_columns(w_in, tuple(segments), cur)

    w3 = w_qb.reshape(q_rank, MLA_HEADS, QK_NOPE_DIM + QK_ROPE_DIM)
    w_rope = w3[:, :, QK_NOPE_DIM:]
    w_qb_p = jnp.concatenate([w3, w_rope[:, :, half:], w_rope[:, :, :half]], axis=2)
    w_qb_p = w_qb_p.reshape(q_rank, MLA_HEADS * QK_PAD).astype(BF16)

    inv_freq = 1.0 / (ROPE_THETA ** (jnp.arange(half, dtype=F32) / half))
    rope_tab = jnp.broadcast_to(inv_freq[:, None], (half, IN_SUB))

    dtb = _pad_cols(jnp.concatenate([dt_bias, dt_bias]).reshape(1, -1), LANES)
    q_scale = (QK_NOPE_DIM + QK_ROPE_DIM) ** -0.5 * LOG2E

    wkv3 = w_kvb.reshape(kv_rank, MLA_HEADS, QK_NOPE_DIM + V_HEAD_DIM)
    w_k = wkv3[:, :, :QK_NOPE_DIM].reshape(kv_rank, MLA_HEADS * QK_NOPE_DIM).astype(BF16)
    w_vt = wkv3[:, :, QK_NOPE_DIM:].reshape(kv_rank, nv).T.astype(BF16)

    q, k, vt, gza, xs, bm, cm, dt, gzs = _in_proj(
        x2, mod3, pos_row, rope_tab, w_in_p, q_norm_g.reshape(1, -1), w_qb_p,
        kv_norm_g.reshape(1, -1), w_k, w_vt, conv_w, conv_b.reshape(1, -1), dtb,
        off=off, tm=IN_ROWS, sub=IN_SUB, q_scale=q_scale)

    o_attn = _attention(q, k, vt, tq=ATTN_TILE, tk=ATTN_TILE, ntile=ATTN_TILES_PER_STEP)

    alog = _pad_cols(jnp.concatenate([a_log, a_log]).reshape(1, -1), LANES)
    e_head = jnp.repeat(jnp.eye(nheads, dtype=BF16), SSM_HEAD_DIM, axis=1)
    e2 = jnp.pad(jnp.concatenate([e_head, e_head], axis=0), ((0, LANES - 2 * nheads), (0, 0)))
    dskip_x = jnp.repeat(d_skip, SSM_HEAD_DIM).reshape(1, -1)
    alpha = (2.0 * depth) ** 0.25
    return _ssd_out(xs, bm, cm, dt, gzs, alog, dskip_x, ssm_norm_g.reshape(1, -1), e2,
                    o_attn, gza, x2, w_out, mod3, ln_g.reshape(1, -1), ln_b.reshape(1, -1),
                    rows=SSD_ROWS, sub=OUT_SUB, nheads=nheads, alpha=alpha)


def kernel(x, c, positions, w_ada, b_ada, w_in, q_norm_g, w_qb, kv_norm_g, w_kvb, conv_w, conv_b,
           dt_bias, a_log, d_skip, ssm_norm_g, w_out, ln_g, ln_b):
    b, s, d = x.shape
    depth = w_in.shape[0]
    assert b == 1, "one sequence per call"
    h = x.reshape(s, d)
    pos_row = positions.reshape(1, s)
    for l in range(depth):
        h = _layer(h, c, pos_row, w_ada[l], b_ada[l], w_in[l], q_norm_g[l], w_qb[l], kv_norm_g[l],
                   w_kvb[l], conv_w[l], conv_b[l], dt_bias[l], a_log[l], d_skip[l], ssm_norm_g[l],
                   w_out[l], ln_g[l], ln_b[l], depth=depth)
    return h.reshape(b, s, d)
```

```python
import functools
import math

import jax
import jax.numpy as jnp
from jax import lax
from jax.experimental import pallas as pl
from jax.experimental.pallas import tpu as pltpu

F32 = jnp.float32
BF16 = jnp.bfloat16

MLA_HEADS = 8
QK_NOPE_DIM = 128
QK_ROPE_DIM = 64
V_HEAD_DIM = 128
ROPE_THETA = 10000.0
SSM_HEAD_DIM = 64
SSM_GROUPS = 2
SSM_STATE = 128
CHUNK = 128
RMS_EPS = 1e-6
LN_EPS = 1e-5

LANES = 128
SUBLANES = 8
MXU_DIM = 256
QK_PAD = MXU_DIM
VMEM_LIMIT = 56 * 1024 * 1024

IN_ROWS = 512
IN_SUB = 256
OUT_SUB = 256
ATTN_TILE = 1024
ATTN_TILES_PER_STEP = 2
SSD_ROWS = 512

LOG2E = 1.4426950408889634
NEG_BIG = -1e30


def _silu(z):
    h = 0.5 * z
    return h * jnp.tanh(h) + h


def _softplus(z):
    return jnp.maximum(z, 0.0) + jnp.log1p(jnp.exp(-jnp.abs(z)))


def _split_bf16(x, parts):
    out, rem = [], x
    for _ in range(parts):
        hi = rem.astype(BF16)
        out.append(hi)
        rem = rem - hi.astype(F32)
    return out


def _mod_kernel(c_ref, w_ref, b_ref, o_ref):
    o_ref[...] = jnp.sum(w_ref[...] * c_ref[...], axis=0, keepdims=True) + b_ref[...]


def _adaln_mod(c_col, w_ada, b_ada):
    d, n = w_ada.shape
    bn = 1024
    return pl.pallas_call(
        _mod_kernel,
        grid=(n // bn,),
        in_specs=[pl.BlockSpec((d, 1), lambda j: (0, 0)),
                  pl.BlockSpec((d, bn), lambda j: (0, j)),
                  pl.BlockSpec((1, bn), lambda j: (0, j))],
        out_specs=pl.BlockSpec((1, bn), lambda j: (0, j)),
        out_shape=jax.ShapeDtypeStruct((1, n), F32),
        name="adaln_mod",
    )(c_col, w_ada, b_ada)


def _in_kernel(x_ref, mod_ref, pos_ref, rope_ref, w_in_ref, qg_ref, w_qb_ref, kvg_ref, w_k_ref, w_vt_ref,
               conv_w_ref, conv_b_ref, dtb_ref,
               q_out, k_out, vt_out, gza_out, xs_out, bm_out, cm_out, dt_out, gzs_out,
               xbuf, *, tm, sub, off, q_scale):
    i = pl.program_id(0)
    shift = mod_ref[0:1, :]
    scale1 = 1.0 + mod_ref[1:2, :]
    lane = lax.broadcasted_iota(jnp.int32, (sub, LANES), 1)
    low_half = lane < QK_ROPE_DIM
    nconv, nch = conv_w_ref.shape
    nx = xs_out.shape[1]
    nb = bm_out.shape[1]

    @pl.when(i == 0)
    def _():
        xbuf[0:SUBLANES, :] = jnp.zeros((SUBLANES, nch), F32)

    css = []
    for h in range(tm // sub):
        ang_t = pos_ref[:, h * sub:(h + 1) * sub].astype(F32) * rope_ref[...]
        cos_t, sin_t = jnp.cos(ang_t), jnp.sin(ang_t)
        css.append(jnp.concatenate([cos_t, cos_t, -sin_t, sin_t], axis=0).T)

    projs = []
    for h in range(tm // sub):
        u = x_ref[h * sub:(h + 1) * sub, :] * scale1 + shift
        projs.append(jnp.dot(u.astype(BF16), w_in_ref[...], preferred_element_type=F32))

    for h, (proj, cs) in enumerate(zip(projs, css)):
        rows = slice(h * sub, (h + 1) * sub)

        def rope(rr, cs=cs):
            t = rr * cs
            return jnp.where(low_half, t + pltpu.roll(t, QK_ROPE_DIM, axis=1), 0.0)

        q_lat = proj[:, off["q"]:off["q"] + qg_ref.shape[1]]
        qn = q_lat * lax.rsqrt(jnp.mean(q_lat * q_lat, axis=-1, keepdims=True) + RMS_EPS) * qg_ref[...]
        qf = jnp.dot(qn.astype(BF16), w_qb_ref[...], preferred_element_type=F32)
        for hd in range(MLA_HEADS):
            c0 = hd * QK_PAD
            q_out[rows, c0:c0 + QK_NOPE_DIM] = (qf[:, c0:c0 + QK_NOPE_DIM] * q_scale).astype(BF16)
            q_out[rows, c0 + QK_NOPE_DIM:c0 + QK_PAD] = (
                rope(qf[:, c0 + QK_NOPE_DIM:c0 + QK_PAD]) * q_scale).astype(BF16)

        c_kv = proj[:, off["ckv"]:off["ckv"] + kvg_ref.shape[1]]
        ckvn = c_kv * lax.rsqrt(jnp.mean(c_kv * c_kv, axis=-1, keepdims=True) + RMS_EPS) * kvg_ref[...]
        ckvn_bf = ckvn.astype(BF16)
        kf = jnp.dot(ckvn_bf, w_k_ref[...], preferred_element_type=F32)
        vt_out[:, rows] = lax.dot_general(w_vt_ref[...], ckvn_bf, (((1,), (1,)), ((), ())),
                                          preferred_element_type=F32).astype(BF16)
        k_rope = rope(proj[:, off["krope"]:off["krope"] + LANES]).astype(BF16)
        for hd in range(MLA_HEADS):
            k_out[rows, hd * QK_PAD:hd * QK_PAD + QK_NOPE_DIM] = (
                kf[:, hd * QK_NOPE_DIM:(hd + 1) * QK_NOPE_DIM].astype(BF16))
            k_out[rows, hd * QK_PAD + QK_NOPE_DIM:(hd + 1) * QK_PAD] = k_rope

        gza_out[rows, :] = _silu(proj[:, off["za"]:off["za"] + gza_out.shape[1]]).astype(BF16)
        gzs_out[rows, :] = _silu(proj[:, off["zs"]:off["zs"] + gzs_out.shape[1]]).astype(BF16)
        dt_out[rows, :] = _softplus(proj[:, off["dt"]:off["dt"] + LANES] + dtb_ref[...])

        r1 = SUBLANES + h * sub
        xbuf[r1:r1 + sub, :] = proj[:, off["xbc"]:off["xbc"] + nch]
        xfull = xbuf[r1 - SUBLANES:r1 + sub, :]
        acc = conv_b_ref[...] + conv_w_ref[nconv - 1:nconv, :] * xfull[SUBLANES:, :]
        for k in range(1, nconv):
            shifted = pltpu.roll(xfull, k, axis=0)[SUBLANES:, :]
            acc = acc + conv_w_ref[nconv - 1 - k:nconv - k, :] * shifted
        xc = _silu(acc)
        xs_out[rows, :] = xc[:, :nx].astype(BF16)
        bm_out[rows, :] = xc[:, nx:nx + nb].astype(BF16)
        cm_out[rows, :] = xc[:, nx + nb:nx + 2 * nb].astype(BF16)

    xbuf[0:SUBLANES, :] = xbuf[tm:tm + SUBLANES, :]


def _in_proj(x2, mod3, pos_row, rope_tab, w_in_p, qg, w_qb_p, kvg, w_k, w_vt, conv_w, conv_b, dtb,
             *, off, tm, sub, q_scale):
    s, d = x2.shape
    nq = MLA_HEADS * QK_PAD
    nv = MLA_HEADS * V_HEAD_DIM
    nch = conv_w.shape[1]
    nbc = SSM_GROUPS * SSM_STATE
    nx = nch - 2 * nbc
    row = lambda w: pl.BlockSpec((tm, w), lambda i: (i, 0))
    full = lambda a: pl.BlockSpec(a.shape, lambda i: (0,) * a.ndim, pipeline_mode=pl.Buffered(1))
    outs = [
        jax.ShapeDtypeStruct((s, nq), BF16),
        jax.ShapeDtypeStruct((s, nq), BF16),
        jax.ShapeDtypeStruct((nv, s), BF16),
        jax.ShapeDtypeStruct((s, nv), BF16),
        jax.ShapeDtypeStruct((s, nx), BF16),
        jax.ShapeDtypeStruct((s, nbc), BF16),
        jax.ShapeDtypeStruct((s, nbc), BF16),
        jax.ShapeDtypeStruct((s, LANES), F32),
        jax.ShapeDtypeStruct((s, nx), BF16),
    ]
    return pl.pallas_call(
        functools.partial(_in_kernel, tm=tm, sub=sub, off=off, q_scale=q_scale),
        grid=(s // tm,),
        in_specs=[row(d), full(mod3), pl.BlockSpec((1, tm), lambda i: (0, i)), full(rope_tab),
                  full(w_in_p), full(qg), full(w_qb_p), full(kvg), full(w_k), full(w_vt),
                  full(conv_w), full(conv_b), full(dtb)],
        out_specs=[pl.BlockSpec((nv, tm), lambda i: (0, i)) if n == 2 else row(o.shape[1])
                   for n, o in enumerate(outs)],
        out_shape=outs,
        scratch_shapes=[pltpu.VMEM((tm + 2 * SUBLANES, nch), F32)],
        compiler_params=pltpu.CompilerParams(dimension_semantics=("arbitrary",),
                                             vmem_limit_bytes=VMEM_LIMIT),
        name="in_proj",
    )(x2, mod3, pos_row, rope_tab, w_in_p, qg, w_qb_p, kvg, w_k, w_vt, conv_w, conv_b, dtb)


def _sublane_allmax(x):
    shift = SUBLANES // 2
    while shift:
        x = jnp.maximum(x, pltpu.roll(x, shift, axis=0))
        shift //= 2
    return x


def _sublane_allsum(x):
    shift = SUBLANES // 2
    while shift:
        x = x + pltpu.roll(x, shift, axis=0)
        shift //= 2
    return x


BF16_ROWS = 16
ACC_ROWS = V_HEAD_DIM + BF16_ROWS
EXP_ROWS = 32


def _attn_kernel(q_ref, qn_ref, k_ref, vt_ref, o_ref, m_a, acc_a, m_b, acc_b,
                 s_a, s_b, s_c, cmax_a, cmax_b, cmax_c, p_a, p_b, p_c, al_a, al_b, al_c,
                 *, tq, tk, ntile, n_tiles):
    assert tq == tk
    g = pl.program_id(1)
    nsub = tk // SUBLANES
    nacc = ACC_ROWS // SUBLANES
    buf_a = (s_a, cmax_a, p_a, al_a)
    buf_b = (s_b, cmax_b, p_b, al_b)
    buf_c = (s_c, cmax_c, p_c, al_c)
    ones_rows = jnp.ones((ACC_ROWS - V_HEAD_DIM, tk), BF16)

    def reset(stats):
        m_scr, acc_scr = stats
        m_scr[...] = jnp.full(m_scr.shape, NEG_BIG, F32)
        acc_scr[...] = jnp.zeros(acc_scr.shape, F32)

    def qk_stage(kv, q_rows, buf, diag=False, q_ref=q_ref):
        s_out, cmax_out = buf[0], buf[1]
        start = pl.multiple_of(kv * tk, tk)
        nt = (((1,), (1,)), ((), ()))
        colmax = lambda x: _sublane_allmax(jnp.max(x.reshape(x.shape[0] // SUBLANES, SUBLANES, x.shape[1]), axis=0))
        if not diag:
            st = lax.dot_general(k_ref[pl.ds(start, tk), :], q_ref[q_rows, :], nt,
                                 preferred_element_type=F32)
            s_out[...] = st
            cmax_out[...] = colmax(st)
            return
        hk = tk // 2
        q_lo = q_rows.start
        mask = (lax.broadcasted_iota(jnp.int32, (hk, tq), 0) <= lax.broadcasted_iota(jnp.int32, (hk, tq), 1))
        top = lax.dot_general(k_ref[pl.ds(start, hk), :], q_ref[q_rows, :], nt, preferred_element_type=F32)
        top = jnp.where(mask, top, NEG_BIG)
        bot = lax.dot_general(k_ref[pl.ds(start + hk, hk), :], q_ref[q_lo + hk:q_lo + tq, :], nt,
                              preferred_element_type=F32)
        bot = jnp.where(mask[:, :tq - hk], bot, NEG_BIG)
        s_out[0:hk, :] = top
        s_out[hk:tk, 0:hk] = jnp.full((tk - hk, hk), NEG_BIG, F32)
        s_out[hk:tk, hk:tq] = bot
        cm_top = colmax(top)
        cmax_out[:, 0:hk] = cm_top[:, 0:hk]
        cmax_out[:, hk:tq] = jnp.maximum(cm_top[:, hk:tq], colmax(bot))

    def softmax_stage(buf, stats):
        s_in, cmax_in, p_out, al_out = buf
        m_scr = stats[0]
        m_old = m_scr[...]
        m_new = jnp.maximum(m_old, cmax_in[...])
        al_out[...] = jnp.exp2(m_old - m_new)
        m_scr[...] = m_new
        for c in range(0, tq, MXU_DIM):
            m_c = m_new[:, c:c + MXU_DIM][None]
            for r in range(0, tk, EXP_ROWS):
                sc = s_in[r:r + EXP_ROWS, c:c + MXU_DIM].reshape(EXP_ROWS // SUBLANES, SUBLANES, MXU_DIM)
                p_out[r:r + EXP_ROWS, c:c + MXU_DIM] = (
                    jnp.exp2(sc - m_c).reshape(EXP_ROWS, MXU_DIM).astype(BF16))

    def pv_stage(kv, buf, stats):
        p_in, al_in = buf[2], buf[3]
        acc_scr = stats[1]
        start = pl.multiple_of(jnp.maximum(kv, 0) * tk, tk)
        vt_aug = jnp.concatenate([vt_ref[:, pl.ds(start, tk)], ones_rows], axis=0)
        pv = jnp.dot(vt_aug, p_in[...], preferred_element_type=F32)
        acc3 = acc_scr[...].reshape(nacc, SUBLANES, tq) * al_in[...][None]
        acc_scr[...] = acc3.reshape(ACC_ROWS, tq) + pv

    def finalize(q_rows, stats):
        acc_scr = stats[1]
        inv_l = 1.0 / acc_scr[V_HEAD_DIM:V_HEAD_DIM + SUBLANES, :]
        o_t = acc_scr[0:V_HEAD_DIM, :].reshape(V_HEAD_DIM // SUBLANES, SUBLANES, tq) * inv_l[None]
        o_ref[q_rows, :] = o_t.reshape(V_HEAD_DIM, tq).T.astype(o_ref.dtype)

    def make_step(qi, q_rows, stats):
        kv_of = lambda pos: jnp.where(pos == 0, qi, pos - 1)

        def step(pos, buf, other):
            qk_stage(kv_of(pos), q_rows, buf)
            pv_stage(kv_of(pos - 2), buf, stats)
            softmax_stage(other, stats)

        return kv_of, step

    def run_pairs(lo, hi, pair):
        def body(pp, carry):
            pair(pp)
            return carry

        lax.fori_loop(lo, hi, body, 0)

    bufs = (buf_a, buf_b)
    stats2 = ((m_a, acc_a), (m_b, acc_b))
    start = [0]
    for j in range(ntile - 1):
        last = start[j] if j % 2 == 0 else 1 - start[j]
        start.append(1 - last)

    def tile_ctx(j):
        qi = ntile * g + j
        rows = slice(j * tq, (j + 1) * tq)
        stats = stats2[j % 2]
        x, y = bufs[start[j]], bufs[1 - start[j]]
        kv_of, step = make_step(qi, rows, stats)
        return qi, rows, stats, x, y, kv_of, step

    def run_steps(j, first_pos, n_pairs):
        _, _, _, x, y, _, step = tile_ctx(j)
        even_buf, odd_buf = x, y

        def pair(pp):
            p0 = first_pos + 2 * pp
            if first_pos % 2 == 0:
                step(p0, even_buf, odd_buf)
                step(p0 + 1, odd_buf, even_buf)
            else:
                step(p0, odd_buf, even_buf)
                step(p0 + 1, even_buf, odd_buf)

        run_pairs(0, n_pairs, pair)

    def transition(j, last_buf, other):
        qi, rows, stats, _, _, kv_of, _ = tile_ctx(j)
        qi_n, rows_n, st_n, x_n, y_n, kv_n, _ = tile_ctx(j + 1)
        if other is not None:
            assert x_n is other
            pv_stage(kv_of(qi - 1), other, stats)
        reset(st_n)
        qk_stage(qi_n, rows_n, x_n, diag=True)
        softmax_stage(last_buf, stats)
        pv_stage(kv_of(qi), last_buf, stats)
        qk_stage(kv_n(1), rows_n, y_n)
        softmax_stage(x_n, st_n)
        finalize(rows, stats)

    qi0, rows0, st0, x0, y0, kv0, step0 = tile_ctx(0)
    reset(st0)

    @pl.when(g == 0)
    def _():
        qk_stage(qi0, rows0, buf_c, diag=True)
        transition(0, buf_c, None)

    @pl.when(g > 0)
    def _():
        qk_stage(kv0(1), rows0, y0)
        softmax_stage(buf_c, st0)
        qk_stage(kv0(2), rows0, x0)
        pv_stage(kv0(0), buf_c, st0)
        softmax_stage(y0, st0)
        run_steps(0, 3, (ntile * g - 2) // 2)
        transition(0, x0, y0)

    for j in range(1, ntile):
        qi, rows, stats, x, y, kv_of, step = tile_ctx(j)
        if j % 2 == 0:
            step(2, x, y)
            run_steps(j, 3, (ntile * g + j - 2) // 2)
        else:
            run_steps(j, 2, (ntile * g + j - 1) // 2)
        last_buf, other = (x, y) if j % 2 == 0 else (y, x)
        if j + 1 < ntile:
            transition(j, last_buf, other)
        else:
            pv_stage(kv_of(qi - 1), other, stats)
            qk_stage(jnp.minimum(qi + 1, n_tiles - 1), slice(0, tq), buf_c, diag=True, q_ref=qn_ref)
            softmax_stage(last_buf, stats)
            pv_stage(kv_of(qi), last_buf, stats)
            finalize(rows, stats)


def _attention(q, k, vt, *, tq, tk, ntile):
    s = q.shape[0]
    assert ntile % 2 == 0 and s % (ntile * tq) == 0, "sequence length must be a multiple of ntile query tiles"
    stat = pltpu.VMEM((SUBLANES, tq), F32)
    n_tiles = s // tq
    return pl.pallas_call(
        functools.partial(_attn_kernel, tq=tq, tk=tk, ntile=ntile, n_tiles=n_tiles),
        grid=(MLA_HEADS, s // (ntile * tq)),
        in_specs=[pl.BlockSpec((ntile * tq, QK_PAD), lambda h, i: (i, h)),
                  pl.BlockSpec((tq, QK_PAD), lambda h, i: (jnp.minimum(ntile * (i + 1), n_tiles - 1), h)),
                  pl.BlockSpec((s, QK_PAD), lambda h, i: (0, h)),
                  pl.BlockSpec((V_HEAD_DIM, s), lambda h, i: (h, 0))],
        out_specs=pl.BlockSpec((ntile * tq, V_HEAD_DIM), lambda h, i: (i, h)),
        out_shape=jax.ShapeDtypeStruct((s, MLA_HEADS * V_HEAD_DIM), BF16),
        scratch_shapes=[stat, pltpu.VMEM((ACC_ROWS, tq), F32), stat, pltpu.VMEM((ACC_ROWS, tq), F32),
                        pltpu.VMEM((tk, tq), F32), pltpu.VMEM((tk, tq), F32), pltpu.VMEM((tk, tq), F32),
                        stat, stat, stat,
                        pltpu.VMEM((tk, tq), BF16), pltpu.VMEM((tk, tq), BF16), pltpu.VMEM((tk, tq), BF16),
                        stat, stat, stat],
        compiler_params=pltpu.CompilerParams(dimension_semantics=("arbitrary", "arbitrary"),
                                             vmem_limit_bytes=VMEM_LIMIT),
        name="attention",
    )(q, q, k, vt)


def _ssd_out_kernel(xs_ref, bm_ref, cm_ref, dt_ref, gz_ref, alog_ref, dskip_ref, g_ref, e2_ref,
                    oa_ref, gza_ref, x_ref, w32_ref, mod_ref, lng_ref, lnb_ref,
                    o_ref, state, y_scr, w_ref, os_scr, *, rows, nheads, alpha, sub):
    i = pl.program_id(0)
    gw = SSM_STATE
    hw = state.shape[2]
    na = oa_ref.shape[1]

    @pl.when(i == 0)
    def _():
        state[...] = jnp.zeros(state.shape, F32)
        for r in range(0, w_ref.shape[0], sub):
            w_ref[r:r + sub, :] = w32_ref[r:r + sub, :].astype(BF16)

    mixes = []
    for r in range(0, rows, sub):
        oa = (oa_ref[r:r + sub, :].astype(F32) * gza_ref[r:r + sub, :].astype(F32)).astype(BF16)
        mixes.append(jnp.dot(oa, w_ref[0:na, :], preferred_element_type=F32))

    a_neg = -jnp.exp(alog_ref[...])
    r_i = lax.broadcasted_iota(jnp.int32, (CHUNK, CHUNK), 0)
    c_i = lax.broadcasted_iota(jnp.int32, (CHUNK, CHUNK), 1)
    tri = c_i <= r_i
    tri_bf = tri.astype(BF16)
    lane = lax.broadcasted_iota(jnp.int32, (CHUNK, LANES), 1)
    first_copy = lane < nheads
    head_lo = lane < SSM_HEAD_DIM
    e2 = e2_ref[...]

    def expand(v):
        hi = v.astype(BF16)
        lo = (v - hi.astype(F32)).astype(BF16)
        return jnp.dot(jnp.where(first_copy, hi, lo), e2, preferred_element_type=F32)

    for cidx in range(rows // CHUNK):
        sl = pl.ds(cidx * CHUNK, CHUNK)
        dt = dt_ref[sl, :]
        da = dt * (a_neg * LOG2E)
        a_cum = jnp.zeros((CHUNK, LANES), F32)
        for part in _split_bf16(da, 3):
            a_cum = a_cum + jnp.dot(tri_bf, part, preferred_element_type=F32)
        a_cum_t = a_cum.T
        a_last = a_cum[CHUNK - 1:CHUNK, :]
        ea = jnp.exp2(a_cum)
        dte = jnp.exp2(a_last - a_cum)

        xs = xs_ref[sl, :].astype(F32)
        xd = xs * expand(dt)
        ea_x = expand(ea)
        xdd = (xd * expand(dte)).astype(BF16)
        xd_bf = xd.astype(BF16)

        for g in range(SSM_GROUPS):
            bg = bm_ref[sl, g * gw:(g + 1) * gw]
            cg = cm_ref[sl, g * gw:(g + 1) * gw]
            cb = lax.dot_general(cg, bg, (((1,), (1,)), ((), ())), preferred_element_type=F32)
            prev = state[g]
            y_off = jnp.dot(cg, prev.astype(BF16), preferred_element_type=F32)
            new = lax.dot_general(bg, xdd[:, g * hw:(g + 1) * hw], (((0,), (0,)), ((), ())),
                                  preferred_element_type=F32)
            state[g] = prev * ea_x[CHUNK - 1:CHUNK, g * hw:(g + 1) * hw] + new
            y_scr[:, g * hw:(g + 1) * hw] = y_off * ea_x[:, g * hw:(g + 1) * hw]

            hpg = hw // SSM_HEAD_DIM
            for pair in range(hpg // 2):
                h0 = g * hpg + 2 * pair
                c0 = h0 * SSM_HEAD_DIM
                xp = xd_bf[:, c0:c0 + LANES]
                yp = jnp.zeros((CHUNK, LANES), F32)
                for k, keep in ((0, head_lo), (1, ~head_lo)):
                    h = h0 + k
                    seg = a_cum[:, h:h + 1] - a_cum_t[h:h + 1, :]
                    m_h = (cb * jnp.exp2(jnp.where(tri, seg, NEG_BIG))).astype(BF16)
                    yp = yp + jnp.dot(m_h, jnp.where(keep, xp, jnp.zeros_like(xp)),
                                      preferred_element_type=F32)
                y_scr[:, c0:c0 + LANES] = y_scr[:, c0:c0 + LANES] + yp

        y = y_scr[...] + xs * dskip_ref[...]
        hf = y * gz_ref[sl, :].astype(F32)
        for g in range(SSM_GROUPS):
            hg = hf[:, g * hw:(g + 1) * hw]
            ms = jnp.mean(hg * hg, axis=-1, keepdims=True)
            os_scr[sl, g * hw:(g + 1) * hw] = (hg * lax.rsqrt(ms + RMS_EPS)
                                                * g_ref[:, g * hw:(g + 1) * hw]).astype(os_scr.dtype)

    for n, mixed in enumerate(mixes):
        rws = slice(n * sub, (n + 1) * sub)
        mixed = mixed + jnp.dot(os_scr[rws, :], w_ref[na:, :], preferred_element_type=F32)
        y = alpha * x_ref[rws, :] + mod_ref[2:3, :] * mixed
        mu = jnp.mean(y, axis=-1, keepdims=True)
        yc = y - mu
        var = jnp.mean(yc * yc, axis=-1, keepdims=True)
        o_ref[rws, :] = yc * lax.rsqrt(var + LN_EPS) * lng_ref[...] + lnb_ref[...]


def _ssd_out(xs, bm, cm, dt, gzs, alog, dskip_x, norm_g, e2, oa, gza, x2, w_out, mod3, ln_g, ln_b,
             *, rows, sub, nheads, alpha):
    s, nx = xs.shape
    d = x2.shape[1]
    hw = nx // SSM_GROUPS
    row = lambda a: pl.BlockSpec((rows, a.shape[1]), lambda i: (i, 0))
    full = lambda a: pl.BlockSpec(a.shape, lambda i: (0,) * a.ndim)
    return pl.pallas_call(
        functools.partial(_ssd_out_kernel, rows=rows, nheads=nheads, alpha=alpha, sub=sub),
        grid=(s // rows,),
        in_specs=[row(xs), row(bm), row(cm), row(dt), row(gzs),
                  full(alog), full(dskip_x), full(norm_g), full(e2),
                  row(oa), row(gza), row(x2),
                  pl.BlockSpec(w_out.shape, lambda i: (0, 0), pipeline_mode=pl.Buffered(1)),
                  full(mod3), full(ln_g), full(ln_b)],
        out_specs=pl.BlockSpec((rows, d), lambda i: (i, 0)),
        out_shape=jax.ShapeDtypeStruct((s, d), F32),
        scratch_shapes=[pltpu.VMEM((SSM_GROUPS, SSM_STATE, hw), F32),
                        pltpu.VMEM((CHUNK, nx), F32),
                        pltpu.VMEM(w_out.shape, BF16),
                        pltpu.VMEM((rows, nx), BF16)],
        compiler_params=pltpu.CompilerParams(dimension_semantics=("arbitrary",),
                                             vmem_limit_bytes=VMEM_LIMIT),
        name="ssd_out",
    )(xs, bm, cm, dt, gzs, alog, dskip_x, norm_g, e2, oa, gza, x2, w_out, mod3, ln_g, ln_b)


def _pad_cols(a, width):
    return jnp.pad(a, ((0, 0), (0, width - a.shape[1])))


def _regroup_kernel(wt_ref, o_ref, *, segments):
    wt = wt_ref[...]
    parts = [jnp.zeros((b, wt.shape[1]), wt.dtype) if a is None else wt[a:b, :] for a, b in segments]
    o_ref[...] = jnp.concatenate(parts, axis=0).T.astype(o_ref.dtype)


def _regroup_columns(w, segments, width):
    assert all(b % SUBLANES == 0 and (a or 0) % SUBLANES == 0 for a, b in segments)
    rows = w.shape[0]
    br = 256
    return pl.pallas_call(
        functools.partial(_regroup_kernel, segments=segments),
        grid=(rows // br,),
        in_specs=[pl.BlockSpec((w.shape[1], br), lambda i: (0, i))],
        out_specs=pl.BlockSpec((br, width), lambda i: (i, 0)),
        out_shape=jax.ShapeDtypeStruct((rows, width), BF16),
        compiler_params=pltpu.CompilerParams(vmem_limit_bytes=VMEM_LIMIT),
        name="regroup_w_in",
    )(w.T)


def _layer(x2, c, pos_row, w_ada, b_ada, w_in, q_norm_g, w_qb, kv_norm_g, w_kvb,
           conv_w, conv_b, dt_bias, a_log, d_skip, ssm_norm_g, w_out, ln_g, ln_b, *, depth):
    s, d = x2.shape
    q_rank = q_norm_g.shape[0]
    kv_rank = kv_norm_g.shape[0]
    nheads = dt_bias.shape[0]
    nch = conv_w.shape[1]
    nv = MLA_HEADS * V_HEAD_DIM
    nx = nheads * SSM_HEAD_DIM
    half = QK_ROPE_DIM // 2

    mod = _adaln_mod(c.reshape(d, 1), w_ada, b_ada.reshape(1, -1))
    mod3 = mod.reshape(3, d)

    o_q, o_ckv = 0, q_rank
    o_kr = o_ckv + kv_rank
    o_za = o_kr + QK_ROPE_DIM
    o_xbc = o_za + nv
    o_dt = o_xbc + nch
    o_zs = o_dt + nheads
    dt_pad = LANES - 2 * nheads
    groups = [("q", [(o_q, o_ckv)]), ("ckv", [(o_ckv, o_kr)]),
              ("krope", [(o_kr, o_za), (o_kr + half, o_za), (o_kr, o_kr + half)]),
              ("za", [(o_za, o_xbc)]), ("xbc", [(o_xbc, o_dt)]), ("zs", [(o_zs, o_zs + nx)]),
              ("dt", [(o_dt, o_zs), (o_dt, o_zs), (None, dt_pad)])]
    off, cur, segments = {}, 0, []
    for name, segs in groups:
        assert cur % LANES == 0
        off[name] = cur
        for a, b in segs:
            cur += b if a is None else b - a
            if segments and a is not None and segments[-1][0] is not None and segments[-1][1] == a:
                segments[-1] = (segments[-1][0], b)
            else:
                segments.append((a, b))
    assert cur % LANES == 0
    w_in_p = _regroup_columns(w_in, tuple(segments), cur)

    w3 = w_qb.reshape(q_rank, MLA_HEADS, QK_NOPE_DIM + QK_ROPE_DIM)
    w_rope = w3[:, :, QK_NOPE_DIM:]
    w_qb_p = jnp.concatenate([w3, w_rope[:, :, half:], w_rope[:, :, :half]], axis=2)
    w_qb_p = w_qb_p.reshape(q_rank, MLA_HEADS * QK_PAD).astype(BF16)

    inv_freq = 1.0 / (ROPE_THETA ** (jnp.arange(half, dtype=F32) / half))
    rope_tab = jnp.broadcast_to(inv_freq[:, None], (half, IN_SUB))

    dtb = _pad_cols(jnp.concatenate([dt_bias, dt_bias]).reshape(1, -1), LANES)
    q_scale = (QK_NOPE_DIM + QK_ROPE_DIM) ** -0.5 * LOG2E

    wkv3 = w_kvb.reshape(kv_rank, MLA_HEADS, QK_NOPE_DIM + V_HEAD_DIM)
    w_k = wkv3[:, :, :QK_NOPE_DIM].reshape(kv_rank, MLA_HEADS * QK_NOPE_DIM).astype(BF16)
    w_vt = wkv3[:, :, QK_NOPE_DIM:].reshape(kv_rank, nv).T.astype(BF16)

    q, k, vt, gza, xs, bm, cm, dt, gzs = _in_proj(
        x2, mod3, pos_row, rope_tab, w_in_p, q_norm_g.reshape(1, -1), w_qb_p,
        kv_norm_g.reshape(1, -1), w_k, w_vt, conv_w, conv_b.reshape(1, -1), dtb,
        off=off, tm=IN_ROWS, sub=IN_SUB, q_scale=q_scale)

    o_attn = _attention(q, k, vt, tq=ATTN_TILE, tk=ATTN_TILE, ntile=ATTN_TILES_PER_STEP)

    alog = _pad_cols(jnp.concatenate([a_log, a_log]).reshape(1, -1), LANES)
    e_head = jnp.repeat(jnp.eye(nheads, dtype=BF16), SSM_HEAD_DIM, axis=1)
    e2 = jnp.pad(jnp.concatenate([e_head, e_head], axis=0), ((0, LANES - 2 * nheads), (0, 0)))
    dskip_x = jnp.repeat(d_skip, SSM_HEAD_DIM).reshape(1, -1)
    alpha = (2.0 * depth) ** 0.25
    return _ssd_out(xs, bm, cm, dt, gzs, alog, dskip_x, ssm_norm_g.reshape(1, -1), e2,
                    o_attn, gza, x2, w_out, mod3, ln_g.reshape(1, -1), ln_b.reshape(1, -1),
                    rows=SSD_ROWS, sub=OUT_SUB, nheads=nheads, alpha=alpha)


def kernel(x, c, positions, w_ada, b_ada, w_in, q_norm_g, w_qb, kv_norm_g, w_kvb, conv_w, conv_b,
           dt_bias, a_log, d_skip, ssm_norm_g, w_out, ln_g, ln_b):
    b, s, d = x.shape
    depth = w_in.shape[0]
    assert b == 1, "one sequence per call"
    h = x.reshape(s, d)
    pos_row = positions.reshape(1, s)
    for l in range(depth):
        h = _layer(h, c, pos_row, w_ada[l], b_ada[l], w_in[l], q_norm_g[l], w_qb[l], kv_norm_g[l],
                   w_kvb[l], conv_w[l], conv_b[l], dt_bias[l], a_log[l], d_skip[l], ssm_norm_g[l],
                   w_out[l], ln_g[l], ln_b[l], depth=depth)
    return h.reshape(b, s, d)
```

```python
import functools
import math

import jax
import jax.numpy as jnp
from jax import lax
from jax.experimental import pallas as pl
from jax.experimental.pallas import tpu as pltpu

F32 = jnp.float32
BF16 = jnp.bfloat16

MLA_HEADS = 8
QK_NOPE_DIM = 128
QK_ROPE_DIM = 64
V_HEAD_DIM = 128
ROPE_THETA = 10000.0
SSM_HEAD_DIM = 64
SSM_GROUPS = 2
SSM_STATE = 128
CHUNK = 128
RMS_EPS = 1e-6
LN_EPS = 1e-5

LANES = 128
SUBLANES = 8
MXU_DIM = 256
QK_PAD = MXU_DIM
VMEM_LIMIT = 56 * 1024 * 1024

IN_ROWS = 512
IN_SUB = 256
OUT_SUB = 256
ATTN_TILE = 1024
ATTN_TILES_PER_STEP = 2
SSD_ROWS = 512

LOG2E = 1.4426950408889634
NEG_BIG = -1e30


def _silu(z):
    h = 0.5 * z
    return h * jnp.tanh(h) + h


def _softplus(z):
    return jnp.maximum(z, 0.0) + jnp.log1p(jnp.exp(-jnp.abs(z)))


def _split_bf16(x, parts):
    out, rem = [], x
    for _ in range(parts):
        hi = rem.astype(BF16)
        out.append(hi)
        rem = rem - hi.astype(F32)
    return out


def _mod_kernel(c_ref, w_ref, b_ref, o_ref):
    o_ref[...] = jnp.sum(w_ref[...] * c_ref[...], axis=0, keepdims=True) + b_ref[...]


def _adaln_mod(c_col, w_ada, b_ada):
    d, n = w_ada.shape
    bn = 1024
    return pl.pallas_call(
        _mod_kernel,
        grid=(n // bn,),
        in_specs=[pl.BlockSpec((d, 1), lambda j: (0, 0)),
                  pl.BlockSpec((d, bn), lambda j: (0, j)),
                  pl.BlockSpec((1, bn), lambda j: (0, j))],
        out_specs=pl.BlockSpec((1, bn), lambda j: (0, j)),
        out_shape=jax.ShapeDtypeStruct((1, n), F32),
        name="adaln_mod",
    )(c_col, w_ada, b_ada)


def _in_kernel(x_ref, mod_ref, pos_ref, rope_ref, w_in_ref, qg_ref, w_qb_ref, kvg_ref, w_k_ref, w_vt_ref,
               conv_w_ref, conv_b_ref, dtb_ref,
               q_out, k_out, vt_out, gza_out, xs_out, bm_out, cm_out, dt_out, gzs_out,
               xbuf, *, tm, sub, off, q_scale):
    i = pl.program_id(0)
    shift = mod_ref[0:1, :]
    scale1 = 1.0 + mod_ref[1:2, :]
    lane = lax.broadcasted_iota(jnp.int32, (sub, LANES), 1)
    low_half = lane < QK_ROPE_DIM
    nconv, nch = conv_w_ref.shape
    nx = xs_out.shape[1]
    nb = bm_out.shape[1]

    @pl.when(i == 0)
    def _():
        xbuf[0:SUBLANES, :] = jnp.zeros((SUBLANES, nch), F32)

    css = []
    for h in range(tm // sub):
        ang_t = pos_ref[:, h * sub:(h + 1) * sub].astype(F32) * rope_ref[...]
        cos_t, sin_t = jnp.cos(ang_t), jnp.sin(ang_t)
        css.append(jnp.concatenate([cos_t, cos_t, -sin_t, sin_t], axis=0).T)

    projs = []
    for h in range(tm // sub):
        u = x_ref[h * sub:(h + 1) * sub, :] * scale1 + shift
        projs.append(jnp.dot(u.astype(BF16), w_in_ref[...], preferred_element_type=F32))

    for h, (proj, cs) in enumerate(zip(projs, css)):
        rows = slice(h * sub, (h + 1) * sub)

        def rope(rr, cs=cs):
            t = rr * cs
            return jnp.where(low_half, t + pltpu.roll(t, QK_ROPE_DIM, axis=1), 0.0)

        q_lat = proj[:, off["q"]:off["q"] + qg_ref.shape[1]]
        qn = q_lat * lax.rsqrt(jnp.mean(q_lat * q_lat, axis=-1, keepdims=True) + RMS_EPS) * qg_ref[...]
        qf = jnp.dot(qn.astype(BF16), w_qb_ref[...], preferred_element_type=F32)
        for hd in range(MLA_HEADS):
            c0 = hd * QK_PAD
            q_out[rows, c0:c0 + QK_NOPE_DIM] = (qf[:, c0:c0 + QK_NOPE_DIM] * q_scale).astype(BF16)
            q_out[rows, c0 + QK_NOPE_DIM:c0 + QK_PAD] = (
                rope(qf[:, c0 + QK_NOPE_DIM:c0 + QK_PAD]) * q_scale).astype(BF16)

        c_kv = proj[:, off["ckv"]:off["ckv"] + kvg_ref.shape[1]]
        ckvn = c_kv * lax.rsqrt(jnp.mean(c_kv * c_kv, axis=-1, keepdims=True) + RMS_EPS) * kvg_ref[...]
        ckvn_bf = ckvn.astype(BF16)
        kf = jnp.dot(ckvn_bf, w_k_ref[...], preferred_element_type=F32)
        vt_out[:, rows] = lax.dot_general(w_vt_ref[...], ckvn_bf, (((1,), (1,)), ((), ())),
                                          preferred_element_type=F32).astype(BF16)
        k_rope = rope(proj[:, off["krope"]:off["krope"] + LANES]).astype(BF16)
        for hd in range(MLA_HEADS):
            k_out[rows, hd * QK_PAD:hd * QK_PAD + QK_NOPE_DIM] = (
                kf[:, hd * QK_NOPE_DIM:(hd + 1) * QK_NOPE_DIM].astype(BF16))
            k_out[rows, hd * QK_PAD + QK_NOPE_DIM:(hd + 1) * QK_PAD] = k_rope

        gza_out[rows, :] = _silu(proj[:, off["za"]:off["za"] + gza_out.shape[1]]).astype(BF16)
        gzs_out[rows, :] = _silu(proj[:, off["zs"]:off["zs"] + gzs_out.shape[1]]).astype(BF16)
        dt_out[rows, :] = _softplus(proj[:, off["dt"]:off["dt"] + LANES] + dtb_ref[...])

        r1 = SUBLANES + h * sub
        xbuf[r1:r1 + sub, :] = proj[:, off["xbc"]:off["xbc"] + nch]
        xfull = xbuf[r1 - SUBLANES:r1 + sub, :]
        acc = conv_b_ref[...] + conv_w_ref[nconv - 1:nconv, :] * xfull[SUBLANES:, :]
        for k in range(1, nconv):
            shifted = pltpu.roll(xfull, k, axis=0)[SUBLANES:, :]
            acc = acc + conv_w_ref[nconv - 1 - k:nconv - k, :] * shifted
        xc = _silu(acc)
        xs_out[rows, :] = xc[:, :nx].astype(BF16)
        bm_out[rows, :] = xc[:, nx:nx + nb].astype(BF16)
        cm_out[rows, :] = xc[:, nx + nb:nx + 2 * nb].astype(BF16)

    xbuf[0:SUBLANES, :] = xbuf[tm:tm + SUBLANES, :]


def _in_proj(x2, mod3, pos_row, rope_tab, w_in_p, qg, w_qb_p, kvg, w_k, w_vt, conv_w, conv_b, dtb,
             *, off, tm, sub, q_scale):
    s, d = x2.shape
    nq = MLA_HEADS * QK_PAD
    nv = MLA_HEADS * V_HEAD_DIM
    nch = conv_w.shape[1]
    nbc = SSM_GROUPS * SSM_STATE
    nx = nch - 2 * nbc
    row = lambda w: pl.BlockSpec((tm, w), lambda i: (i, 0))
    full = lambda a: pl.BlockSpec(a.shape, lambda i: (0,) * a.ndim, pipeline_mode=pl.Buffered(1))
    outs = [
        jax.ShapeDtypeStruct((s, nq), BF16),
        jax.ShapeDtypeStruct((s, nq), BF16),
        jax.ShapeDtypeStruct((nv, s), BF16),
        jax.ShapeDtypeStruct((s, nv), BF16),
        jax.ShapeDtypeStruct((s, nx), BF16),
        jax.ShapeDtypeStruct((s, nbc), BF16),
        jax.ShapeDtypeStruct((s, nbc), BF16),
        jax.ShapeDtypeStruct((s, LANES), F32),
        jax.ShapeDtypeStruct((s, nx), BF16),
    ]
    return pl.pallas_call(
        functools.partial(_in_kernel, tm=tm, sub=sub, off=off, q_scale=q_scale),
        grid=(s // tm,),
        in_specs=[row(d), full(mod3), pl.BlockSpec((1, tm), lambda i: (0, i)), full(rope_tab),
                  full(w_in_p), full(qg), full(w_qb_p), full(kvg), full(w_k), full(w_vt),
                  full(conv_w), full(conv_b), full(dtb)],
        out_specs=[pl.BlockSpec((nv, tm), lambda i: (0, i)) if n == 2 else row(o.shape[1])
                   for n, o in enumerate(outs)],
        out_shape=outs,
        scratch_shapes=[pltpu.VMEM((tm + 2 * SUBLANES, nch), F32)],
        compiler_params=pltpu.CompilerParams(dimension_semantics=("arbitrary",),
                                             vmem_limit_bytes=VMEM_LIMIT),
        name="in_proj",
    )(x2, mod3, pos_row, rope_tab, w_in_p, qg, w_qb_p, kvg, w_k, w_vt, conv_w, conv_b, dtb)


def _sublane_allmax(x):
    shift = SUBLANES // 2
    while shift:
        x = jnp.maximum(x, pltpu.roll(x, shift, axis=0))
        shift //= 2
    return x


def _sublane_allsum(x):
    shift = SUBLANES // 2
    while shift:
        x = x + pltpu.roll(x, shift, axis=0)
        shift //= 2
    return x


BF16_ROWS = 16
ACC_ROWS = V_HEAD_DIM + BF16_ROWS
EXP_ROWS = 32


def _attn_kernel(q_ref, qn_ref, k_ref, vt_ref, o_ref, m_a, acc_a, m_b, acc_b,
                 s_a, s_b, s_c, cmax_a, cmax_b, cmax_c, p_a, p_b, p_c, al_a, al_b, al_c,
                 *, tq, tk, ntile, n_tiles):
    assert tq == tk
    g = pl.program_id(1)
    nsub = tk // SUBLANES
    nacc = ACC_ROWS // SUBLANES
    buf_a = (s_a, cmax_a, p_a, al_a)
    buf_b = (s_b, cmax_b, p_b, al_b)
    buf_c = (s_c, cmax_c, p_c, al_c)
    ones_rows = jnp.ones((ACC_ROWS - V_HEAD_DIM, tk), BF16)

    def reset(stats):
        m_scr, acc_scr = stats
        m_scr[...] = jnp.full(m_scr.shape, NEG_BIG, F32)
        acc_scr[...] = jnp.zeros(acc_scr.shape, F32)

    def qk_stage(kv, q_rows, buf, diag=False, q_ref=q_ref):
        s_out, cmax_out = buf[0], buf[1]
        start = pl.multiple_of(kv * tk, tk)
        nt = (((1,), (1,)), ((), ()))
        colmax = lambda x: _sublane_allmax(jnp.max(x.reshape(x.shape[0] // SUBLANES, SUBLANES, x.shape[1]), axis=0))
        if not diag:
            st = lax.dot_general(k_ref[pl.ds(start, tk), :], q_ref[q_rows, :], nt,
                                 preferred_element_type=F32)
            s_out[...] = st
            cmax_out[...] = colmax(st)
            return
        hk = tk // 2
        q_lo = q_rows.start
        mask = (lax.broadcasted_iota(jnp.int32, (hk, tq), 0) <= lax.broadcasted_iota(jnp.int32, (hk, tq), 1))
        top = lax.dot_general(k_ref[pl.ds(start, hk), :], q_ref[q_rows, :], nt, preferred_element_type=F32)
        top = jnp.where(mask, top, NEG_BIG)
        bot = lax.dot_general(k_ref[pl.ds(start + hk, hk), :], q_ref[q_lo + hk:q_lo + tq, :], nt,
                              preferred_element_type=F32)
        bot = jnp.where(mask[:, :tq - hk], bot, NEG_BIG)
        s_out[0:hk, :] = top
        s_out[hk:tk, 0:hk] = jnp.full((tk - hk, hk), NEG_BIG, F32)
        s_out[hk:tk, hk:tq] = bot
        cm_top = colmax(top)
        cmax_out[:, 0:hk] = cm_top[:, 0:hk]
        cmax_out[:, hk:tq] = jnp.maximum(cm_top[:, hk:tq], colmax(bot))

    def softmax_stage(buf, stats):
        s_in, cmax_in, p_out, al_out = buf
        m_scr = stats[0]
        m_old = m_scr[...]
        m_new = jnp.maximum(m_old, cmax_in[...])
        al_out[...] = jnp.exp2(m_old - m_new)
        m_scr[...] = m_new
        for c in range(0, tq, MXU_DIM):
            m_c = m_new[:, c:c + MXU_DIM][None]
            for r in range(0, tk, EXP_ROWS):
                sc = s_in[r:r + EXP_ROWS, c:c + MXU_DIM].reshape(EXP_ROWS // SUBLANES, SUBLANES, MXU_DIM)
                p_out[r:r + EXP_ROWS, c:c + MXU_DIM] = (
                    jnp.exp2(sc - m_c).reshape(EXP_ROWS, MXU_DIM).astype(BF16))

    def pv_stage(kv, buf, stats):
        p_in, al_in = buf[2], buf[3]
        acc_scr = stats[1]
        start = pl.multiple_of(jnp.maximum(kv, 0) * tk, tk)
        vt_aug = jnp.concatenate([vt_ref[:, pl.ds(start, tk)], ones_rows], axis=0)
        pv = jnp.dot(vt_aug, p_in[...], preferred_element_type=F32)
        acc3 = acc_scr[...].reshape(nacc, SUBLANES, tq) * al_in[...][None]
        acc_scr[...] = acc3.reshape(ACC_ROWS, tq) + pv

    def finalize(q_rows, stats):
        acc_scr = stats[1]
        inv_l = 1.0 / acc_scr[V_HEAD_DIM:V_HEAD_DIM + SUBLANES, :]
        o_t = acc_scr[0:V_HEAD_DIM, :].reshape(V_HEAD_DIM // SUBLANES, SUBLANES, tq) * inv_l[None]
        o_ref[q_rows, :] = o_t.reshape(V_HEAD_DIM, tq).T.astype(o_ref.dtype)

    def make_step(qi, q_rows, stats):
        kv_of = lambda pos: jnp.where(pos == 0, qi, pos - 1)

        def step(pos, buf, other):
            qk_stage(kv_of(pos), q_rows, buf)
            pv_stage(kv_of(pos - 2), buf, stats)
            softmax_stage(other, stats)

        return kv_of, step

    def run_pairs(lo, hi, pair):
        def body(pp, carry):
            pair(pp)
            return carry

        lax.fori_loop(lo, hi, body, 0)

    bufs = (buf_a, buf_b)
    stats2 = ((m_a, acc_a), (m_b, acc_b))
    start = [0]
    for j in range(ntile - 1):
        last = start[j] if j % 2 == 0 else 1 - start[j]
        start.append(1 - last)

    def tile_ctx(j):
        qi = ntile * g + j
        rows = slice(j * tq, (j + 1) * tq)
        stats = stats2[j % 2]
        x, y = bufs[start[j]], bufs[1 - start[j]]
        kv_of, step = make_step(qi, rows, stats)
        return qi, rows, stats, x, y, kv_of, step

    def run_steps(j, first_pos, n_pairs):
        _, _, _, x, y, _, step = tile_ctx(j)
        even_buf, odd_buf = x, y

        def pair(pp):
            p0 = first_pos + 2 * pp
            if first_pos % 2 == 0:
                step(p0, even_buf, odd_buf)
                step(p0 + 1, odd_buf, even_buf)
            else:
                step(p0, odd_buf, even_buf)
                step(p0 + 1, even_buf, odd_buf)

        run_pairs(0, n_pairs, pair)

    def transition(j, last_buf, other):
        qi, rows, stats, _, _, kv_of, _ = tile_ctx(j)
        qi_n, rows_n, st_n, x_n, y_n, kv_n, _ = tile_ctx(j + 1)
        reset(st_n)
        qk_stage(qi_n, rows_n, x_n, diag=True)
        if other is not None:
            assert x_n is other
            pv_stage(kv_of(qi - 1), other, stats)
        softmax_stage(last_buf, stats)
        qk_stage(kv_n(1), rows_n, y_n)
        pv_stage(kv_of(qi), last_buf, stats)
        softmax_stage(x_n, st_n)
        finalize(rows, stats)

    qi0, rows0, st0, x0, y0, kv0, step0 = tile_ctx(0)
    reset(st0)

    @pl.when(g == 0)
    def _():
        qk_stage(qi0, rows0, buf_c, diag=True)
        transition(0, buf_c, None)

    @pl.when(g > 0)
    def _():
        qk_stage(kv0(1), rows0, y0)
        softmax_stage(buf_c, st0)
        qk_stage(kv0(2), rows0, x0)
        pv_stage(kv0(0), buf_c, st0)
        softmax_stage(y0, st0)
        run_steps(0, 3, (ntile * g - 2) // 2)
        transition(0, x0, y0)

    for j in range(1, ntile):
        qi, rows, stats, x, y, kv_of, step = tile_ctx(j)
        if j % 2 == 0:
            step(2, x, y)
            run_steps(j, 3, (ntile * g + j - 2) // 2)
        else:
            run_steps(j, 2, (ntile * g + j - 1) // 2)
        last_buf, other = (x, y) if j % 2 == 0 else (y, x)
        if j + 1 < ntile:
            transition(j, last_buf, other)
        else:
            qk_stage(jnp.minimum(qi + 1, n_tiles - 1), slice(0, tq), buf_c, diag=True, q_ref=qn_ref)
            pv_stage(kv_of(qi - 1), other, stats)
            softmax_stage(last_buf, stats)
            pv_stage(kv_of(qi), last_buf, stats)
            finalize(rows, stats)


def _attention(q, k, vt, *, tq, tk, ntile):
    s = q.shape[0]
    assert ntile % 2 == 0 and s % (ntile * tq) == 0, "sequence length must be a multiple of ntile query tiles"
    stat = pltpu.VMEM((SUBLANES, tq), F32)
    n_tiles = s // tq
    return pl.pallas_call(
        functools.partial(_attn_kernel, tq=tq, tk=tk, ntile=ntile, n_tiles=n_tiles),
        grid=(MLA_HEADS, s // (ntile * tq)),
        in_specs=[pl.BlockSpec((ntile * tq, QK_PAD), lambda h, i: (i, h)),
                  pl.BlockSpec((tq, QK_PAD), lambda h, i: (jnp.minimum(ntile * (i + 1), n_tiles - 1), h)),
                  pl.BlockSpec((s, QK_PAD), lambda h, i: (0, h)),
                  pl.BlockSpec((V_HEAD_DIM, s), lambda h, i: (h, 0))],
        out_specs=pl.BlockSpec((ntile * tq, V_HEAD_DIM), lambda h, i: (i, h)),
        out_shape=jax.ShapeDtypeStruct((s, MLA_HEADS * V_HEAD_DIM), BF16),
        scratch_shapes=[stat, pltpu.VMEM((ACC_ROWS, tq), F32), stat, pltpu.VMEM((ACC_ROWS, tq), F32),
                        pltpu.VMEM((tk, tq), F32), pltpu.VMEM((tk, tq), F32), pltpu.VMEM((tk, tq), F32),
                        stat, stat, stat,
                        pltpu.VMEM((tk, tq), BF16), pltpu.VMEM((tk, tq), BF16), pltpu.VMEM((tk, tq), BF16),
                        stat, stat, stat],
        compiler_params=pltpu.CompilerParams(dimension_semantics=("arbitrary", "arbitrary"),
                                             vmem_limit_bytes=VMEM_LIMIT),
        name="attention",
    )(q, q, k, vt)


def _ssd_out_kernel(xs_ref, bm_ref, cm_ref, dt_ref, gz_ref, alog_ref, dskip_ref, g_ref, e2_ref,
                    oa_ref, gza_ref, x_ref, w32_ref, mod_ref, lng_ref, lnb_ref,
                    o_ref, state, y_scr, w_ref, os_scr, *, rows, nheads, alpha, sub):
    i = pl.program_id(0)
    gw = SSM_STATE
    hw = state.shape[2]
    na = oa_ref.shape[1]

    @pl.when(i == 0)
    def _():
        state[...] = jnp.zeros(state.shape, F32)
        for r in range(0, w_ref.shape[0], sub):
            w_ref[r:r + sub, :] = w32_ref[r:r + sub, :].astype(BF16)

    mixes = []
    for r in range(0, rows, sub):
        oa = (oa_ref[r:r + sub, :].astype(F32) * gza_ref[r:r + sub, :].astype(F32)).astype(BF16)
        mixes.append(jnp.dot(oa, w_ref[0:na, :], preferred_element_type=F32))

    a_neg = -jnp.exp(alog_ref[...])
    r_i = lax.broadcasted_iota(jnp.int32, (CHUNK, CHUNK), 0)
    c_i = lax.broadcasted_iota(jnp.int32, (CHUNK, CHUNK), 1)
    tri = c_i <= r_i
    tri_bf = tri.astype(BF16)
    lane = lax.broadcasted_iota(jnp.int32, (CHUNK, LANES), 1)
    first_copy = lane < nheads
    head_lo = lane < SSM_HEAD_DIM
    e2 = e2_ref[...]

    def expand(v):
        hi = v.astype(BF16)
        lo = (v - hi.astype(F32)).astype(BF16)
        return jnp.dot(jnp.where(first_copy, hi, lo), e2, preferred_element_type=F32)

    for cidx in range(rows // CHUNK):
        sl = pl.ds(cidx * CHUNK, CHUNK)
        dt = dt_ref[sl, :]
        da = dt * (a_neg * LOG2E)
        a_cum = jnp.zeros((CHUNK, LANES), F32)
        for part in _split_bf16(da, 3):
            a_cum = a_cum + jnp.dot(tri_bf, part, preferred_element_type=F32)
        a_cum_t = a_cum.T
        a_last = a_cum[CHUNK - 1:CHUNK, :]
        ea = jnp.exp2(a_cum)
        dte = jnp.exp2(a_last - a_cum)

        xs = xs_ref[sl, :].astype(F32)
        xd = xs * expand(dt)
        ea_x = expand(ea)
        xdd = (xd * expand(dte)).astype(BF16)
        xd_bf = xd.astype(BF16)

        for g in range(SSM_GROUPS):
            bg = bm_ref[sl, g * gw:(g + 1) * gw]
            cg = cm_ref[sl, g * gw:(g + 1) * gw]
            cb = lax.dot_general(cg, bg, (((1,), (1,)), ((), ())), preferred_element_type=F32)
            prev = state[g]
            y_off = jnp.dot(cg, prev.astype(BF16), preferred_element_type=F32)
            new = lax.dot_general(bg, xdd[:, g * hw:(g + 1) * hw], (((0,), (0,)), ((), ())),
                                  preferred_element_type=F32)
            state[g] = prev * ea_x[CHUNK - 1:CHUNK, g * hw:(g + 1) * hw] + new
            y_scr[:, g * hw:(g + 1) * hw] = y_off * ea_x[:, g * hw:(g + 1) * hw]

            hpg = hw // SSM_HEAD_DIM
            for pair in range(hpg // 2):
                h0 = g * hpg + 2 * pair
                c0 = h0 * SSM_HEAD_DIM
                xp = xd_bf[:, c0:c0 + LANES]
                yp = jnp.zeros((CHUNK, LANES), F32)
                for k, keep in ((0, head_lo), (1, ~head_lo)):
                    h = h0 + k
                    seg = a_cum[:, h:h + 1] - a_cum_t[h:h + 1, :]
                    m_h = (cb * jnp.exp2(jnp.where(tri, seg, NEG_BIG))).astype(BF16)
                    yp = yp + jnp.dot(m_h, jnp.where(keep, xp, jnp.zeros_like(xp)),
                                      preferred_element_type=F32)
                y_scr[:, c0:c0 + LANES] = y_scr[:, c0:c0 + LANES] + yp

        y = y_scr[...] + xs * dskip_ref[...]
        hf = y * gz_ref[sl, :].astype(F32)
        for g in range(SSM_GROUPS):
            hg = hf[:, g * hw:(g + 1) * hw]
            ms = jnp.mean(hg * hg, axis=-1, keepdims=True)
            os_scr[sl, g * hw:(g + 1) * hw] = (hg * lax.rsqrt(ms + RMS_EPS)
                                                * g_ref[:, g * hw:(g + 1) * hw]).astype(os_scr.dtype)

    for n, mixed in enumerate(mixes):
        rws = slice(n * sub, (n + 1) * sub)
        mixed = mixed + jnp.dot(os_scr[rws, :], w_ref[na:, :], preferred_element_type=F32)
        y = alpha * x_ref[rws, :] + mod_ref[2:3, :] * mixed
        mu = jnp.mean(y, axis=-1, keepdims=True)
        yc = y - mu
        var = jnp.mean(yc * yc, axis=-1, keepdims=True)
        o_ref[rws, :] = yc * lax.rsqrt(var + LN_EPS) * lng_ref[...] + lnb_ref[...]


def _ssd_out(xs, bm, cm, dt, gzs, alog, dskip_x, norm_g, e2, oa, gza, x2, w_out, mod3, ln_g, ln_b,
             *, rows, sub, nheads, alpha):
    s, nx = xs.shape
    d = x2.shape[1]
    hw = nx // SSM_GROUPS
    row = lambda a: pl.BlockSpec((rows, a.shape[1]), lambda i: (i, 0))
    full = lambda a: pl.BlockSpec(a.shape, lambda i: (0,) * a.ndim)
    return pl.pallas_call(
        functools.partial(_ssd_out_kernel, rows=rows, nheads=nheads, alpha=alpha, sub=sub),
        grid=(s // rows,),
        in_specs=[row(xs), row(bm), row(cm), row(dt), row(gzs),
                  full(alog), full(dskip_x), full(norm_g), full(e2),
                  row(oa), row(gza), row(x2),
                  pl.BlockSpec(w_out.shape, lambda i: (0, 0), pipeline_mode=pl.Buffered(1)),
                  full(mod3), full(ln_g), full(ln_b)],
        out_specs=pl.BlockSpec((rows, d), lambda i: (i, 0)),
        out_shape=jax.ShapeDtypeStruct((s, d), F32),
        scratch_shapes=[pltpu.VMEM((SSM_GROUPS, SSM_STATE, hw), F32),
                        pltpu.VMEM((CHUNK, nx), F32),
                        pltpu.VMEM(w_out.shape, BF16),
                        pltpu.VMEM((rows, nx), BF16)],
        compiler_params=pltpu.CompilerParams(dimension_semantics=("arbitrary",),
                                             vmem_limit_bytes=VMEM_LIMIT),
        name="ssd_out",
    )(xs, bm, cm, dt, gzs, alog, dskip_x, norm_g, e2, oa, gza, x2, w_out, mod3, ln_g, ln_b)


def _pad_cols(a, width):
    return jnp.pad(a, ((0, 0), (0, width - a.shape[1])))


def _regroup_kernel(wt_ref, o_ref, *, segments):
    wt = wt_ref[...]
    parts = [jnp.zeros((b, wt.shape[1]), wt.dtype) if a is None else wt[a:b, :] for a, b in segments]
    o_ref[...] = jnp.concatenate(parts, axis=0).T.astype(o_ref.dtype)


def _regroup_columns(w, segments, width):
    assert all(b % SUBLANES == 0 and (a or 0) % SUBLANES == 0 for a, b in segments)
    rows = w.shape[0]
    br = 256
    return pl.pallas_call(
        functools.partial(_regroup_kernel, segments=segments),
        grid=(rows // br,),
        in_specs=[pl.BlockSpec((w.shape[1], br), lambda i: (0, i))],
        out_specs=pl.BlockSpec((br, width), lambda i: (i, 0)),
        out_shape=jax.ShapeDtypeStruct((rows, width), BF16),
        compiler_params=pltpu.CompilerParams(vmem_limit_bytes=VMEM_LIMIT),
        name="regroup_w_in",
    )(w.T)


def _layer(x2, c, pos_row, w_ada, b_ada, w_in, q_norm_g, w_qb, kv_norm_g, w_kvb,
           conv_w, conv_b, dt_bias, a_log, d_skip, ssm_norm_g, w_out, ln_g, ln_b, *, depth):
    s, d = x2.shape
    q_rank = q_norm_g.shape[0]
    kv_rank = kv_norm_g.shape[0]
    nheads = dt_bias.shape[0]
    nch = conv_w.shape[1]
    nv = MLA_HEADS * V_HEAD_DIM
    nx = nheads * SSM_HEAD_DIM
    half = QK_ROPE_DIM // 2

    mod = _adaln_mod(c.reshape(d, 1), w_ada, b_ada.reshape(1, -1))
    mod3 = mod.reshape(3, d)

    o_q, o_ckv = 0, q_rank
    o_kr = o_ckv + kv_rank
    o_za = o_kr + QK_ROPE_DIM
    o_xbc = o_za + nv
    o_dt = o_xbc + nch
    o_zs = o_dt + nheads
    dt_pad = LANES - 2 * nheads
    groups = [("q", [(o_q, o_ckv)]), ("ckv", [(o_ckv, o_kr)]),
              ("krope", [(o_kr, o_za), (o_kr + half, o_za), (o_kr, o_kr + half)]),
              ("za", [(o_za, o_xbc)]), ("xbc", [(o_xbc, o_dt)]), ("zs", [(o_zs, o_zs + nx)]),
              ("dt", [(o_dt, o_zs), (o_dt, o_zs), (None, dt_pad)])]
    off, cur, segments = {}, 0, []
    for name, segs in groups:
        assert cur % LANES == 0
        off[name] = cur
        for a, b in segs:
            cur += b if a is None else b - a
            if segments and a is not None and segments[-1][0] is not None and segments[-1][1] == a:
                segments[-1] = (segments[-1][0], b)
            else:
                segments.append((a, b))
    assert cur % LANES == 0
    w_in_p = _regroup_columns(w_in, tuple(segments), cur)

    w3 = w_qb.reshape(q_rank, MLA_HEADS, QK_NOPE_DIM + QK_ROPE_DIM)
    w_rope = w3[:, :, QK_NOPE_DIM:]
    w_qb_p = jnp.concatenate([w3, w_rope[:, :, half:], w_rope[:, :, :half]], axis=2)
    w_qb_p = w_qb_p.reshape(q_rank, MLA_HEADS * QK_PAD).astype(BF16)

    inv_freq = 1.0 / (ROPE_THETA ** (jnp.arange(half, dtype=F32) / half))
    rope_tab = jnp.broadcast_to(inv_freq[:, None], (half, IN_SUB))

    dtb = _pad_cols(jnp.concatenate([dt_bias, dt_bias]).reshape(1, -1), LANES)
    q_scale = (QK_NOPE_DIM + QK_ROPE_DIM) ** -0.5 * LOG2E

    wkv3 = w_kvb.reshape(kv_rank, MLA_HEADS, QK_NOPE_DIM + V_HEAD_DIM)
    w_k = wkv3[:, :, :QK_NOPE_DIM].reshape(kv_rank, MLA_HEADS * QK_NOPE_DIM).astype(BF16)
    w_vt = wkv3[:, :, QK_NOPE_DIM:].reshape(kv_rank, nv).T.astype(BF16)

    q, k, vt, gza, xs, bm, cm, dt, gzs = _in_proj(
        x2, mod3, pos_row, rope_tab, w_in_p, q_norm_g.reshape(1, -1), w_qb_p,
        kv_norm_g.reshape(1, -1), w_k, w_vt, conv_w, conv_b.reshape(1, -1), dtb,
        off=off, tm=IN_ROWS, sub=IN_SUB, q_scale=q_scale)

    o_attn = _attention(q, k, vt, tq=ATTN_TILE, tk=ATTN_TILE, ntile=ATTN_TILES_PER_STEP)

    alog = _pad_cols(jnp.concatenate([a_log, a_log]).reshape(1, -1), LANES)
    e_head = jnp.repeat(jnp.eye(nheads, dtype=BF16), SSM_HEAD_DIM, axis=1)
    e2 = jnp.pad(jnp.concatenate([e_head, e_head], axis=0), ((0, LANES - 2 * nheads), (0, 0)))
    dskip_x = jnp.repeat(d_skip, SSM_HEAD_DIM).reshape(1, -1)
    alpha = (2.0 * depth) ** 0.25
    return _ssd_out(xs, bm, cm, dt, gzs, alog, dskip_x, ssm_norm_g.reshape(1, -1), e2,
                    o_attn, gza, x2, w_out, mod3, ln_g.reshape(1, -1), ln_b.reshape(1, -1),
                    rows=SSD_ROWS, sub=OUT_SUB, nheads=nheads, alpha=alpha)


def kernel(x, c, positions, w_ada, b_ada, w_in, q_norm_g, w_qb, kv_norm_g, w_kvb, conv_w, conv_b,
           dt_bias, a_log, d_skip, ssm_norm_g, w_out, ln_g, ln_b):
    b, s, d = x.shape
    depth = w_in.shape[0]
    assert b == 1, "one sequence per call"
    h = x.reshape(s, d)
    pos_row = positions.reshape(1, s)
    for l in range(depth):
        h = _layer(h, c, pos_row, w_ada[l], b_ada[l], w_in[l], q_norm_g[l], w_qb[l], kv_norm_g[l],
                   w_kvb[l], conv_w[l], conv_b[l], dt_bias[l], a_log[l], d_skip[l], ssm_norm_g[l],
                   w_out[l], ln_g[l], ln_b[l], depth=depth)
    return h.reshape(b, s, d)
```

```python
import functools
import math

import jax
import jax.numpy as jnp
from jax import lax
from jax.experimental import pallas as pl
from jax.experimental.pallas import tpu as pltpu

F32 = jnp.float32
BF16 = jnp.bfloat16

MLA_HEADS = 8
QK_NOPE_DIM = 128
QK_ROPE_DIM = 64
V_HEAD_DIM = 128
ROPE_THETA = 10000.0
SSM_HEAD_DIM = 64
SSM_GROUPS = 2
SSM_STATE = 128
CHUNK = 128
RMS_EPS = 1e-6
LN_EPS = 1e-5

LANES = 128
SUBLANES = 8
MXU_DIM = 256
QK_PAD = MXU_DIM
VMEM_LIMIT = 56 * 1024 * 1024

IN_ROWS = 512
IN_SUB = 256
OUT_SUB = 256
ATTN_TILE = 1024
ATTN_TILES_PER_STEP = 2
SSD_ROWS = 512

LOG2E = 1.4426950408889634
NEG_BIG = -1e30


def _silu(z):
    h = 0.5 * z
    return h * jnp.tanh(h) + h


def _softplus(z):
    return jnp.maximum(z, 0.0) + jnp.log1p(jnp.exp(-jnp.abs(z)))


def _split_bf16(x, parts):
    out, rem = [], x
    for _ in range(parts):
        hi = rem.astype(BF16)
        out.append(hi)
        rem = rem - hi.astype(F32)
    return out


def _mod_kernel(c_ref, w_ref, b_ref, o_ref):
    o_ref[...] = jnp.sum(w_ref[...] * c_ref[...], axis=0, keepdims=True) + b_ref[...]


def _adaln_mod(c_col, w_ada, b_ada):
    d, n = w_ada.shape
    bn = 1024
    return pl.pallas_call(
        _mod_kernel,
        grid=(n // bn,),
        in_specs=[pl.BlockSpec((d, 1), lambda j: (0, 0)),
                  pl.BlockSpec((d, bn), lambda j: (0, j)),
                  pl.BlockSpec((1, bn), lambda j: (0, j))],
        out_specs=pl.BlockSpec((1, bn), lambda j: (0, j)),
        out_shape=jax.ShapeDtypeStruct((1, n), F32),
        name="adaln_mod",
    )(c_col, w_ada, b_ada)


def _in_kernel(x_ref, mod_ref, pos_ref, rope_ref, w_in_ref, qg_ref, w_qb_ref, kvg_ref, w_k_ref, w_vt_ref,
               conv_w_ref, conv_b_ref, dtb_ref,
               q_out, k_out, vt_out, gza_out, xs_out, bm_out, cm_out, dt_out, gzs_out,
               xbuf, *, tm, sub, off, q_scale):
    i = pl.program_id(0)
    shift = mod_ref[0:1, :]
    scale1 = 1.0 + mod_ref[1:2, :]
    lane = lax.broadcasted_iota(jnp.int32, (sub, LANES), 1)
    low_half = lane < QK_ROPE_DIM
    nconv, nch = conv_w_ref.shape
    nx = xs_out.shape[1]
    nb = bm_out.shape[1]

    @pl.when(i == 0)
    def _():
        xbuf[0:SUBLANES, :] = jnp.zeros((SUBLANES, nch), F32)

    css = []
    for h in range(tm // sub):
        ang_t = pos_ref[:, h * sub:(h + 1) * sub].astype(F32) * rope_ref[...]
        cos_t, sin_t = jnp.cos(ang_t), jnp.sin(ang_t)
        css.append(jnp.concatenate([cos_t, cos_t, -sin_t, sin_t], axis=0).T)

    projs = []
    for h in range(tm // sub):
        u = x_ref[h * sub:(h + 1) * sub, :] * scale1 + shift
        projs.append(jnp.dot(u.astype(BF16), w_in_ref[...], preferred_element_type=F32))

    for h, (proj, cs) in enumerate(zip(projs, css)):
        rows = slice(h * sub, (h + 1) * sub)

        def rope(rr, cs=cs):
            t = rr * cs
            return jnp.where(low_half, t + pltpu.roll(t, QK_ROPE_DIM, axis=1), 0.0)

        q_lat = proj[:, off["q"]:off["q"] + qg_ref.shape[1]]
        qn = q_lat * lax.rsqrt(jnp.mean(q_lat * q_lat, axis=-1, keepdims=True) + RMS_EPS) * qg_ref[...]
        qf = jnp.dot(qn.astype(BF16), w_qb_ref[...], preferred_element_type=F32)
        for hd in range(MLA_HEADS):
            c0 = hd * QK_PAD
            q_out[rows, c0:c0 + QK_NOPE_DIM] = (qf[:, c0:c0 + QK_NOPE_DIM] * q_scale).astype(BF16)
            q_out[rows, c0 + QK_NOPE_DIM:c0 + QK_PAD] = (
                rope(qf[:, c0 + QK_NOPE_DIM:c0 + QK_PAD]) * q_scale).astype(BF16)

        c_kv = proj[:, off["ckv"]:off["ckv"] + kvg_ref.shape[1]]
        ckvn = c_kv * lax.rsqrt(jnp.mean(c_kv * c_kv, axis=-1, keepdims=True) + RMS_EPS) * kvg_ref[...]
        ckvn_bf = ckvn.astype(BF16)
        kf = jnp.dot(ckvn_bf, w_k_ref[...], preferred_element_type=F32)
        vt_out[:, rows] = lax.dot_general(w_vt_ref[...], ckvn_bf, (((1,), (1,)), ((), ())),
                                          preferred_element_type=F32).astype(BF16)
        k_rope = rope(proj[:, off["krope"]:off["krope"] + LANES]).astype(BF16)
        for hd in range(MLA_HEADS):
            k_out[rows, hd * QK_PAD:hd * QK_PAD + QK_NOPE_DIM] = (
                kf[:, hd * QK_NOPE_DIM:(hd + 1) * QK_NOPE_DIM].astype(BF16))
            k_out[rows, hd * QK_PAD + QK_NOPE_DIM:(hd + 1) * QK_PAD] = k_rope

        gza_out[rows, :] = _silu(proj[:, off["za"]:off["za"] + gza_out.shape[1]]).astype(BF16)
        gzs_out[rows, :] = _silu(proj[:, off["zs"]:off["zs"] + gzs_out.shape[1]]).astype(BF16)
        dt_out[rows, :] = _softplus(proj[:, off["dt"]:off["dt"] + LANES] + dtb_ref[...])

        r1 = SUBLANES + h * sub
        xbuf[r1:r1 + sub, :] = proj[:, off["xbc"]:off["xbc"] + nch]
        xfull = xbuf[r1 - SUBLANES:r1 + sub, :]
        acc = conv_b_ref[...] + conv_w_ref[nconv - 1:nconv, :] * xfull[SUBLANES:, :]
        for k in range(1, nconv):
            shifted = pltpu.roll(xfull, k, axis=0)[SUBLANES:, :]
            acc = acc + conv_w_ref[nconv - 1 - k:nconv - k, :] * shifted
        xc = _silu(acc)
        xs_out[rows, :] = xc[:, :nx].astype(BF16)
        bm_out[rows, :] = xc[:, nx:nx + nb].astype(BF16)
        cm_out[rows, :] = xc[:, nx + nb:nx + 2 * nb].astype(BF16)

    xbuf[0:SUBLANES, :] = xbuf[tm:tm + SUBLANES, :]


def _in_proj(x2, mod3, pos_row, rope_tab, w_in_p, qg, w_qb_p, kvg, w_k, w_vt, conv_w, conv_b, dtb,
             *, off, tm, sub, q_scale):
    s, d = x2.shape
    nq = MLA_HEADS * QK_PAD
    nv = MLA_HEADS * V_HEAD_DIM
    nch = conv_w.shape[1]
    nbc = SSM_GROUPS * SSM_STATE
    nx = nch - 2 * nbc
    row = lambda w: pl.BlockSpec((tm, w), lambda i: (i, 0))
    full = lambda a: pl.BlockSpec(a.shape, lambda i: (0,) * a.ndim, pipeline_mode=pl.Buffered(1))
    outs = [
        jax.ShapeDtypeStruct((s, nq), BF16),
        jax.ShapeDtypeStruct((s, nq), BF16),
        jax.ShapeDtypeStruct((nv, s), BF16),
        jax.ShapeDtypeStruct((s, nv), BF16),
        jax.ShapeDtypeStruct((s, nx), BF16),
        jax.ShapeDtypeStruct((s, nbc), BF16),
        jax.ShapeDtypeStruct((s, nbc), BF16),
        jax.ShapeDtypeStruct((s, LANES), F32),
        jax.ShapeDtypeStruct((s, nx), BF16),
    ]
    return pl.pallas_call(
        functools.partial(_in_kernel, tm=tm, sub=sub, off=off, q_scale=q_scale),
        grid=(s // tm,),
        in_specs=[row(d), full(mod3), pl.BlockSpec((1, tm), lambda i: (0, i)), full(rope_tab),
                  full(w_in_p), full(qg), full(w_qb_p), full(kvg), full(w_k), full(w_vt),
                  full(conv_w), full(conv_b), full(dtb)],
        out_specs=[pl.BlockSpec((nv, tm), lambda i: (0, i)) if n == 2 else row(o.shape[1])
                   for n, o in enumerate(outs)],
        out_shape=outs,
        scratch_shapes=[pltpu.VMEM((tm + 2 * SUBLANES, nch), F32)],
        compiler_params=pltpu.CompilerParams(dimension_semantics=("arbitrary",),
                                             vmem_limit_bytes=VMEM_LIMIT),
        name="in_proj",
    )(x2, mod3, pos_row, rope_tab, w_in_p, qg, w_qb_p, kvg, w_k, w_vt, conv_w, conv_b, dtb)


def _sublane_allmax(x):
    shift = SUBLANES // 2
    while shift:
        x = jnp.maximum(x, pltpu.roll(x, shift, axis=0))
        shift //= 2
    return x


def _sublane_allsum(x):
    shift = SUBLANES // 2
    while shift:
        x = x + pltpu.roll(x, shift, axis=0)
        shift //= 2
    return x


BF16_ROWS = 16
ACC_ROWS = V_HEAD_DIM + BF16_ROWS
EXP_ROWS = 32


def _attn_kernel(q_ref, qn_ref, k_ref, vt_ref, o_ref, m_a, acc_a, m_b, acc_b,
                 s_a, s_b, s_c, cmax_a, cmax_b, cmax_c, p_a, p_b, p_c, al_a, al_b, al_c,
                 *, tq, tk, ntile, n_tiles):
    assert tq == tk
    g = pl.program_id(1)
    nsub = tk // SUBLANES
    nacc = ACC_ROWS // SUBLANES
    buf_a = (s_a, cmax_a, p_a, al_a)
    buf_b = (s_b, cmax_b, p_b, al_b)
    buf_c = (s_c, cmax_c, p_c, al_c)
    ones_rows = jnp.ones((ACC_ROWS - V_HEAD_DIM, tk), BF16)

    def reset(stats):
        m_scr, acc_scr = stats
        m_scr[...] = jnp.full(m_scr.shape, NEG_BIG, F32)
        acc_scr[...] = jnp.zeros(acc_scr.shape, F32)

    def qk_stage(kv, q_rows, buf, diag=False, q_ref=q_ref, between=None):
        s_out, cmax_out = buf[0], buf[1]
        start = pl.multiple_of(kv * tk, tk)
        nt = (((1,), (1,)), ((), ()))
        colmax = lambda x: _sublane_allmax(jnp.max(x.reshape(x.shape[0] // SUBLANES, SUBLANES, x.shape[1]), axis=0))
        if not diag:
            st = lax.dot_general(k_ref[pl.ds(start, tk), :], q_ref[q_rows, :], nt,
                                 preferred_element_type=F32)
            s_out[...] = st
            cmax_out[...] = colmax(st)
            return
        hk = tk // 2
        q_lo = q_rows.start
        mask = (lax.broadcasted_iota(jnp.int32, (hk, tq), 0) <= lax.broadcasted_iota(jnp.int32, (hk, tq), 1))
        top = lax.dot_general(k_ref[pl.ds(start, hk), :], q_ref[q_rows, :], nt, preferred_element_type=F32)
        top = jnp.where(mask, top, NEG_BIG)
        s_out[0:hk, :] = top
        s_out[hk:tk, 0:hk] = jnp.full((tk - hk, hk), NEG_BIG, F32)
        cmax_out[...] = colmax(top)
        if between is not None:
            between()
        bot = lax.dot_general(k_ref[pl.ds(start + hk, hk), :], q_ref[q_lo + hk:q_lo + tq, :], nt,
                              preferred_element_type=F32)
        bot = jnp.where(mask[:, :tq - hk], bot, NEG_BIG)
        s_out[hk:tk, hk:tq] = bot
        cmax_out[:, hk:tq] = jnp.maximum(cmax_out[:, hk:tq], colmax(bot))

    def softmax_stage(buf, stats):
        s_in, cmax_in, p_out, al_out = buf
        m_scr = stats[0]
        m_old = m_scr[...]
        m_new = jnp.maximum(m_old, cmax_in[...])
        al_out[...] = jnp.exp2(m_old - m_new)
        m_scr[...] = m_new
        for c in range(0, tq, MXU_DIM):
            m_c = m_new[:, c:c + MXU_DIM][None]
            for r in range(0, tk, EXP_ROWS):
                sc = s_in[r:r + EXP_ROWS, c:c + MXU_DIM].reshape(EXP_ROWS // SUBLANES, SUBLANES, MXU_DIM)
                p_out[r:r + EXP_ROWS, c:c + MXU_DIM] = (
                    jnp.exp2(sc - m_c).reshape(EXP_ROWS, MXU_DIM).astype(BF16))

    def pv_stage(kv, buf, stats):
        p_in, al_in = buf[2], buf[3]
        acc_scr = stats[1]
        start = pl.multiple_of(jnp.maximum(kv, 0) * tk, tk)
        vt_aug = jnp.concatenate([vt_ref[:, pl.ds(start, tk)], ones_rows], axis=0)
        pv = jnp.dot(vt_aug, p_in[...], preferred_element_type=F32)
        acc3 = acc_scr[...].reshape(nacc, SUBLANES, tq) * al_in[...][None]
        acc_scr[...] = acc3.reshape(ACC_ROWS, tq) + pv

    def finalize(q_rows, stats):
        acc_scr = stats[1]
        inv_l = 1.0 / acc_scr[V_HEAD_DIM:V_HEAD_DIM + SUBLANES, :]
        o_t = acc_scr[0:V_HEAD_DIM, :].reshape(V_HEAD_DIM // SUBLANES, SUBLANES, tq) * inv_l[None]
        o_ref[q_rows, :] = o_t.reshape(V_HEAD_DIM, tq).T.astype(o_ref.dtype)

    def make_step(qi, q_rows, stats):
        kv_of = lambda pos: jnp.where(pos == 0, qi, pos - 1)

        def step(pos, buf, other):
            qk_stage(kv_of(pos), q_rows, buf)
            pv_stage(kv_of(pos - 2), buf, stats)
            softmax_stage(other, stats)

        return kv_of, step

    def run_pairs(lo, hi, pair):
        def body(pp, carry):
            pair(pp)
            return carry

        lax.fori_loop(lo, hi, body, 0)

    bufs = (buf_a, buf_b)
    stats2 = ((m_a, acc_a), (m_b, acc_b))
    start = [0]
    for j in range(ntile - 1):
        last = start[j] if j % 2 == 0 else 1 - start[j]
        start.append(1 - last)

    def tile_ctx(j):
        qi = ntile * g + j
        rows = slice(j * tq, (j + 1) * tq)
        stats = stats2[j % 2]
        x, y = bufs[start[j]], bufs[1 - start[j]]
        kv_of, step = make_step(qi, rows, stats)
        return qi, rows, stats, x, y, kv_of, step

    def run_steps(j, first_pos, n_pairs):
        _, _, _, x, y, _, step = tile_ctx(j)
        even_buf, odd_buf = x, y

        def pair(pp):
            p0 = first_pos + 2 * pp
            if first_pos % 2 == 0:
                step(p0, even_buf, odd_buf)
                step(p0 + 1, odd_buf, even_buf)
            else:
                step(p0, odd_buf, even_buf)
                step(p0 + 1, even_buf, odd_buf)

        run_pairs(0, n_pairs, pair)

    def transition(j, last_buf, other):
        qi, rows, stats, _, _, kv_of, _ = tile_ctx(j)
        qi_n, rows_n, st_n, x_n, y_n, kv_n, _ = tile_ctx(j + 1)
        reset(st_n)
        qk_stage(qi_n, rows_n, x_n, diag=True)
        if other is not None:
            assert x_n is other
            pv_stage(kv_of(qi - 1), other, stats)
        softmax_stage(last_buf, stats)
        qk_stage(kv_n(1), rows_n, y_n)
        pv_stage(kv_of(qi), last_buf, stats)
        softmax_stage(x_n, st_n)
        finalize(rows, stats)

    qi0, rows0, st0, x0, y0, kv0, step0 = tile_ctx(0)
    reset(st0)

    @pl.when(g == 0)
    def _():
        qk_stage(qi0, rows0, buf_c, diag=True)
        transition(0, buf_c, None)

    @pl.when(g > 0)
    def _():
        qk_stage(kv0(1), rows0, y0)
        softmax_stage(buf_c, st0)
        qk_stage(kv0(2), rows0, x0)
        pv_stage(kv0(0), buf_c, st0)
        softmax_stage(y0, st0)
        run_steps(0, 3, (ntile * g - 2) // 2)
        transition(0, x0, y0)

    for j in range(1, ntile):
        qi, rows, stats, x, y, kv_of, step = tile_ctx(j)
        if j % 2 == 0:
            step(2, x, y)
            run_steps(j, 3, (ntile * g + j - 2) // 2)
        else:
            run_steps(j, 2, (ntile * g + j - 1) // 2)
        last_buf, other = (x, y) if j % 2 == 0 else (y, x)
        if j + 1 < ntile:
            transition(j, last_buf, other)
        else:
            def first_pv(kv_of=kv_of, qi=qi, other=other, last_buf=last_buf, stats=stats):
                pv_stage(kv_of(qi - 1), other, stats)
                softmax_stage(last_buf, stats)

            qk_stage(jnp.minimum(qi + 1, n_tiles - 1), slice(0, tq), buf_c, diag=True, q_ref=qn_ref,
                     between=first_pv)
            pv_stage(kv_of(qi), last_buf, stats)
            finalize(rows, stats)


def _attention(q, k, vt, *, tq, tk, ntile):
    s = q.shape[0]
    assert ntile % 2 == 0 and s % (ntile * tq) == 0, "sequence length must be a multiple of ntile query tiles"
    stat = pltpu.VMEM((SUBLANES, tq), F32)
    n_tiles = s // tq
    return pl.pallas_call(
        functools.partial(_attn_kernel, tq=tq, tk=tk, ntile=ntile, n_tiles=n_tiles),
        grid=(MLA_HEADS, s // (ntile * tq)),
        in_specs=[pl.BlockSpec((ntile * tq, QK_PAD), lambda h, i: (i, h)),
                  pl.BlockSpec((tq, QK_PAD), lambda h, i: (jnp.minimum(ntile * (i + 1), n_tiles - 1), h)),
                  pl.BlockSpec((s, QK_PAD), lambda h, i: (0, h)),
                  pl.BlockSpec((V_HEAD_DIM, s), lambda h, i: (h, 0))],
        out_specs=pl.BlockSpec((ntile * tq, V_HEAD_DIM), lambda h, i: (i, h)),
        out_shape=jax.ShapeDtypeStruct((s, MLA_HEADS * V_HEAD_DIM), BF16),
        scratch_shapes=[stat, pltpu.VMEM((ACC_ROWS, tq), F32), stat, pltpu.VMEM((ACC_ROWS, tq), F32),
                        pltpu.VMEM((tk, tq), F32), pltpu.VMEM((tk, tq), F32), pltpu.VMEM((tk, tq), F32),
                        stat, stat, stat,
                        pltpu.VMEM((tk, tq), BF16), pltpu.VMEM((tk, tq), BF16), pltpu.VMEM((tk, tq), BF16),
                        stat, stat, stat],
        compiler_params=pltpu.CompilerParams(dimension_semantics=("arbitrary", "arbitrary"),
                                             vmem_limit_bytes=VMEM_LIMIT),
        name="attention",
    )(q, q, k, vt)


def _ssd_out_kernel(xs_ref, bm_ref, cm_ref, dt_ref, gz_ref, alog_ref, dskip_ref, g_ref, e2_ref,
                    oa_ref, gza_ref, x_ref, w32_ref, mod_ref, lng_ref, lnb_ref,
                    o_ref, state, y_scr, w_ref, os_scr, *, rows, nheads, alpha, sub):
    i = pl.program_id(0)
    gw = SSM_STATE
    hw = state.shape[2]
    na = oa_ref.shape[1]

    @pl.when(i == 0)
    def _():
        state[...] = jnp.zeros(state.shape, F32)
        for r in range(0, w_ref.shape[0], sub):
            w_ref[r:r + sub, :] = w32_ref[r:r + sub, :].astype(BF16)

    mixes = []
    for r in range(0, rows, sub):
        oa = (oa_ref[r:r + sub, :].astype(F32) * gza_ref[r:r + sub, :].astype(F32)).astype(BF16)
        mixes.append(jnp.dot(oa, w_ref[0:na, :], preferred_element_type=F32))

    a_neg = -jnp.exp(alog_ref[...])
    r_i = lax.broadcasted_iota(jnp.int32, (CHUNK, CHUNK), 0)
    c_i = lax.broadcasted_iota(jnp.int32, (CHUNK, CHUNK), 1)
    tri = c_i <= r_i
    tri_bf = tri.astype(BF16)
    lane = lax.broadcasted_iota(jnp.int32, (CHUNK, LANES), 1)
    first_copy = lane < nheads
    head_lo = lane < SSM_HEAD_DIM
    e2 = e2_ref[...]

    def expand(v):
        hi = v.astype(BF16)
        lo = (v - hi.astype(F32)).astype(BF16)
        return jnp.dot(jnp.where(first_copy, hi, lo), e2, preferred_element_type=F32)

    for cidx in range(rows // CHUNK):
        sl = pl.ds(cidx * CHUNK, CHUNK)
        dt = dt_ref[sl, :]
        da = dt * (a_neg * LOG2E)
        a_cum = jnp.zeros((CHUNK, LANES), F32)
        for part in _split_bf16(da, 3):
            a_cum = a_cum + jnp.dot(tri_bf, part, preferred_element_type=F32)
        a_cum_t = a_cum.T
        a_last = a_cum[CHUNK - 1:CHUNK, :]
        ea = jnp.exp2(a_cum)
        dte = jnp.exp2(a_last - a_cum)

        xs = xs_ref[sl, :].astype(F32)
        xd = xs * expand(dt)
        ea_x = expand(ea)
        xdd = (xd * expand(dte)).astype(BF16)
        xd_bf = xd.astype(BF16)

        for g in range(SSM_GROUPS):
            bg = bm_ref[sl, g * gw:(g + 1) * gw]
            cg = cm_ref[sl, g * gw:(g + 1) * gw]
            cb = lax.dot_general(cg, bg, (((1,), (1,)), ((), ())), preferred_element_type=F32)
            prev = state[g]
            y_off = jnp.dot(cg, prev.astype(BF16), preferred_element_type=F32)
            new = lax.dot_general(bg, xdd[:, g * hw:(g + 1) * hw], (((0,), (0,)), ((), ())),
                                  preferred_element_type=F32)
            state[g] = prev * ea_x[CHUNK - 1:CHUNK, g * hw:(g + 1) * hw] + new
            y_scr[:, g * hw:(g + 1) * hw] = y_off * ea_x[:, g * hw:(g + 1) * hw]

            hpg = hw // SSM_HEAD_DIM
            for pair in range(hpg // 2):
                h0 = g * hpg + 2 * pair
                c0 = h0 * SSM_HEAD_DIM
                xp = xd_bf[:, c0:c0 + LANES]
                yp = jnp.zeros((CHUNK, LANES), F32)
                for k, keep in ((0, head_lo), (1, ~head_lo)):
                    h = h0 + k
                    seg = a_cum[:, h:h + 1] - a_cum_t[h:h + 1, :]
                    m_h = (cb * jnp.exp2(jnp.where(tri, seg, NEG_BIG))).astype(BF16)
                    yp = yp + jnp.dot(m_h, jnp.where(keep, xp, jnp.zeros_like(xp)),
                                      preferred_element_type=F32)
                y_scr[:, c0:c0 + LANES] = y_scr[:, c0:c0 + LANES] + yp

        y = y_scr[...] + xs * dskip_ref[...]
        hf = y * gz_ref[sl, :].astype(F32)
        for g in range(SSM_GROUPS):
            hg = hf[:, g * hw:(g + 1) * hw]
            ms = jnp.mean(hg * hg, axis=-1, keepdims=True)
            os_scr[sl, g * hw:(g + 1) * hw] = (hg * lax.rsqrt(ms + RMS_EPS)
                                                * g_ref[:, g * hw:(g + 1) * hw]).astype(os_scr.dtype)

    for n, mixed in enumerate(mixes):
        rws = slice(n * sub, (n + 1) * sub)
        mixed = mixed + jnp.dot(os_scr[rws, :], w_ref[na:, :], preferred_element_type=F32)
        y = alpha * x_ref[rws, :] + mod_ref[2:3, :] * mixed
        mu = jnp.mean(y, axis=-1, keepdims=True)
        yc = y - mu
        var = jnp.mean(yc * yc, axis=-1, keepdims=True)
        o_ref[rws, :] = yc * lax.rsqrt(var + LN_EPS) * lng_ref[...] + lnb_ref[...]


def _ssd_out(xs, bm, cm, dt, gzs, alog, dskip_x, norm_g, e2, oa, gza, x2, w_out, mod3, ln_g, ln_b,
             *, rows, sub, nheads, alpha):
    s, nx = xs.shape
    d = x2.shape[1]
    hw = nx // SSM_GROUPS
    row = lambda a: pl.BlockSpec((rows, a.shape[1]), lambda i: (i, 0))
    full = lambda a: pl.BlockSpec(a.shape, lambda i: (0,) * a.ndim)
    return pl.pallas_call(
        functools.partial(_ssd_out_kernel, rows=rows, nheads=nheads, alpha=alpha, sub=sub),
        grid=(s // rows,),
        in_specs=[row(xs), row(bm), row(cm), row(dt), row(gzs),
                  full(alog), full(dskip_x), full(norm_g), full(e2),
                  row(oa), row(gza), row(x2),
                  pl.BlockSpec(w_out.shape, lambda i: (0, 0), pipeline_mode=pl.Buffered(1)),
                  full(mod3), full(ln_g), full(ln_b)],
        out_specs=pl.BlockSpec((rows, d), lambda i: (i, 0)),
        out_shape=jax.ShapeDtypeStruct((s, d), F32),
        scratch_shapes=[pltpu.VMEM((SSM_GROUPS, SSM_STATE, hw), F32),
                        pltpu.VMEM((CHUNK, nx), F32),
                        pltpu.VMEM(w_out.shape, BF16),
                        pltpu.VMEM((rows, nx), BF16)],
        compiler_params=pltpu.CompilerParams(dimension_semantics=("arbitrary",),
                                             vmem_limit_bytes=VMEM_LIMIT),
        name="ssd_out",
    )(xs, bm, cm, dt, gzs, alog, dskip_x, norm_g, e2, oa, gza, x2, w_out, mod3, ln_g, ln_b)


def _pad_cols(a, width):
    return jnp.pad(a, ((0, 0), (0, width - a.shape[1])))


def _regroup_kernel(wt_ref, o_ref, *, segments):
    wt = wt_ref[...]
    parts = [jnp.zeros((b, wt.shape[1]), wt.dtype) if a is None else wt[a:b, :] for a, b in segments]
    o_ref[...] = jnp.concatenate(parts, axis=0).T.astype(o_ref.dtype)


def _regroup_columns(w, segments, width):
    assert all(b % SUBLANES == 0 and (a or 0) % SUBLANES == 0 for a, b in segments)
    rows = w.shape[0]
    br = 256
    return pl.pallas_call(
        functools.partial(_regroup_kernel, segments=segments),
        grid=(rows // br,),
        in_specs=[pl.BlockSpec((w.shape[1], br), lambda i: (0, i))],
        out_specs=pl.BlockSpec((br, width), lambda i: (i, 0)),
        out_shape=jax.ShapeDtypeStruct((rows, width), BF16),
        compiler_params=pltpu.CompilerParams(vmem_limit_bytes=VMEM_LIMIT),
        name="regroup_w_in",
    )(w.T)


def _layer(x2, c, pos_row, w_ada, b_ada, w_in, q_norm_g, w_qb, kv_norm_g, w_kvb,
           conv_w, conv_b, dt_bias, a_log, d_skip, ssm_norm_g, w_out, ln_g, ln_b, *, depth):
    s, d = x2.shape
    q_rank = q_norm_g.shape[0]
    kv_rank = kv_norm_g.shape[0]
    nheads = dt_bias.shape[0]
    nch = conv_w.shape[1]
    nv = MLA_HEADS * V_HEAD_DIM
    nx = nheads * SSM_HEAD_DIM
    half = QK_ROPE_DIM // 2

    mod = _adaln_mod(c.reshape(d, 1), w_ada, b_ada.reshape(1, -1))
    mod3 = mod.reshape(3, d)

    o_q, o_ckv = 0, q_rank
    o_kr = o_ckv + kv_rank
    o_za = o_kr + QK_ROPE_DIM
    o_xbc = o_za + nv
    o_dt = o_xbc + nch
    o_zs = o_dt + nheads
    dt_pad = LANES - 2 * nheads
    groups = [("q", [(o_q, o_ckv)]), ("ckv", [(o_ckv, o_kr)]),
              ("krope", [(o_kr, o_za), (o_kr + half, o_za), (o_kr, o_kr + half)]),
              ("za", [(o_za, o_xbc)]), ("xbc", [(o_xbc, o_dt)]), ("zs", [(o_zs, o_zs + nx)]),
              ("dt", [(o_dt, o_zs), (o_dt, o_zs), (None, dt_pad)])]
    off, cur, segments = {}, 0, []
    for name, segs in groups:
        assert cur % LANES == 0
        off[name] = cur
        for a, b in segs:
            cur += b if a is None else b - a
            if segments and a is not None and segments[-1][0] is not None and segments[-1][1] == a:
                segments[-1] = (segments[-1][0], b)
            else:
                segments.append((a, b))
    assert cur % LANES == 0
    w_in_p = _regroup_columns(w_in, tuple(segments), cur)

    w3 = w_qb.reshape(q_rank, MLA_HEADS, QK_NOPE_DIM + QK_ROPE_DIM)
    w_rope = w3[:, :, QK_NOPE_DIM:]
    w_qb_p = jnp.concatenate([w3, w_rope[:, :, half:], w_rope[:, :, :half]], axis=2)
    w_qb_p = w_qb_p.reshape(q_rank, MLA_HEADS * QK_PAD).astype(BF16)

    inv_freq = 1.0 / (ROPE_THETA ** (jnp.arange(half, dtype=F32) / half))
    rope_tab = jnp.broadcast_to(inv_freq[:, None], (half, IN_SUB))

    dtb = _pad_cols(jnp.concatenate([dt_bias, dt_bias]).reshape(1, -1), LANES)
    q_scale = (QK_NOPE_DIM + QK_ROPE_DIM) ** -0.5 * LOG2E

    wkv3 = w_kvb.reshape(kv_rank, MLA_HEADS, QK_NOPE_DIM + V_HEAD_DIM)
    w_k = wkv3[:, :, :QK_NOPE_DIM].reshape(kv_rank, MLA_HEADS * QK_NOPE_DIM).astype(BF16)
    w_vt = wkv3[:, :, QK_NOPE_DIM:].reshape(kv_rank, nv).T.astype(BF16)

    q, k, vt, gza, xs, bm, cm, dt, gzs = _in_proj(
        x2, mod3, pos_row, rope_tab, w_in_p, q_norm_g.reshape(1, -1), w_qb_p,
        kv_norm_g.reshape(1, -1), w_k, w_vt, conv_w, conv_b.reshape(1, -1), dtb,
        off=off, tm=IN_ROWS, sub=IN_SUB, q_scale=q_scale)

    o_attn = _attention(q, k, vt, tq=ATTN_TILE, tk=ATTN_TILE, ntile=ATTN_TILES_PER_STEP)

    alog = _pad_cols(jnp.concatenate([a_log, a_log]).reshape(1, -1), LANES)
    e_head = jnp.repeat(jnp.eye(nheads, dtype=BF16), SSM_HEAD_DIM, axis=1)
    e2 = jnp.pad(jnp.concatenate([e_head, e_head], axis=0), ((0, LANES - 2 * nheads), (0, 0)))
    dskip_x = jnp.repeat(d_skip, SSM_HEAD_DIM).reshape(1, -1)
    alpha = (2.0 * depth) ** 0.25
    return _ssd_out(xs, bm, cm, dt, gzs, alog, dskip_x, ssm_norm_g.reshape(1, -1), e2,
                    o_attn, gza, x2, w_out, mod3, ln_g.reshape(1, -1), ln_b.reshape(1, -1),
                    rows=SSD_ROWS, sub=OUT_SUB, nheads=nheads, alpha=alpha)


def kernel(x, c, positions, w_ada, b_ada, w_in, q_norm_g, w_qb, kv_norm_g, w_kvb, conv_w, conv_b,
           dt_bias, a_log, d_skip, ssm_norm_g, w_out, ln_g, ln_b):
    b, s, d = x.shape
    depth = w_in.shape[0]
    assert b == 1, "one sequence per call"
    h = x.reshape(s, d)
    pos_row = positions.reshape(1, s)
    for l in range(depth):
        h = _layer(h, c, pos_row, w_ada[l], b_ada[l], w_in[l], q_norm_g[l], w_qb[l], kv_norm_g[l],
                   w_kvb[l], conv_w[l], conv_b[l], dt_bias[l], a_log[l], d_skip[l], ssm_norm_g[l],
                   w_out[l], ln_g[l], ln_b[l], depth=depth)
    return h.reshape(b, s, d)
```

```python
import functools
import math

import jax
import jax.numpy as jnp
from jax import lax
from jax.experimental import pallas as pl
from jax.experimental.pallas import tpu as pltpu

F32 = jnp.float32
BF16 = jnp.bfloat16

MLA_HEADS = 8
QK_NOPE_DIM = 128
QK_ROPE_DIM = 64
V_HEAD_DIM = 128
ROPE_THETA = 10000.0
SSM_HEAD_DIM = 64
SSM_GROUPS = 2
SSM_STATE = 128
CHUNK = 128
RMS_EPS = 1e-6
LN_EPS = 1e-5

LANES = 128
SUBLANES = 8
MXU_DIM = 256
QK_PAD = MXU_DIM
VMEM_LIMIT = 56 * 1024 * 1024

IN_ROWS = 512
IN_SUB = 256
OUT_SUB = 256
ATTN_TILE = 1024
ATTN_TILES_PER_STEP = 2
SSD_ROWS = 512

LOG2E = 1.4426950408889634
NEG_BIG = -1e30


def _silu(z):
    h = 0.5 * z
    return h * jnp.tanh(h) + h


def _softplus(z):
    return jnp.maximum(z, 0.0) + jnp.log1p(jnp.exp(-jnp.abs(z)))


def _split_bf16(x, parts):
    out, rem = [], x
    for _ in range(parts):
        hi = rem.astype(BF16)
        out.append(hi)
        rem = rem - hi.astype(F32)
    return out


def _mod_kernel(c_ref, w_ref, b_ref, o_ref):
    o_ref[...] = jnp.sum(w_ref[...] * c_ref[...], axis=0, keepdims=True) + b_ref[...]


def _adaln_mod(c_col, w_ada, b_ada):
    d, n = w_ada.shape
    bn = 1024
    return pl.pallas_call(
        _mod_kernel,
        grid=(n // bn,),
        in_specs=[pl.BlockSpec((d, 1), lambda j: (0, 0)),
                  pl.BlockSpec((d, bn), lambda j: (0, j)),
                  pl.BlockSpec((1, bn), lambda j: (0, j))],
        out_specs=pl.BlockSpec((1, bn), lambda j: (0, j)),
        out_shape=jax.ShapeDtypeStruct((1, n), F32),
        name="adaln_mod",
    )(c_col, w_ada, b_ada)


def _in_kernel(x_ref, mod_ref, pos_ref, rope_ref, w_in_ref, qg_ref, w_qb_ref, kvg_ref, w_k_ref, w_vt_ref,
               conv_w_ref, conv_b_ref, dtb_ref,
               q_out, k_out, vt_out, gza_out, xs_out, bm_out, cm_out, dt_out, gzs_out,
               xbuf, *, tm, sub, off, q_scale):
    i = pl.program_id(0)
    shift = mod_ref[0:1, :]
    scale1 = 1.0 + mod_ref[1:2, :]
    lane = lax.broadcasted_iota(jnp.int32, (sub, LANES), 1)
    low_half = lane < QK_ROPE_DIM
    nconv, nch = conv_w_ref.shape
    nx = xs_out.shape[1]
    nb = bm_out.shape[1]

    @pl.when(i == 0)
    def _():
        xbuf[0:SUBLANES, :] = jnp.zeros((SUBLANES, nch), F32)

    css = []
    for h in range(tm // sub):
        ang_t = pos_ref[:, h * sub:(h + 1) * sub].astype(F32) * rope_ref[...]
        cos_t, sin_t = jnp.cos(ang_t), jnp.sin(ang_t)
        css.append(jnp.concatenate([cos_t, cos_t, -sin_t, sin_t], axis=0).T)

    projs = []
    for h in range(tm // sub):
        u = x_ref[h * sub:(h + 1) * sub, :] * scale1 + shift
        projs.append(jnp.dot(u.astype(BF16), w_in_ref[...], preferred_element_type=F32))

    for h, (proj, cs) in enumerate(zip(projs, css)):
        rows = slice(h * sub, (h + 1) * sub)

        def rope(rr, cs=cs):
            t = rr * cs
            return jnp.where(low_half, t + pltpu.roll(t, QK_ROPE_DIM, axis=1), 0.0)

        q_lat = proj[:, off["q"]:off["q"] + qg_ref.shape[1]]
        qn = q_lat * lax.rsqrt(jnp.mean(q_lat * q_lat, axis=-1, keepdims=True) + RMS_EPS) * qg_ref[...]
        qf = jnp.dot(qn.astype(BF16), w_qb_ref[...], preferred_element_type=F32)
        for hd in range(MLA_HEADS):
            c0 = hd * QK_PAD
            q_out[rows, c0:c0 + QK_NOPE_DIM] = (qf[:, c0:c0 + QK_NOPE_DIM] * q_scale).astype(BF16)
            q_out[rows, c0 + QK_NOPE_DIM:c0 + QK_PAD] = (
                rope(qf[:, c0 + QK_NOPE_DIM:c0 + QK_PAD]) * q_scale).astype(BF16)

        c_kv = proj[:, off["ckv"]:off["ckv"] + kvg_ref.shape[1]]
        ckvn = c_kv * lax.rsqrt(jnp.mean(c_kv * c_kv, axis=-1, keepdims=True) + RMS_EPS) * kvg_ref[...]
        ckvn_bf = ckvn.astype(BF16)
        kf = jnp.dot(ckvn_bf, w_k_ref[...], preferred_element_type=F32)
        vt_out[:, rows] = lax.dot_general(w_vt_ref[...], ckvn_bf, (((1,), (1,)), ((), ())),
                                          preferred_element_type=F32).astype(BF16)
        k_rope = rope(proj[:, off["krope"]:off["krope"] + LANES]).astype(BF16)
        for hd in range(MLA_HEADS):
            k_out[rows, hd * QK_PAD:hd * QK_PAD + QK_NOPE_DIM] = (
                kf[:, hd * QK_NOPE_DIM:(hd + 1) * QK_NOPE_DIM].astype(BF16))
            k_out[rows, hd * QK_PAD + QK_NOPE_DIM:(hd + 1) * QK_PAD] = k_rope

        gza_out[rows, :] = _silu(proj[:, off["za"]:off["za"] + gza_out.shape[1]]).astype(BF16)
        gzs_out[rows, :] = _silu(proj[:, off["zs"]:off["zs"] + gzs_out.shape[1]]).astype(BF16)
        dt_out[rows, :] = _softplus(proj[:, off["dt"]:off["dt"] + LANES] + dtb_ref[...])

        r1 = SUBLANES + h * sub
        xbuf[r1:r1 + sub, :] = proj[:, off["xbc"]:off["xbc"] + nch]
        xfull = xbuf[r1 - SUBLANES:r1 + sub, :]
        acc = conv_b_ref[...] + conv_w_ref[nconv - 1:nconv, :] * xfull[SUBLANES:, :]
        for k in range(1, nconv):
            shifted = pltpu.roll(xfull, k, axis=0)[SUBLANES:, :]
            acc = acc + conv_w_ref[nconv - 1 - k:nconv - k, :] * shifted
        xc = _silu(acc)
        xs_out[rows, :] = xc[:, :nx].astype(BF16)
        bm_out[rows, :] = xc[:, nx:nx + nb].astype(BF16)
        cm_out[rows, :] = xc[:, nx + nb:nx + 2 * nb].astype(BF16)

    xbuf[0:SUBLANES, :] = xbuf[tm:tm + SUBLANES, :]


def _in_proj(x2, mod3, pos_row, rope_tab, w_in_p, qg, w_qb_p, kvg, w_k, w_vt, conv_w, conv_b, dtb,
             *, off, tm, sub, q_scale):
    s, d = x2.shape
    nq = MLA_HEADS * QK_PAD
    nv = MLA_HEADS * V_HEAD_DIM
    nch = conv_w.shape[1]
    nbc = SSM_GROUPS * SSM_STATE
    nx = nch - 2 * nbc
    row = lambda w: pl.BlockSpec((tm, w), lambda i: (i, 0))
    full = lambda a: pl.BlockSpec(a.shape, lambda i: (0,) * a.ndim, pipeline_mode=pl.Buffered(1))
    outs = [
        jax.ShapeDtypeStruct((s, nq), BF16),
        jax.ShapeDtypeStruct((s, nq), BF16),
        jax.ShapeDtypeStruct((nv, s), BF16),
        jax.ShapeDtypeStruct((s, nv), BF16),
        jax.ShapeDtypeStruct((s, nx), BF16),
        jax.ShapeDtypeStruct((s, nbc), BF16),
        jax.ShapeDtypeStruct((s, nbc), BF16),
        jax.ShapeDtypeStruct((s, LANES), F32),
        jax.ShapeDtypeStruct((s, nx), BF16),
    ]
    return pl.pallas_call(
        functools.partial(_in_kernel, tm=tm, sub=sub, off=off, q_scale=q_scale),
        grid=(s // tm,),
        in_specs=[row(d), full(mod3), pl.BlockSpec((1, tm), lambda i: (0, i)), full(rope_tab),
                  full(w_in_p), full(qg), full(w_qb_p), full(kvg), full(w_k), full(w_vt),
                  full(conv_w), full(conv_b), full(dtb)],
        out_specs=[pl.BlockSpec((nv, tm), lambda i: (0, i)) if n == 2 else row(o.shape[1])
                   for n, o in enumerate(outs)],
        out_shape=outs,
        scratch_shapes=[pltpu.VMEM((tm + 2 * SUBLANES, nch), F32)],
        compiler_params=pltpu.CompilerParams(dimension_semantics=("arbitrary",),
                                             vmem_limit_bytes=VMEM_LIMIT),
        name="in_proj",
    )(x2, mod3, pos_row, rope_tab, w_in_p, qg, w_qb_p, kvg, w_k, w_vt, conv_w, conv_b, dtb)


def _sublane_allmax(x):
    shift = SUBLANES // 2
    while shift:
        x = jnp.maximum(x, pltpu.roll(x, shift, axis=0))
        shift //= 2
    return x


def _sublane_allsum(x):
    shift = SUBLANES // 2
    while shift:
        x = x + pltpu.roll(x, shift, axis=0)
        shift //= 2
    return x


BF16_ROWS = 16
ACC_ROWS = V_HEAD_DIM + BF16_ROWS
EXP_ROWS = 32


def _attn_kernel(q_ref, qn_ref, k_ref, vt_ref, o_ref, m_a, acc_a, m_b, acc_b,
                 s_a, s_b, s_c, cmax_a, cmax_b, cmax_c, p_a, p_b, p_c, al_a, al_b, al_c,
                 *, tq, tk, ntile, n_tiles):
    assert tq == tk
    g = pl.program_id(1)
    nsub = tk // SUBLANES
    nacc = ACC_ROWS // SUBLANES
    buf_a = (s_a, cmax_a, p_a, al_a)
    buf_b = (s_b, cmax_b, p_b, al_b)
    buf_c = (s_c, cmax_c, p_c, al_c)
    ones_rows = jnp.ones((ACC_ROWS - V_HEAD_DIM, tk), BF16)

    def reset(stats):
        m_scr, acc_scr = stats
        m_scr[...] = jnp.full(m_scr.shape, NEG_BIG, F32)
        acc_scr[...] = jnp.zeros(acc_scr.shape, F32)

    def qk_stage(kv, q_rows, buf, diag=False, q_ref=q_ref):
        s_out, cmax_out = buf[0], buf[1]
        start = pl.multiple_of(kv * tk, tk)
        nt = (((1,), (1,)), ((), ()))
        colmax = lambda x: _sublane_allmax(jnp.max(x.reshape(x.shape[0] // SUBLANES, SUBLANES, x.shape[1]), axis=0))
        if not diag:
            st = lax.dot_general(k_ref[pl.ds(start, tk), :], q_ref[q_rows, :], nt,
                                 preferred_element_type=F32)
            s_out[...] = st
            cmax_out[...] = colmax(st)
            return
        hk = tk // 2
        q_lo = q_rows.start
        mask = (lax.broadcasted_iota(jnp.int32, (hk, tq), 0) <= lax.broadcasted_iota(jnp.int32, (hk, tq), 1))
        top = lax.dot_general(k_ref[pl.ds(start, hk), :], q_ref[q_rows, :], nt, preferred_element_type=F32)
        top = jnp.where(mask, top, NEG_BIG)
        bot = lax.dot_general(k_ref[pl.ds(start + hk, hk), :], q_ref[q_lo + hk:q_lo + tq, :], nt,
                              preferred_element_type=F32)
        bot = jnp.where(mask[:, :tq - hk], bot, NEG_BIG)
        s_out[0:hk, :] = top
        s_out[hk:tk, 0:hk] = jnp.full((tk - hk, hk), NEG_BIG, F32)
        s_out[hk:tk, hk:tq] = bot
        cm_top = colmax(top)
        cmax_out[:, 0:hk] = cm_top[:, 0:hk]
        cmax_out[:, hk:tq] = jnp.maximum(cm_top[:, hk:tq], colmax(bot))

    def softmax_stage(buf, stats):
        s_in, cmax_in, p_out, al_out = buf
        m_scr = stats[0]
        m_old = m_scr[...]
        m_new = jnp.maximum(m_old, cmax_in[...])
        al_out[...] = jnp.exp2(m_old - m_new)
        m_scr[...] = m_new
        for c in range(0, tq, MXU_DIM):
            m_c = m_new[:, c:c + MXU_DIM][None]
            for r in range(0, tk, EXP_ROWS):
                sc = s_in[r:r + EXP_ROWS, c:c + MXU_DIM].reshape(EXP_ROWS // SUBLANES, SUBLANES, MXU_DIM)
                p_out[r:r + EXP_ROWS, c:c + MXU_DIM] = (
                    jnp.exp2(sc - m_c).reshape(EXP_ROWS, MXU_DIM).astype(BF16))

    def pv_stage(kv, buf, stats):
        p_in, al_in = buf[2], buf[3]
        acc_scr = stats[1]
        start = pl.multiple_of(jnp.maximum(kv, 0) * tk, tk)
        vt_aug = jnp.concatenate([vt_ref[:, pl.ds(start, tk)], ones_rows], axis=0)
        pv = jnp.dot(vt_aug, p_in[...], preferred_element_type=F32)
        acc3 = acc_scr[...].reshape(nacc, SUBLANES, tq) * al_in[...][None]
        acc_scr[...] = acc3.reshape(ACC_ROWS, tq) + pv

    def finalize(q_rows, stats):
        acc_scr = stats[1]
        inv_l = 1.0 / acc_scr[V_HEAD_DIM:V_HEAD_DIM + SUBLANES, :]
        o_t = acc_scr[0:V_HEAD_DIM, :].reshape(V_HEAD_DIM // SUBLANES, SUBLANES, tq) * inv_l[None]
        o_ref[q_rows, :] = o_t.reshape(V_HEAD_DIM, tq).T.astype(o_ref.dtype)

    def make_step(qi, q_rows, stats):
        kv_of = lambda pos: jnp.where(pos == 0, qi, pos - 1)

        def step(pos, buf, other):
            qk_stage(kv_of(pos), q_rows, buf)
            pv_stage(kv_of(pos - 2), buf, stats)
            softmax_stage(other, stats)

        return kv_of, step

    def run_pairs(lo, hi, pair):
        def body(pp, carry):
            pair(pp)
            return carry

        lax.fori_loop(lo, hi, body, 0)

    bufs = (buf_a, buf_b)
    stats2 = ((m_a, acc_a), (m_b, acc_b))
    start = [0]
    for j in range(ntile - 1):
        last = start[j] if j % 2 == 0 else 1 - start[j]
        start.append(1 - last)

    def tile_ctx(j):
        qi = ntile * g + j
        rows = slice(j * tq, (j + 1) * tq)
        stats = stats2[j % 2]
        x, y = bufs[start[j]], bufs[1 - start[j]]
        kv_of, step = make_step(qi, rows, stats)
        return qi, rows, stats, x, y, kv_of, step

    def run_steps(j, first_pos, n_pairs):
        _, _, _, x, y, _, step = tile_ctx(j)
        even_buf, odd_buf = x, y

        def pair(pp):
            p0 = first_pos + 2 * pp
            if first_pos % 2 == 0:
                step(p0, even_buf, odd_buf)
                step(p0 + 1, odd_buf, even_buf)
            else:
                step(p0, odd_buf, even_buf)
                step(p0 + 1, even_buf, odd_buf)

        run_pairs(0, n_pairs, pair)

    def transition(j, last_buf, other):
        qi, rows, stats, _, _, kv_of, _ = tile_ctx(j)
        qi_n, rows_n, st_n, x_n, y_n, kv_n, _ = tile_ctx(j + 1)
        reset(st_n)
        qk_stage(qi_n, rows_n, x_n, diag=True)
        if other is not None:
            assert x_n is other
            pv_stage(kv_of(qi - 1), other, stats)
        softmax_stage(last_buf, stats)
        qk_stage(kv_n(1), rows_n, y_n)
        pv_stage(kv_of(qi), last_buf, stats)
        softmax_stage(x_n, st_n)
        finalize(rows, stats)

    qi0, rows0, st0, x0, y0, kv0, step0 = tile_ctx(0)
    reset(st0)

    @pl.when(g == 0)
    def _():
        qk_stage(qi0, rows0, buf_c, diag=True)
        transition(0, buf_c, None)

    @pl.when(g > 0)
    def _():
        qk_stage(kv0(1), rows0, y0)
        softmax_stage(buf_c, st0)
        qk_stage(kv0(2), rows0, x0)
        pv_stage(kv0(0), buf_c, st0)
        softmax_stage(y0, st0)
        run_steps(0, 3, (ntile * g - 2) // 2)
        transition(0, x0, y0)

    for j in range(1, ntile):
        qi, rows, stats, x, y, kv_of, step = tile_ctx(j)
        if j % 2 == 0:
            step(2, x, y)
            run_steps(j, 3, (ntile * g + j - 2) // 2)
        else:
            run_steps(j, 2, (ntile * g + j - 1) // 2)
        last_buf, other = (x, y) if j % 2 == 0 else (y, x)
        if j + 1 < ntile:
            transition(j, last_buf, other)
        else:
            qk_stage(jnp.minimum(qi + 1, n_tiles - 1), slice(0, tq), buf_c, diag=True, q_ref=qn_ref)
            pv_stage(kv_of(qi - 1), other, stats)
            softmax_stage(last_buf, stats)
            pv_stage(kv_of(qi), last_buf, stats)
            finalize(rows, stats)


def _attention(q, k, vt, *, tq, tk, ntile):
    s = q.shape[0]
    assert ntile % 2 == 0 and s % (ntile * tq) == 0, "sequence length must be a multiple of ntile query tiles"
    stat = pltpu.VMEM((SUBLANES, tq), F32)
    n_tiles = s // tq
    return pl.pallas_call(
        functools.partial(_attn_kernel, tq=tq, tk=tk, ntile=ntile, n_tiles=n_tiles),
        grid=(MLA_HEADS, s // (ntile * tq)),
        in_specs=[pl.BlockSpec((ntile * tq, QK_PAD), lambda h, i: (i, h)),
                  pl.BlockSpec((tq, QK_PAD), lambda h, i: (jnp.minimum(ntile * (i + 1), n_tiles - 1), h)),
                  pl.BlockSpec((s, QK_PAD), lambda h, i: (0, h)),
                  pl.BlockSpec((V_HEAD_DIM, s), lambda h, i: (h, 0))],
        out_specs=pl.BlockSpec((ntile * tq, V_HEAD_DIM), lambda h, i: (i, h)),
        out_shape=jax.ShapeDtypeStruct((s, MLA_HEADS * V_HEAD_DIM), BF16),
        scratch_shapes=[stat, pltpu.VMEM((ACC_ROWS, tq), F32), stat, pltpu.VMEM((ACC_ROWS, tq), F32),
                        pltpu.VMEM((tk, tq), F32), pltpu.VMEM((tk, tq), F32), pltpu.VMEM((tk, tq), F32),
                        stat, stat, stat,
                        pltpu.VMEM((tk, tq), BF16), pltpu.VMEM((tk, tq), BF16), pltpu.VMEM((tk, tq), BF16),
                        stat, stat, stat],
        compiler_params=pltpu.CompilerParams(dimension_semantics=("arbitrary", "arbitrary"),
                                             vmem_limit_bytes=VMEM_LIMIT),
        name="attention",
    )(q, q, k, vt)


def _ssd_out_kernel(xs_ref, bm_ref, cm_ref, dt_ref, gz_ref, alog_ref, dskip_ref, g_ref, e2_ref,
                    oa_ref, gza_ref, x_ref, w32_ref, mod_ref, lng_ref, lnb_ref,
                    o_ref, state, y_scr, w_ref, os_scr, *, rows, nheads, alpha, sub):
    i = pl.program_id(0)
    gw = SSM_STATE
    hw = state.shape[2]
    na = oa_ref.shape[1]

    @pl.when(i == 0)
    def _():
        state[...] = jnp.zeros(state.shape, F32)
        for r in range(0, w_ref.shape[0], sub):
            w_ref[r:r + sub, :] = w32_ref[r:r + sub, :].astype(BF16)

    mixes = []
    for r in range(0, rows, sub):
        oa = (oa_ref[r:r + sub, :].astype(F32) * gza_ref[r:r + sub, :].astype(F32)).astype(BF16)
        mixes.append(jnp.dot(oa, w_ref[0:na, :], preferred_element_type=F32))

    a_neg = -jnp.exp(alog_ref[...])
    r_i = lax.broadcasted_iota(jnp.int32, (CHUNK, CHUNK), 0)
    c_i = lax.broadcasted_iota(jnp.int32, (CHUNK, CHUNK), 1)
    tri = c_i <= r_i
    tri_bf = tri.astype(BF16)
    lane = lax.broadcasted_iota(jnp.int32, (CHUNK, LANES), 1)
    head_lo = lane < SSM_HEAD_DIM
    first_copy = lax.broadcasted_iota(jnp.int32, (rows, LANES), 1) < nheads
    e2 = e2_ref[...]

    def expand(v):
        hi = v.astype(BF16)
        lo = (v - hi.astype(F32)).astype(BF16)
        return jnp.dot(jnp.where(first_copy, hi, lo), e2, preferred_element_type=F32)

    dt_all = dt_ref[...]
    da_all = dt_all * (a_neg * LOG2E)
    a_cums = []
    for cidx in range(rows // CHUNK):
        a_cum = jnp.zeros((CHUNK, LANES), F32)
        for part in _split_bf16(da_all[cidx * CHUNK:(cidx + 1) * CHUNK, :], 3):
            a_cum = a_cum + jnp.dot(tri_bf, part, preferred_element_type=F32)
        a_cums.append(a_cum)
    dt_x_all = expand(dt_all)
    ea_x_all = expand(jnp.exp2(jnp.concatenate(a_cums, axis=0)))
    dte_x_all = expand(jnp.exp2(jnp.concatenate([a[CHUNK - 1:CHUNK, :] - a for a in a_cums], axis=0)))

    for cidx in range(rows // CHUNK):
        sl = pl.ds(cidx * CHUNK, CHUNK)
        rs = slice(cidx * CHUNK, (cidx + 1) * CHUNK)
        a_cum = a_cums[cidx]
        a_cum_t = a_cum.T
        ea_x = ea_x_all[rs, :]

        xs = xs_ref[sl, :].astype(F32)
        xd = xs * dt_x_all[rs, :]
        xdd = (xd * dte_x_all[rs, :]).astype(BF16)
        xd_bf = xd.astype(BF16)

        for g in range(SSM_GROUPS):
            bg = bm_ref[sl, g * gw:(g + 1) * gw]
            cg = cm_ref[sl, g * gw:(g + 1) * gw]
            cb = lax.dot_general(cg, bg, (((1,), (1,)), ((), ())), preferred_element_type=F32)
            prev = state[g]
            y_off = jnp.dot(cg, prev.astype(BF16), preferred_element_type=F32)
            new = lax.dot_general(bg, xdd[:, g * hw:(g + 1) * hw], (((0,), (0,)), ((), ())),
                                  preferred_element_type=F32)
            state[g] = prev * ea_x[CHUNK - 1:CHUNK, g * hw:(g + 1) * hw] + new
            y_scr[:, g * hw:(g + 1) * hw] = y_off * ea_x[:, g * hw:(g + 1) * hw]

            hpg = hw // SSM_HEAD_DIM
            for pair in range(hpg // 2):
                h0 = g * hpg + 2 * pair
                c0 = h0 * SSM_HEAD_DIM
                xp = xd_bf[:, c0:c0 + LANES]
                yp = jnp.zeros((CHUNK, LANES), F32)
                for k, keep in ((0, head_lo), (1, ~head_lo)):
                    h = h0 + k
                    seg = a_cum[:, h:h + 1] - a_cum_t[h:h + 1, :]
                    m_h = (cb * jnp.exp2(jnp.where(tri, seg, NEG_BIG))).astype(BF16)
                    yp = yp + jnp.dot(m_h, jnp.where(keep, xp, jnp.zeros_like(xp)),
                                      preferred_element_type=F32)
                y_scr[:, c0:c0 + LANES] = y_scr[:, c0:c0 + LANES] + yp

        y = y_scr[...] + xs * dskip_ref[...]
        hf = y * gz_ref[sl, :].astype(F32)
        for g in range(SSM_GROUPS):
            hg = hf[:, g * hw:(g + 1) * hw]
            ms = jnp.mean(hg * hg, axis=-1, keepdims=True)
            os_scr[sl, g * hw:(g + 1) * hw] = (hg * lax.rsqrt(ms + RMS_EPS)
                                                * g_ref[:, g * hw:(g + 1) * hw]).astype(os_scr.dtype)

    for n, mixed in enumerate(mixes):
        rws = slice(n * sub, (n + 1) * sub)
        mixed = mixed + jnp.dot(os_scr[rws, :], w_ref[na:, :], preferred_element_type=F32)
        y = alpha * x_ref[rws, :] + mod_ref[2:3, :] * mixed
        mu = jnp.mean(y, axis=-1, keepdims=True)
        yc = y - mu
        var = jnp.mean(yc * yc, axis=-1, keepdims=True)
        o_ref[rws, :] = yc * lax.rsqrt(var + LN_EPS) * lng_ref[...] + lnb_ref[...]


def _ssd_out(xs, bm, cm, dt, gzs, alog, dskip_x, norm_g, e2, oa, gza, x2, w_out, mod3, ln_g, ln_b,
             *, rows, sub, nheads, alpha):
    s, nx = xs.shape
    d = x2.shape[1]
    hw = nx // SSM_GROUPS
    row = lambda a: pl.BlockSpec((rows, a.shape[1]), lambda i: (i, 0))
    full = lambda a: pl.BlockSpec(a.shape, lambda i: (0,) * a.ndim)
    return pl.pallas_call(
        functools.partial(_ssd_out_kernel, rows=rows, nheads=nheads, alpha=alpha, sub=sub),
        grid=(s // rows,),
        in_specs=[row(xs), row(bm), row(cm), row(dt), row(gzs),
                  full(alog), full(dskip_x), full(norm_g), full(e2),
                  row(oa), row(gza), row(x2),
                  pl.BlockSpec(w_out.shape, lambda i: (0, 0), pipeline_mode=pl.Buffered(1)),
                  full(mod3), full(ln_g), full(ln_b)],
        out_specs=pl.BlockSpec((rows, d), lambda i: (i, 0)),
        out_shape=jax.ShapeDtypeStruct((s, d), F32),
        scratch_shapes=[pltpu.VMEM((SSM_GROUPS, SSM_STATE, hw), F32),
                        pltpu.VMEM((CHUNK, nx), F32),
                        pltpu.VMEM(w_out.shape, BF16),
                        pltpu.VMEM((rows, nx), BF16)],
        compiler_params=pltpu.CompilerParams(dimension_semantics=("arbitrary",),
                                             vmem_limit_bytes=VMEM_LIMIT),
        name="ssd_out",
    )(xs, bm, cm, dt, gzs, alog, dskip_x, norm_g, e2, oa, gza, x2, w_out, mod3, ln_g, ln_b)


def _pad_cols(a, width):
    return jnp.pad(a, ((0, 0), (0, width - a.shape[1])))


def _regroup_kernel(wt_ref, o_ref, *, segments):
    wt = wt_ref[...]
    parts = [jnp.zeros((b, wt.shape[1]), wt.dtype) if a is None else wt[a:b, :] for a, b in segments]
    o_ref[...] = jnp.concatenate(parts, axis=0).T.astype(o_ref.dtype)


def _regroup_columns(w, segments, width):
    assert all(b % SUBLANES == 0 and (a or 0) % SUBLANES == 0 for a, b in segments)
    rows = w.shape[0]
    br = 256
    return pl.pallas_call(
        functools.partial(_regroup_kernel, segments=segments),
        grid=(rows // br,),
        in_specs=[pl.BlockSpec((w.shape[1], br), lambda i: (0, i))],
        out_specs=pl.BlockSpec((br, width), lambda i: (i, 0)),
        out_shape=jax.ShapeDtypeStruct((rows, width), BF16),
        compiler_params=pltpu.CompilerParams(vmem_limit_bytes=VMEM_LIMIT),
        name="regroup_w_in",
    )(w.T)


def _layer(x2, c, pos_row, w_ada, b_ada, w_in, q_norm_g, w_qb, kv_norm_g, w_kvb,
           conv_w, conv_b, dt_bias, a_log, d_skip, ssm_norm_g, w_out, ln_g, ln_b, *, depth):
    s, d = x2.shape
    q_rank = q_norm_g.shape[0]
    kv_rank = kv_norm_g.shape[0]
    nheads = dt_bias.shape[0]
    nch = conv_w.shape[1]
    nv = MLA_HEADS * V_HEAD_DIM
    nx = nheads * SSM_HEAD_DIM
    half = QK_ROPE_DIM // 2

    mod = _adaln_mod(c.reshape(d, 1), w_ada, b_ada.reshape(1, -1))
    mod3 = mod.reshape(3, d)

    o_q, o_ckv = 0, q_rank
    o_kr = o_ckv + kv_rank
    o_za = o_kr + QK_ROPE_DIM
    o_xbc = o_za + nv
    o_dt = o_xbc + nch
    o_zs = o_dt + nheads
    dt_pad = LANES - 2 * nheads
    groups = [("q", [(o_q, o_ckv)]), ("ckv", [(o_ckv, o_kr)]),
              ("krope", [(o_kr, o_za), (o_kr + half, o_za), (o_kr, o_kr + half)]),
              ("za", [(o_za, o_xbc)]), ("xbc", [(o_xbc, o_dt)]), ("zs", [(o_zs, o_zs + nx)]),
              ("dt", [(o_dt, o_zs), (o_dt, o_zs), (None, dt_pad)])]
    off, cur, segments = {}, 0, []
    for name, segs in groups:
        assert cur % LANES == 0
        off[name] = cur
        for a, b in segs:
            cur += b if a is None else b - a
            if segments and a is not None and segments[-1][0] is not None and segments[-1][1] == a:
                segments[-1] = (segments[-1][0], b)
            else:
                segments.append((a, b))
    assert cur % LANES == 0
    w_in_p = _regroup_columns(w_in, tuple(segments), cur)

    w3 = w_qb.reshape(q_rank, MLA_HEADS, QK_NOPE_DIM + QK_ROPE_DIM)
    w_rope = w3[:, :, QK_NOPE_DIM:]
    w_qb_p = jnp.concatenate([w3, w_rope[:, :, half:], w_rope[:, :, :half]], axis=2)
    w_qb_p = w_qb_p.reshape(q_rank, MLA_HEADS * QK_PAD).astype(BF16)

    inv_freq = 1.0 / (ROPE_THETA ** (jnp.arange(half, dtype=F32) / half))
    rope_tab = jnp.broadcast_to(inv_freq[:, None], (half, IN_SUB))

    dtb = _pad_cols(jnp.concatenate([dt_bias, dt_bias]).reshape(1, -1), LANES)
    q_scale = (QK_NOPE_DIM + QK_ROPE_DIM) ** -0.5 * LOG2E

    wkv3 = w_kvb.reshape(kv_rank, MLA_HEADS, QK_NOPE_DIM + V_HEAD_DIM)
    w_k = wkv3[:, :, :QK_NOPE_DIM].reshape(kv_rank, MLA_HEADS * QK_NOPE_DIM).astype(BF16)
    w_vt = wkv3[:, :, QK_NOPE_DIM:].reshape(kv_rank, nv).T.astype(BF16)

    q, k, vt, gza, xs, bm, cm, dt, gzs = _in_proj(
        x2, mod3, pos_row, rope_tab, w_in_p, q_norm_g.reshape(1, -1), w_qb_p,
        kv_norm_g.reshape(1, -1), w_k, w_vt, conv_w, conv_b.reshape(1, -1), dtb,
        off=off, tm=IN_ROWS, sub=IN_SUB, q_scale=q_scale)

    o_attn = _attention(q, k, vt, tq=ATTN_TILE, tk=ATTN_TILE, ntile=ATTN_TILES_PER_STEP)

    alog = _pad_cols(jnp.concatenate([a_log, a_log]).reshape(1, -1), LANES)
    e_head = jnp.repeat(jnp.eye(nheads, dtype=BF16), SSM_HEAD_DIM, axis=1)
    e2 = jnp.pad(jnp.concatenate([e_head, e_head], axis=0), ((0, LANES - 2 * nheads), (0, 0)))
    dskip_x = jnp.repeat(d_skip, SSM_HEAD_DIM).reshape(1, -1)
    alpha = (2.0 * depth) ** 0.25
    return _ssd_out(xs, bm, cm, dt, gzs, alog, dskip_x, ssm_norm_g.reshape(1, -1), e2,
                    o_attn, gza, x2, w_out, mod3, ln_g.reshape(1, -1), ln_b.reshape(1, -1),
                    rows=SSD_ROWS, sub=OUT_SUB, nheads=nheads, alpha=alpha)


def kernel(x, c, positions, w_ada, b_ada, w_in, q_norm_g, w_qb, kv_norm_g, w_kvb, conv_w, conv_b,
           dt_bias, a_log, d_skip, ssm_norm_g, w_out, ln_g, ln_b):
    b, s, d = x.shape
    depth = w_in.shape[0]
    assert b == 1, "one sequence per call"
    h = x.reshape(s, d)
    pos_row = positions.reshape(1, s)
    for l in range(depth):
        h = _layer(h, c, pos_row, w_ada[l], b_ada[l], w_in[l], q_norm_g[l], w_qb[l], kv_norm_g[l],
                   w_kvb[l], conv_w[l], conv_b[l], dt_bias[l], a_log[l], d_skip[l], ssm_norm_g[l],
                   w_out[l], ln_g[l], ln_b[l], depth=depth)
    return h.reshape(b, s, d)
```

```python
import functools
import math

import jax
import jax.numpy as jnp
from jax import lax
from jax.experimental import pallas as pl
from jax.experimental.pallas import tpu as pltpu

F32 = jnp.float32
BF16 = jnp.bfloat16

MLA_HEADS = 8
QK_NOPE_DIM = 128
QK_ROPE_DIM = 64
V_HEAD_DIM = 128
ROPE_THETA = 10000.0
SSM_HEAD_DIM = 64
SSM_GROUPS = 2
SSM_STATE = 128
CHUNK = 128
RMS_EPS = 1e-6
LN_EPS = 1e-5

LANES = 128
SUBLANES = 8
MXU_DIM = 256
QK_PAD = MXU_DIM
VMEM_LIMIT = 56 * 1024 * 1024

IN_ROWS = 512
IN_SUB = 256
OUT_SUB = 256
ATTN_TILE = 1024
ATTN_TILES_PER_STEP = 2
SSD_ROWS = 512

LOG2E = 1.4426950408889634
NEG_BIG = -1e30


def _silu(z):
    h = 0.5 * z
    return h * jnp.tanh(h) + h


def _softplus(z):
    return jnp.maximum(z, 0.0) + jnp.log1p(jnp.exp(-jnp.abs(z)))


def _split_bf16(x, parts):
    out, rem = [], x
    for _ in range(parts):
        hi = rem.astype(BF16)
        out.append(hi)
        rem = rem - hi.astype(F32)
    return out


def _mod_kernel(c_ref, w_ref, b_ref, o_ref):
    o_ref[...] = jnp.sum(w_ref[...] * c_ref[...], axis=0, keepdims=True) + b_ref[...]


def _adaln_mod(c_col, w_ada, b_ada):
    d, n = w_ada.shape
    bn = 1024
    return pl.pallas_call(
        _mod_kernel,
        grid=(n // bn,),
        in_specs=[pl.BlockSpec((d, 1), lambda j: (0, 0)),
                  pl.BlockSpec((d, bn), lambda j: (0, j)),
                  pl.BlockSpec((1, bn), lambda j: (0, j))],
        out_specs=pl.BlockSpec((1, bn), lambda j: (0, j)),
        out_shape=jax.ShapeDtypeStruct((1, n), F32),
        name="adaln_mod",
    )(c_col, w_ada, b_ada)


def _in_kernel(x_ref, mod_ref, pos_ref, rope_ref, w_in_ref, qg_ref, w_qb_ref, kvg_ref, w_k_ref, w_vt_ref,
               conv_w_ref, conv_b_ref, dtb_ref,
               q_out, k_out, vt_out, gza_out, xs_out, bm_out, cm_out, dt_out, gzs_out,
               xbuf, *, tm, sub, off, q_scale):
    i = pl.program_id(0)
    shift = mod_ref[0:1, :]
    scale1 = 1.0 + mod_ref[1:2, :]
    lane = lax.broadcasted_iota(jnp.int32, (sub, LANES), 1)
    low_half = lane < QK_ROPE_DIM
    nconv, nch = conv_w_ref.shape
    nx = xs_out.shape[1]
    nb = bm_out.shape[1]

    @pl.when(i == 0)
    def _():
        xbuf[0:SUBLANES, :] = jnp.zeros((SUBLANES, nch), F32)

    css, cos_ts, sin_ts = [], [], []
    for h in range(tm // sub):
        ang_t = pos_ref[:, h * sub:(h + 1) * sub].astype(F32) * rope_ref[...]
        cos_t, sin_t = jnp.cos(ang_t), jnp.sin(ang_t)
        cos_ts.append(cos_t)
        sin_ts.append(sin_t)
        css.append(jnp.concatenate([cos_t, cos_t, -sin_t, sin_t], axis=0).T)

    projs = []
    for h in range(tm // sub):
        u = x_ref[h * sub:(h + 1) * sub, :] * scale1 + shift
        projs.append(jnp.dot(u.astype(BF16), w_in_ref[...], preferred_element_type=F32))

    for h, (proj, cs) in enumerate(zip(projs, css)):
        rows = slice(h * sub, (h + 1) * sub)

        def rope(rr, cs=cs):
            t = rr * cs
            return jnp.where(low_half, t + pltpu.roll(t, QK_ROPE_DIM, axis=1), 0.0)

        q_lat = proj[:, off["q"]:off["q"] + qg_ref.shape[1]]
        qn = q_lat * lax.rsqrt(jnp.mean(q_lat * q_lat, axis=-1, keepdims=True) + RMS_EPS) * qg_ref[...]
        qft = lax.dot_general(w_qb_ref[...], qn.astype(BF16), (((1,), (1,)), ((), ())),
                              preferred_element_type=F32) * q_scale
        half = QK_ROPE_DIM // 2
        hd_rows = QK_NOPE_DIM + QK_ROPE_DIM
        cos_t, sin_t = cos_ts[h], sin_ts[h]
        for hd in range(MLA_HEADS):
            r0, o0 = hd * hd_rows, hd * QK_PAD
            x1 = qft[r0 + QK_NOPE_DIM:r0 + QK_NOPE_DIM + half, :]
            x2 = qft[r0 + QK_NOPE_DIM + half:r0 + hd_rows, :]
            q_out[o0:o0 + QK_NOPE_DIM, rows] = qft[r0:r0 + QK_NOPE_DIM, :].astype(BF16)
            q_out[o0 + QK_NOPE_DIM:o0 + QK_NOPE_DIM + half, rows] = (x1 * cos_t - x2 * sin_t).astype(BF16)
            q_out[o0 + QK_NOPE_DIM + half:o0 + hd_rows, rows] = (x2 * cos_t + x1 * sin_t).astype(BF16)
            q_out[o0 + hd_rows:o0 + QK_PAD, rows] = jnp.zeros((QK_PAD - hd_rows, sub), BF16)

        c_kv = proj[:, off["ckv"]:off["ckv"] + kvg_ref.shape[1]]
        ckvn = c_kv * lax.rsqrt(jnp.mean(c_kv * c_kv, axis=-1, keepdims=True) + RMS_EPS) * kvg_ref[...]
        ckvn_bf = ckvn.astype(BF16)
        kf = jnp.dot(ckvn_bf, w_k_ref[...], preferred_element_type=F32)
        vt_out[:, rows] = lax.dot_general(w_vt_ref[...], ckvn_bf, (((1,), (1,)), ((), ())),
                                          preferred_element_type=F32).astype(BF16)
        k_rope = rope(proj[:, off["krope"]:off["krope"] + LANES]).astype(BF16)
        for hd in range(MLA_HEADS):
            k_out[rows, hd * QK_PAD:hd * QK_PAD + QK_NOPE_DIM] = (
                kf[:, hd * QK_NOPE_DIM:(hd + 1) * QK_NOPE_DIM].astype(BF16))
            k_out[rows, hd * QK_PAD + QK_NOPE_DIM:(hd + 1) * QK_PAD] = k_rope

        gza_out[rows, :] = _silu(proj[:, off["za"]:off["za"] + gza_out.shape[1]]).astype(BF16)
        gzs_out[rows, :] = _silu(proj[:, off["zs"]:off["zs"] + gzs_out.shape[1]]).astype(BF16)
        dt_out[rows, :] = _softplus(proj[:, off["dt"]:off["dt"] + LANES] + dtb_ref[...])

        r1 = SUBLANES + h * sub
        xbuf[r1:r1 + sub, :] = proj[:, off["xbc"]:off["xbc"] + nch]
        xfull = xbuf[r1 - SUBLANES:r1 + sub, :]
        acc = conv_b_ref[...] + conv_w_ref[nconv - 1:nconv, :] * xfull[SUBLANES:, :]
        for k in range(1, nconv):
            shifted = pltpu.roll(xfull, k, axis=0)[SUBLANES:, :]
            acc = acc + conv_w_ref[nconv - 1 - k:nconv - k, :] * shifted
        xc = _silu(acc)
        xs_out[rows, :] = xc[:, :nx].astype(BF16)
        bm_out[rows, :] = xc[:, nx:nx + nb].astype(BF16)
        cm_out[rows, :] = xc[:, nx + nb:nx + 2 * nb].astype(BF16)

    xbuf[0:SUBLANES, :] = xbuf[tm:tm + SUBLANES, :]


def _in_proj(x2, mod3, pos_row, rope_tab, w_in_p, qg, w_qb_p, kvg, w_k, w_vt, conv_w, conv_b, dtb,
             *, off, tm, sub, q_scale):
    s, d = x2.shape
    nq = MLA_HEADS * QK_PAD
    nv = MLA_HEADS * V_HEAD_DIM
    nch = conv_w.shape[1]
    nbc = SSM_GROUPS * SSM_STATE
    nx = nch - 2 * nbc
    row = lambda w: pl.BlockSpec((tm, w), lambda i: (i, 0))
    full = lambda a: pl.BlockSpec(a.shape, lambda i: (0,) * a.ndim, pipeline_mode=pl.Buffered(1))
    outs = [
        jax.ShapeDtypeStruct((nq, s), BF16),
        jax.ShapeDtypeStruct((s, nq), BF16),
        jax.ShapeDtypeStruct((nv, s), BF16),
        jax.ShapeDtypeStruct((s, nv), BF16),
        jax.ShapeDtypeStruct((s, nx), BF16),
        jax.ShapeDtypeStruct((s, nbc), BF16),
        jax.ShapeDtypeStruct((s, nbc), BF16),
        jax.ShapeDtypeStruct((s, LANES), F32),
        jax.ShapeDtypeStruct((s, nx), BF16),
    ]
    return pl.pallas_call(
        functools.partial(_in_kernel, tm=tm, sub=sub, off=off, q_scale=q_scale),
        grid=(s // tm,),
        in_specs=[row(d), full(mod3), pl.BlockSpec((1, tm), lambda i: (0, i)), full(rope_tab),
                  full(w_in_p), full(qg), full(w_qb_p), full(kvg), full(w_k), full(w_vt),
                  full(conv_w), full(conv_b), full(dtb)],
        out_specs=[pl.BlockSpec((o.shape[0], tm), lambda i: (0, i)) if n in (0, 2) else row(o.shape[1])
                   for n, o in enumerate(outs)],
        out_shape=outs,
        scratch_shapes=[pltpu.VMEM((tm + 2 * SUBLANES, nch), F32)],
        compiler_params=pltpu.CompilerParams(dimension_semantics=("arbitrary",),
                                             vmem_limit_bytes=VMEM_LIMIT),
        name="in_proj",
    )(x2, mod3, pos_row, rope_tab, w_in_p, qg, w_qb_p, kvg, w_k, w_vt, conv_w, conv_b, dtb)


def _sublane_allmax(x):
    shift = SUBLANES // 2
    while shift:
        x = jnp.maximum(x, pltpu.roll(x, shift, axis=0))
        shift //= 2
    return x


def _sublane_allsum(x):
    shift = SUBLANES // 2
    while shift:
        x = x + pltpu.roll(x, shift, axis=0)
        shift //= 2
    return x


BF16_ROWS = 16
ACC_ROWS = V_HEAD_DIM + BF16_ROWS
EXP_ROWS = 32


def _attn_kernel(q_ref, qn_ref, k_ref, vt_ref, o_ref, m_a, acc_a, m_b, acc_b,
                 s_a, s_b, s_c, cmax_a, cmax_b, cmax_c, p_a, p_b, p_c, al_a, al_b, al_c,
                 *, tq, tk, ntile, n_tiles):
    assert tq == tk
    g = pl.program_id(1)
    nsub = tk // SUBLANES
    nacc = ACC_ROWS // SUBLANES
    buf_a = (s_a, cmax_a, p_a, al_a)
    buf_b = (s_b, cmax_b, p_b, al_b)
    buf_c = (s_c, cmax_c, p_c, al_c)
    ones_rows = jnp.ones((ACC_ROWS - V_HEAD_DIM, tk), BF16)

    def reset(stats):
        m_scr, acc_scr = stats
        m_scr[...] = jnp.full(m_scr.shape, NEG_BIG, F32)
        acc_scr[...] = jnp.zeros(acc_scr.shape, F32)

    def qk_stage(kv, q_rows, buf, diag=False, q_ref=q_ref):
        s_out, cmax_out = buf[0], buf[1]
        start = pl.multiple_of(kv * tk, tk)
        colmax = lambda x: _sublane_allmax(jnp.max(x.reshape(x.shape[0] // SUBLANES, SUBLANES, x.shape[1]), axis=0))
        if not diag:
            st = jnp.dot(k_ref[pl.ds(start, tk), :], q_ref[:, q_rows],
                         preferred_element_type=F32)
            s_out[...] = st
            cmax_out[...] = colmax(st)
            return
        hk = tk // 2
        q_lo = q_rows.start
        mask = (lax.broadcasted_iota(jnp.int32, (hk, tq), 0) <= lax.broadcasted_iota(jnp.int32, (hk, tq), 1))
        top = jnp.dot(k_ref[pl.ds(start, hk), :], q_ref[:, q_rows], preferred_element_type=F32)
        top = jnp.where(mask, top, NEG_BIG)
        bot = jnp.dot(k_ref[pl.ds(start + hk, hk), :], q_ref[:, q_lo + hk:q_lo + tq],
                      preferred_element_type=F32)
        bot = jnp.where(mask[:, :tq - hk], bot, NEG_BIG)
        s_out[0:hk, :] = top
        s_out[hk:tk, 0:hk] = jnp.full((tk - hk, hk), NEG_BIG, F32)
        s_out[hk:tk, hk:tq] = bot
        cm_top = colmax(top)
        cmax_out[:, 0:hk] = cm_top[:, 0:hk]
        cmax_out[:, hk:tq] = jnp.maximum(cm_top[:, hk:tq], colmax(bot))

    def softmax_stage(buf, stats):
        s_in, cmax_in, p_out, al_out = buf
        m_scr = stats[0]
        m_old = m_scr[...]
        m_new = jnp.maximum(m_old, cmax_in[...])
        al_out[...] = jnp.exp2(m_old - m_new)
        m_scr[...] = m_new
        for c in range(0, tq, MXU_DIM):
            m_c = m_new[:, c:c + MXU_DIM][None]
            for r in range(0, tk, EXP_ROWS):
                sc = s_in[r:r + EXP_ROWS, c:c + MXU_DIM].reshape(EXP_ROWS // SUBLANES, SUBLANES, MXU_DIM)
                p_out[r:r + EXP_ROWS, c:c + MXU_DIM] = (
                    jnp.exp2(sc - m_c).reshape(EXP_ROWS, MXU_DIM).astype(BF16))

    def pv_stage(kv, buf, stats):
        p_in, al_in = buf[2], buf[3]
        acc_scr = stats[1]
        start = pl.multiple_of(jnp.maximum(kv, 0) * tk, tk)
        vt_aug = jnp.concatenate([vt_ref[:, pl.ds(start, tk)], ones_rows], axis=0)
        pv = jnp.dot(vt_aug, p_in[...], preferred_element_type=F32)
        acc3 = acc_scr[...].reshape(nacc, SUBLANES, tq) * al_in[...][None]
        acc_scr[...] = acc3.reshape(ACC_ROWS, tq) + pv

    def finalize(q_rows, stats):
        acc_scr = stats[1]
        inv_l = 1.0 / acc_scr[V_HEAD_DIM:V_HEAD_DIM + SUBLANES, :]
        o_t = acc_scr[0:V_HEAD_DIM, :].reshape(V_HEAD_DIM // SUBLANES, SUBLANES, tq) * inv_l[None]
        o_ref[q_rows, :] = o_t.reshape(V_HEAD_DIM, tq).T.astype(o_ref.dtype)

    def make_step(qi, q_rows, stats):
        kv_of = lambda pos: jnp.where(pos == 0, qi, pos - 1)

        def step(pos, buf, other):
            qk_stage(kv_of(pos), q_rows, buf)
            pv_stage(kv_of(pos - 2), buf, stats)
            softmax_stage(other, stats)

        return kv_of, step

    def run_pairs(lo, hi, pair):
        def body(pp, carry):
            pair(pp)
            return carry

        lax.fori_loop(lo, hi, body, 0)

    bufs = (buf_a, buf_b)
    stats2 = ((m_a, acc_a), (m_b, acc_b))
    start = [0]
    for j in range(ntile - 1):
        last = start[j] if j % 2 == 0 else 1 - start[j]
        start.append(1 - last)

    def tile_ctx(j):
        qi = ntile * g + j
        rows = slice(j * tq, (j + 1) * tq)
        stats = stats2[j % 2]
        x, y = bufs[start[j]], bufs[1 - start[j]]
        kv_of, step = make_step(qi, rows, stats)
        return qi, rows, stats, x, y, kv_of, step

    def run_steps(j, first_pos, n_pairs):
        _, _, _, x, y, _, step = tile_ctx(j)
        even_buf, odd_buf = x, y

        def pair(pp):
            p0 = first_pos + 2 * pp
            if first_pos % 2 == 0:
                step(p0, even_buf, odd_buf)
                step(p0 + 1, odd_buf, even_buf)
            else:
                step(p0, odd_buf, even_buf)
                step(p0 + 1, even_buf, odd_buf)

        run_pairs(0, n_pairs, pair)

    def transition(j, last_buf, other):
        qi, rows, stats, _, _, kv_of, _ = tile_ctx(j)
        qi_n, rows_n, st_n, x_n, y_n, kv_n, _ = tile_ctx(j + 1)
        reset(st_n)
        qk_stage(qi_n, rows_n, x_n, diag=True)
        if other is not None:
            assert x_n is other
            pv_stage(kv_of(qi - 1), other, stats)
        softmax_stage(last_buf, stats)
        qk_stage(kv_n(1), rows_n, y_n)
        pv_stage(kv_of(qi), last_buf, stats)
        softmax_stage(x_n, st_n)
        finalize(rows, stats)

    qi0, rows0, st0, x0, y0, kv0, step0 = tile_ctx(0)
    reset(st0)

    @pl.when(g == 0)
    def _():
        qk_stage(qi0, rows0, buf_c, diag=True)
        transition(0, buf_c, None)

    @pl.when(g > 0)
    def _():
        qk_stage(kv0(1), rows0, y0)
        softmax_stage(buf_c, st0)
        qk_stage(kv0(2), rows0, x0)
        pv_stage(kv0(0), buf_c, st0)
        softmax_stage(y0, st0)
        run_steps(0, 3, (ntile * g - 2) // 2)
        transition(0, x0, y0)

    for j in range(1, ntile):
        qi, rows, stats, x, y, kv_of, step = tile_ctx(j)
        if j % 2 == 0:
            step(2, x, y)
            run_steps(j, 3, (ntile * g + j - 2) // 2)
        else:
            run_steps(j, 2, (ntile * g + j - 1) // 2)
        last_buf, other = (x, y) if j % 2 == 0 else (y, x)
        if j + 1 < ntile:
            transition(j, last_buf, other)
        else:
            qk_stage(jnp.minimum(qi + 1, n_tiles - 1), slice(0, tq), buf_c, diag=True, q_ref=qn_ref)
            pv_stage(kv_of(qi - 1), other, stats)
            softmax_stage(last_buf, stats)
            pv_stage(kv_of(qi), last_buf, stats)
            finalize(rows, stats)


def _attention(qt, k, vt, *, tq, tk, ntile):
    s = k.shape[0]
    assert ntile % 2 == 0 and s % (ntile * tq) == 0, "sequence length must be a multiple of ntile query tiles"
    stat = pltpu.VMEM((SUBLANES, tq), F32)
    n_tiles = s // tq
    return pl.pallas_call(
        functools.partial(_attn_kernel, tq=tq, tk=tk, ntile=ntile, n_tiles=n_tiles),
        grid=(MLA_HEADS, s // (ntile * tq)),
        in_specs=[pl.BlockSpec((QK_PAD, ntile * tq), lambda h, i: (h, i)),
                  pl.BlockSpec((QK_PAD, tq), lambda h, i: (h, jnp.minimum(ntile * (i + 1), n_tiles - 1))),
                  pl.BlockSpec((s, QK_PAD), lambda h, i: (0, h)),
                  pl.BlockSpec((V_HEAD_DIM, s), lambda h, i: (h, 0))],
        out_specs=pl.BlockSpec((ntile * tq, V_HEAD_DIM), lambda h, i: (i, h)),
        out_shape=jax.ShapeDtypeStruct((s, MLA_HEADS * V_HEAD_DIM), BF16),
        scratch_shapes=[stat, pltpu.VMEM((ACC_ROWS, tq), F32), stat, pltpu.VMEM((ACC_ROWS, tq), F32),
                        pltpu.VMEM((tk, tq), F32), pltpu.VMEM((tk, tq), F32), pltpu.VMEM((tk, tq), F32),
                        stat, stat, stat,
                        pltpu.VMEM((tk, tq), BF16), pltpu.VMEM((tk, tq), BF16), pltpu.VMEM((tk, tq), BF16),
                        stat, stat, stat],
        compiler_params=pltpu.CompilerParams(dimension_semantics=("arbitrary", "arbitrary"),
                                             vmem_limit_bytes=VMEM_LIMIT),
        name="attention",
    )(qt, qt, k, vt)


def _ssd_out_kernel(xs_ref, bm_ref, cm_ref, dt_ref, gz_ref, alog_ref, dskip_ref, g_ref, e2_ref,
                    oa_ref, gza_ref, x_ref, w32_ref, mod_ref, lng_ref, lnb_ref,
                    o_ref, state, y_scr, w_ref, os_scr, *, rows, nheads, alpha, sub):
    i = pl.program_id(0)
    gw = SSM_STATE
    hw = state.shape[2]
    na = oa_ref.shape[1]

    @pl.when(i == 0)
    def _():
        state[...] = jnp.zeros(state.shape, F32)
        for r in range(0, w_ref.shape[0], sub):
            w_ref[r:r + sub, :] = w32_ref[r:r + sub, :].astype(BF16)

    mixes = []
    for r in range(0, rows, sub):
        oa = (oa_ref[r:r + sub, :].astype(F32) * gza_ref[r:r + sub, :].astype(F32)).astype(BF16)
        mixes.append(jnp.dot(oa, w_ref[0:na, :], preferred_element_type=F32))

    a_neg = -jnp.exp(alog_ref[...])
    r_i = lax.broadcasted_iota(jnp.int32, (CHUNK, CHUNK), 0)
    c_i = lax.broadcasted_iota(jnp.int32, (CHUNK, CHUNK), 1)
    tri = c_i <= r_i
    tri_bf = tri.astype(BF16)
    lane = lax.broadcasted_iota(jnp.int32, (CHUNK, LANES), 1)
    head_lo = lane < SSM_HEAD_DIM
    first_copy = lax.broadcasted_iota(jnp.int32, (rows, LANES), 1) < nheads
    e2 = e2_ref[...]

    def expand(v):
        hi = v.astype(BF16)
        lo = (v - hi.astype(F32)).astype(BF16)
        return jnp.dot(jnp.where(first_copy, hi, lo), e2, preferred_element_type=F32)

    dt_all = dt_ref[...]
    da_all = dt_all * (a_neg * LOG2E)
    a_cums = []
    for cidx in range(rows // CHUNK):
        a_cum = jnp.zeros((CHUNK, LANES), F32)
        for part in _split_bf16(da_all[cidx * CHUNK:(cidx + 1) * CHUNK, :], 3):
            a_cum = a_cum + jnp.dot(tri_bf, part, preferred_element_type=F32)
        a_cums.append(a_cum)
    dt_x_all = expand(dt_all)
    ea_x_all = expand(jnp.exp2(jnp.concatenate(a_cums, axis=0)))
    dte_x_all = expand(jnp.exp2(jnp.concatenate([a[CHUNK - 1:CHUNK, :] - a for a in a_cums], axis=0)))

    for cidx in range(rows // CHUNK):
        sl = pl.ds(cidx * CHUNK, CHUNK)
        rs = slice(cidx * CHUNK, (cidx + 1) * CHUNK)
        a_cum = a_cums[cidx]
        a_cum_t = a_cum.T
        ea_x = ea_x_all[rs, :]

        xs = xs_ref[sl, :].astype(F32)
        xd = xs * dt_x_all[rs, :]
        xdd = (xd * dte_x_all[rs, :]).astype(BF16)
        xd_bf = xd.astype(BF16)

        for g in range(SSM_GROUPS):
            bg = bm_ref[sl, g * gw:(g + 1) * gw]
            cg = cm_ref[sl, g * gw:(g + 1) * gw]
            cb = lax.dot_general(cg, bg, (((1,), (1,)), ((), ())), preferred_element_type=F32)
            prev = state[g]
            y_off = jnp.dot(cg, prev.astype(BF16), preferred_element_type=F32)
            new = lax.dot_general(bg, xdd[:, g * hw:(g + 1) * hw], (((0,), (0,)), ((), ())),
                                  preferred_element_type=F32)
            state[g] = prev * ea_x[CHUNK - 1:CHUNK, g * hw:(g + 1) * hw] + new
            y_scr[:, g * hw:(g + 1) * hw] = y_off * ea_x[:, g * hw:(g + 1) * hw]

            hpg = hw // SSM_HEAD_DIM
            for pair in range(hpg // 2):
                h0 = g * hpg + 2 * pair
                c0 = h0 * SSM_HEAD_DIM
                xp = xd_bf[:, c0:c0 + LANES]
                yp = jnp.zeros((CHUNK, LANES), F32)
                for k, keep in ((0, head_lo), (1, ~head_lo)):
                    h = h0 + k
                    seg = a_cum[:, h:h + 1] - a_cum_t[h:h + 1, :]
                    m_h = (cb * jnp.exp2(jnp.where(tri, seg, NEG_BIG))).astype(BF16)
                    yp = yp + jnp.dot(m_h, jnp.where(keep, xp, jnp.zeros_like(xp)),
                                      preferred_element_type=F32)
                y_scr[:, c0:c0 + LANES] = y_scr[:, c0:c0 + LANES] + yp

        y = y_scr[...] + xs * dskip_ref[...]
        hf = y * gz_ref[sl, :].astype(F32)
        for g in range(SSM_GROUPS):
            hg = hf[:, g * hw:(g + 1) * hw]
            ms = jnp.mean(hg * hg, axis=-1, keepdims=True)
            os_scr[sl, g * hw:(g + 1) * hw] = (hg * lax.rsqrt(ms + RMS_EPS)
                                                * g_ref[:, g * hw:(g + 1) * hw]).astype(os_scr.dtype)

    for n, mixed in enumerate(mixes):
        rws = slice(n * sub, (n + 1) * sub)
        mixed = mixed + jnp.dot(os_scr[rws, :], w_ref[na:, :], preferred_element_type=F32)
        y = alpha * x_ref[rws, :] + mod_ref[2:3, :] * mixed
        mu = jnp.mean(y, axis=-1, keepdims=True)
        yc = y - mu
        var = jnp.mean(yc * yc, axis=-1, keepdims=True)
        o_ref[rws, :] = yc * lax.rsqrt(var + LN_EPS) * lng_ref[...] + lnb_ref[...]


def _ssd_out(xs, bm, cm, dt, gzs, alog, dskip_x, norm_g, e2, oa, gza, x2, w_out, mod3, ln_g, ln_b,
             *, rows, sub, nheads, alpha):
    s, nx = xs.shape
    d = x2.shape[1]
    hw = nx // SSM_GROUPS
    row = lambda a: pl.BlockSpec((rows, a.shape[1]), lambda i: (i, 0))
    full = lambda a: pl.BlockSpec(a.shape, lambda i: (0,) * a.ndim)
    return pl.pallas_call(
        functools.partial(_ssd_out_kernel, rows=rows, nheads=nheads, alpha=alpha, sub=sub),
        grid=(s // rows,),
        in_specs=[row(xs), row(bm), row(cm), row(dt), row(gzs),
                  full(alog), full(dskip_x), full(norm_g), full(e2),
                  row(oa), row(gza), row(x2),
                  pl.BlockSpec(w_out.shape, lambda i: (0, 0), pipeline_mode=pl.Buffered(1)),
                  full(mod3), full(ln_g), full(ln_b)],
        out_specs=pl.BlockSpec((rows, d), lambda i: (i, 0)),
        out_shape=jax.ShapeDtypeStruct((s, d), F32),
        scratch_shapes=[pltpu.VMEM((SSM_GROUPS, SSM_STATE, hw), F32),
                        pltpu.VMEM((CHUNK, nx), F32),
                        pltpu.VMEM(w_out.shape, BF16),
                        pltpu.VMEM((rows, nx), BF16)],
        compiler_params=pltpu.CompilerParams(dimension_semantics=("arbitrary",),
                                             vmem_limit_bytes=VMEM_LIMIT),
        name="ssd_out",
    )(xs, bm, cm, dt, gzs, alog, dskip_x, norm_g, e2, oa, gza, x2, w_out, mod3, ln_g, ln_b)


def _pad_cols(a, width):
    return jnp.pad(a, ((0, 0), (0, width - a.shape[1])))


def _regroup_kernel(wt_ref, o_ref, *, segments):
    wt = wt_ref[...]
    parts = [jnp.zeros((b, wt.shape[1]), wt.dtype) if a is None else wt[a:b, :] for a, b in segments]
    o_ref[...] = jnp.concatenate(parts, axis=0).T.astype(o_ref.dtype)


def _regroup_columns(w, segments, width):
    assert all(b % SUBLANES == 0 and (a or 0) % SUBLANES == 0 for a, b in segments)
    rows = w.shape[0]
    br = 256
    return pl.pallas_call(
        functools.partial(_regroup_kernel, segments=segments),
        grid=(rows // br,),
        in_specs=[pl.BlockSpec((w.shape[1], br), lambda i: (0, i))],
        out_specs=pl.BlockSpec((br, width), lambda i: (i, 0)),
        out_shape=jax.ShapeDtypeStruct((rows, width), BF16),
        compiler_params=pltpu.CompilerParams(vmem_limit_bytes=VMEM_LIMIT),
        name="regroup_w_in",
    )(w.T)


def _layer(x2, c, pos_row, w_ada, b_ada, w_in, q_norm_g, w_qb, kv_norm_g, w_kvb,
           conv_w, conv_b, dt_bias, a_log, d_skip, ssm_norm_g, w_out, ln_g, ln_b, *, depth):
    s, d = x2.shape
    q_rank = q_norm_g.shape[0]
    kv_rank = kv_norm_g.shape[0]
    nheads = dt_bias.shape[0]
    nch = conv_w.shape[1]
    nv = MLA_HEADS * V_HEAD_DIM
    nx = nheads * SSM_HEAD_DIM
    half = QK_ROPE_DIM // 2

    mod = _adaln_mod(c.reshape(d, 1), w_ada, b_ada.reshape(1, -1))
    mod3 = mod.reshape(3, d)

    o_q, o_ckv = 0, q_rank
    o_kr = o_ckv + kv_rank
    o_za = o_kr + QK_ROPE_DIM
    o_xbc = o_za + nv
    o_dt = o_xbc + nch
    o_zs = o_dt + nheads
    dt_pad = LANES - 2 * nheads
    groups = [("q", [(o_q, o_ckv)]), ("ckv", [(o_ckv, o_kr)]),
              ("krope", [(o_kr, o_za), (o_kr + half, o_za), (o_kr, o_kr + half)]),
              ("za", [(o_za, o_xbc)]), ("xbc", [(o_xbc, o_dt)]), ("zs", [(o_zs, o_zs + nx)]),
              ("dt", [(o_dt, o_zs), (o_dt, o_zs), (None, dt_pad)])]
    off, cur, segments = {}, 0, []
    for name, segs in groups:
        assert cur % LANES == 0
        off[name] = cur
        for a, b in segs:
            cur += b if a is None else b - a
            if segments and a is not None and segments[-1][0] is not None and segments[-1][1] == a:
                segments[-1] = (segments[-1][0], b)
            else:
                segments.append((a, b))
    assert cur % LANES == 0
    w_in_p = _regroup_columns(w_in, tuple(segments), cur)

    w_qb_p = w_qb.T.astype(BF16)

    inv_freq = 1.0 / (ROPE_THETA ** (jnp.arange(half, dtype=F32) / half))
    rope_tab = jnp.broadcast_to(inv_freq[:, None], (half, IN_SUB))

    dtb = _pad_cols(jnp.concatenate([dt_bias, dt_bias]).reshape(1, -1), LANES)
    q_scale = (QK_NOPE_DIM + QK_ROPE_DIM) ** -0.5 * LOG2E

    wkv3 = w_kvb.reshape(kv_rank, MLA_HEADS, QK_NOPE_DIM + V_HEAD_DIM)
    w_k = wkv3[:, :, :QK_NOPE_DIM].reshape(kv_rank, MLA_HEADS * QK_NOPE_DIM).astype(BF16)
    w_vt = wkv3[:, :, QK_NOPE_DIM:].reshape(kv_rank, nv).T.astype(BF16)

    q, k, vt, gza, xs, bm, cm, dt, gzs = _in_proj(
        x2, mod3, pos_row, rope_tab, w_in_p, q_norm_g.reshape(1, -1), w_qb_p,
        kv_norm_g.reshape(1, -1), w_k, w_vt, conv_w, conv_b.reshape(1, -1), dtb,
        off=off, tm=IN_ROWS, sub=IN_SUB, q_scale=q_scale)

    o_attn = _attention(q, k, vt, tq=ATTN_TILE, tk=ATTN_TILE, ntile=ATTN_TILES_PER_STEP)

    alog = _pad_cols(jnp.concatenate([a_log, a_log]).reshape(1, -1), LANES)
    e_head = jnp.repeat(jnp.eye(nheads, dtype=BF16), SSM_HEAD_DIM, axis=1)
    e2 = jnp.pad(jnp.concatenate([e_head, e_head], axis=0), ((0, LANES - 2 * nheads), (0, 0)))
    dskip_x = jnp.repeat(d_skip, SSM_HEAD_DIM).reshape(1, -1)
    alpha = (2.0 * depth) ** 0.25
    return _ssd_out(xs, bm, cm, dt, gzs, alog, dskip_x, ssm_norm_g.reshape(1, -1), e2,
                    o_attn, gza, x2, w_out, mod3, ln_g.reshape(1, -1), ln_b.reshape(1, -1),
                    rows=SSD_ROWS, sub=OUT_SUB, nheads=nheads, alpha=alpha)


def kernel(x, c, positions, w_ada, b_ada, w_in, q_norm_g, w_qb, kv_norm_g, w_kvb, conv_w, conv_b,
           dt_bias, a_log, d_skip, ssm_norm_g, w_out, ln_g, ln_b):
    b, s, d = x.shape
    depth = w_in.shape[0]
    assert b == 1, "one sequence per call"
    h = x.reshape(s, d)
    pos_row = positions.reshape(1, s)
    for l in range(depth):
        h = _layer(h, c, pos_row, w_ada[l], b_ada[l], w_in[l], q_norm_g[l], w_qb[l], kv_norm_g[l],
                   w_kvb[l], conv_w[l], conv_b[l], dt_bias[l], a_log[l], d_skip[l], ssm_norm_g[l],
                   w_out[l], ln_g[l], ln_b[l], depth=depth)
    return h.reshape(b, s, d)
```

```python
import functools
import math

import jax
import jax.numpy as jnp
from jax import lax
from jax.experimental import pallas as pl
from jax.experimental.pallas import tpu as pltpu

F32 = jnp.float32
BF16 = jnp.bfloat16

MLA_HEADS = 8
QK_NOPE_DIM = 128
QK_ROPE_DIM = 64
V_HEAD_DIM = 128
ROPE_THETA = 10000.0
SSM_HEAD_DIM = 64
SSM_GROUPS = 2
SSM_STATE = 128
CHUNK = 128
RMS_EPS = 1e-6
LN_EPS = 1e-5

LANES = 128
SUBLANES = 8
MXU_DIM = 256
QK_PAD = MXU_DIM
VMEM_LIMIT = 56 * 1024 * 1024

IN_ROWS = 512
IN_SUB = 256
OUT_SUB = 256
ATTN_TILE = 1024
ATTN_TILES_PER_STEP = 2
SSD_ROWS = 512

LOG2E = 1.4426950408889634
NEG_BIG = -1e30


def _silu(z):
    h = 0.5 * z
    return h * jnp.tanh(h) + h


def _softplus(z):
    return jnp.maximum(z, 0.0) + jnp.log1p(jnp.exp(-jnp.abs(z)))


def _split_bf16(x, parts):
    out, rem = [], x
    for _ in range(parts):
        hi = rem.astype(BF16)
        out.append(hi)
        rem = rem - hi.astype(F32)
    return out


def _mod_kernel(c_ref, w_ref, b_ref, o_ref):
    o_ref[...] = jnp.sum(w_ref[...] * c_ref[...], axis=0, keepdims=True) + b_ref[...]


def _adaln_mod(c_col, w_ada, b_ada):
    d, n = w_ada.shape
    bn = 1024
    return pl.pallas_call(
        _mod_kernel,
        grid=(n // bn,),
        in_specs=[pl.BlockSpec((d, 1), lambda j: (0, 0)),
                  pl.BlockSpec((d, bn), lambda j: (0, j)),
                  pl.BlockSpec((1, bn), lambda j: (0, j))],
        out_specs=pl.BlockSpec((1, bn), lambda j: (0, j)),
        out_shape=jax.ShapeDtypeStruct((1, n), F32),
        name="adaln_mod",
    )(c_col, w_ada, b_ada)


def _in_kernel(x_ref, mod_ref, pos_ref, rope_ref, w_in_ref, qg_ref, w_qb_ref, kvg_ref, w_k_ref, w_vt_ref,
               conv_w_ref, conv_b_ref, dtb_ref,
               q_out, k_out, vt_out, gza_out, xs_out, bm_out, cm_out, dt_out, gzs_out,
               xbuf, *, tm, sub, off, q_scale):
    i = pl.program_id(0)
    shift = mod_ref[0:1, :]
    scale1 = 1.0 + mod_ref[1:2, :]
    lane = lax.broadcasted_iota(jnp.int32, (sub, LANES), 1)
    low_half = lane < QK_ROPE_DIM
    nconv, nch = conv_w_ref.shape
    nx = xs_out.shape[1]
    nb = bm_out.shape[1]

    @pl.when(i == 0)
    def _():
        xbuf[0:SUBLANES, :] = jnp.zeros((SUBLANES, nch), F32)

    css, cos_ts, sin_ts = [], [], []
    for h in range(tm // sub):
        ang_t = pos_ref[:, h * sub:(h + 1) * sub].astype(F32) * rope_ref[...]
        cos_t, sin_t = jnp.cos(ang_t), jnp.sin(ang_t)
        cos_ts.append(cos_t)
        sin_ts.append(sin_t)
        css.append(jnp.concatenate([cos_t, cos_t, -sin_t, sin_t], axis=0).T)

    projs = []
    for h in range(tm // sub):
        u = x_ref[h * sub:(h + 1) * sub, :] * scale1 + shift
        projs.append(jnp.dot(u.astype(BF16), w_in_ref[...], preferred_element_type=F32))

    for h, (proj, cs) in enumerate(zip(projs, css)):
        rows = slice(h * sub, (h + 1) * sub)

        def rope(rr, cs=cs):
            t = rr * cs
            return jnp.where(low_half, t + pltpu.roll(t, QK_ROPE_DIM, axis=1), 0.0)

        q_lat = proj[:, off["q"]:off["q"] + qg_ref.shape[1]]
        qn = q_lat * lax.rsqrt(jnp.mean(q_lat * q_lat, axis=-1, keepdims=True) + RMS_EPS) * qg_ref[...]
        qft = lax.dot_general(w_qb_ref[...], qn.astype(BF16), (((1,), (1,)), ((), ())),
                              preferred_element_type=F32) * q_scale
        half = QK_ROPE_DIM // 2
        hd_rows = QK_NOPE_DIM + QK_ROPE_DIM
        cos_t, sin_t = cos_ts[h], sin_ts[h]
        for hd in range(MLA_HEADS):
            r0, o0 = hd * hd_rows, hd * QK_PAD
            x1 = qft[r0 + QK_NOPE_DIM:r0 + QK_NOPE_DIM + half, :]
            x2 = qft[r0 + QK_NOPE_DIM + half:r0 + hd_rows, :]
            q_out[o0:o0 + QK_NOPE_DIM, rows] = qft[r0:r0 + QK_NOPE_DIM, :].astype(BF16)
            q_out[o0 + QK_NOPE_DIM:o0 + QK_NOPE_DIM + half, rows] = (x1 * cos_t - x2 * sin_t).astype(BF16)
            q_out[o0 + QK_NOPE_DIM + half:o0 + hd_rows, rows] = (x2 * cos_t + x1 * sin_t).astype(BF16)
            q_out[o0 + hd_rows:o0 + QK_PAD, rows] = jnp.zeros((QK_PAD - hd_rows, sub), BF16)

        c_kv = proj[:, off["ckv"]:off["ckv"] + kvg_ref.shape[1]]
        ckvn = c_kv * lax.rsqrt(jnp.mean(c_kv * c_kv, axis=-1, keepdims=True) + RMS_EPS) * kvg_ref[...]
        ckvn_bf = ckvn.astype(BF16)
        kf = jnp.dot(ckvn_bf, w_k_ref[...], preferred_element_type=F32)
        vt_out[:, rows] = lax.dot_general(w_vt_ref[...], ckvn_bf, (((1,), (1,)), ((), ())),
                                          preferred_element_type=F32).astype(BF16)
        k_rope = rope(proj[:, off["krope"]:off["krope"] + LANES]).astype(BF16)
        for hd in range(MLA_HEADS):
            k_out[rows, hd * QK_PAD:hd * QK_PAD + QK_NOPE_DIM] = (
                kf[:, hd * QK_NOPE_DIM:(hd + 1) * QK_NOPE_DIM].astype(BF16))
            k_out[rows, hd * QK_PAD + QK_NOPE_DIM:(hd + 1) * QK_PAD] = k_rope

        gza_out[rows, :] = _silu(proj[:, off["za"]:off["za"] + gza_out.shape[1]]).astype(BF16)
        gzs_out[rows, :] = _silu(proj[:, off["zs"]:off["zs"] + gzs_out.shape[1]]).astype(BF16)
        dt_out[rows, :] = _softplus(proj[:, off["dt"]:off["dt"] + LANES] + dtb_ref[...])

        r1 = SUBLANES + h * sub
        xbuf[r1:r1 + sub, :] = proj[:, off["xbc"]:off["xbc"] + nch]
        xfull = xbuf[r1 - SUBLANES:r1 + sub, :]
        acc = conv_b_ref[...] + conv_w_ref[nconv - 1:nconv, :] * xfull[SUBLANES:, :]
        for k in range(1, nconv):
            shifted = pltpu.roll(xfull, k, axis=0)[SUBLANES:, :]
            acc = acc + conv_w_ref[nconv - 1 - k:nconv - k, :] * shifted
        xc = _silu(acc)
        xs_out[rows, :] = xc[:, :nx].astype(BF16)
        bm_out[rows, :] = xc[:, nx:nx + nb].astype(BF16)
        cm_out[rows, :] = xc[:, nx + nb:nx + 2 * nb].astype(BF16)

    xbuf[0:SUBLANES, :] = xbuf[tm:tm + SUBLANES, :]


def _in_proj(x2, mod3, pos_row, rope_tab, w_in_p, qg, w_qb_p, kvg, w_k, w_vt, conv_w, conv_b, dtb,
             *, off, tm, sub, q_scale):
    s, d = x2.shape
    nq = MLA_HEADS * QK_PAD
    nv = MLA_HEADS * V_HEAD_DIM
    nch = conv_w.shape[1]
    nbc = SSM_GROUPS * SSM_STATE
    nx = nch - 2 * nbc
    row = lambda w: pl.BlockSpec((tm, w), lambda i: (i, 0))
    full = lambda a: pl.BlockSpec(a.shape, lambda i: (0,) * a.ndim, pipeline_mode=pl.Buffered(1))
    outs = [
        jax.ShapeDtypeStruct((nq, s), BF16),
        jax.ShapeDtypeStruct((s, nq), BF16),
        jax.ShapeDtypeStruct((nv, s), BF16),
        jax.ShapeDtypeStruct((s, nv), BF16),
        jax.ShapeDtypeStruct((s, nx), BF16),
        jax.ShapeDtypeStruct((s, nbc), BF16),
        jax.ShapeDtypeStruct((s, nbc), BF16),
        jax.ShapeDtypeStruct((s, LANES), F32),
        jax.ShapeDtypeStruct((s, nx), BF16),
    ]
    return pl.pallas_call(
        functools.partial(_in_kernel, tm=tm, sub=sub, off=off, q_scale=q_scale),
        grid=(s // tm,),
        in_specs=[row(d), full(mod3), pl.BlockSpec((1, tm), lambda i: (0, i)), full(rope_tab),
                  full(w_in_p), full(qg), full(w_qb_p), full(kvg), full(w_k), full(w_vt),
                  full(conv_w), full(conv_b), full(dtb)],
        out_specs=[pl.BlockSpec((o.shape[0], tm), lambda i: (0, i)) if n in (0, 2) else row(o.shape[1])
                   for n, o in enumerate(outs)],
        out_shape=outs,
        scratch_shapes=[pltpu.VMEM((tm + 2 * SUBLANES, nch), F32)],
        compiler_params=pltpu.CompilerParams(dimension_semantics=("arbitrary",),
                                             vmem_limit_bytes=VMEM_LIMIT),
        name="in_proj",
    )(x2, mod3, pos_row, rope_tab, w_in_p, qg, w_qb_p, kvg, w_k, w_vt, conv_w, conv_b, dtb)


def _sublane_allmax(x):
    shift = SUBLANES // 2
    while shift:
        x = jnp.maximum(x, pltpu.roll(x, shift, axis=0))
        shift //= 2
    return x


def _sublane_allsum(x):
    shift = SUBLANES // 2
    while shift:
        x = x + pltpu.roll(x, shift, axis=0)
        shift //= 2
    return x


BF16_ROWS = 16
ACC_ROWS = V_HEAD_DIM + BF16_ROWS
EXP_ROWS = 32


def _attn_kernel(q_ref, qn_ref, k_ref, vt_ref, o_ref, m_a, acc_a, m_b, acc_b,
                 s_a, s_b, s_c, cmax_a, cmax_b, cmax_c, p_a, p_b, p_c, al_a, al_b, al_c,
                 *, tq, tk, ntile, n_tiles):
    assert tq == tk
    g = pl.program_id(1)
    nsub = tk // SUBLANES
    nacc = ACC_ROWS // SUBLANES
    buf_a = (s_a, cmax_a, p_a, al_a)
    buf_b = (s_b, cmax_b, p_b, al_b)
    buf_c = (s_c, cmax_c, p_c, al_c)
    ones_rows = jnp.ones((ACC_ROWS - V_HEAD_DIM, tk), BF16)

    def reset(stats):
        m_scr, acc_scr = stats
        m_scr[...] = jnp.full(m_scr.shape, NEG_BIG, F32)
        acc_scr[...] = jnp.zeros(acc_scr.shape, F32)

    def qk_stage(kv, q_rows, buf, diag=False, q_ref=q_ref):
        s_out, cmax_out = buf[0], buf[1]
        start = pl.multiple_of(kv * tk, tk)
        colmax = lambda x: _sublane_allmax(jnp.max(x.reshape(x.shape[0] // SUBLANES, SUBLANES, x.shape[1]), axis=0))
        if not diag:
            st = jnp.dot(k_ref[pl.ds(start, tk), :], q_ref[:, q_rows],
                         preferred_element_type=F32)
            s_out[...] = st
            cmax_out[...] = colmax(st)
            return
        hk = tk // 2
        q_lo = q_rows.start
        mask = (lax.broadcasted_iota(jnp.int32, (hk, tq), 0) <= lax.broadcasted_iota(jnp.int32, (hk, tq), 1))
        top = jnp.dot(k_ref[pl.ds(start, hk), :], q_ref[:, q_rows], preferred_element_type=F32)
        top = jnp.where(mask, top, NEG_BIG)
        bot = jnp.dot(k_ref[pl.ds(start + hk, hk), :], q_ref[:, q_lo + hk:q_lo + tq],
                      preferred_element_type=F32)
        bot = jnp.where(mask[:, :tq - hk], bot, NEG_BIG)
        s_out[0:hk, :] = top
        s_out[hk:tk, 0:hk] = jnp.full((tk - hk, hk), NEG_BIG, F32)
        s_out[hk:tk, hk:tq] = bot
        cm_top = colmax(top)
        cmax_out[:, 0:hk] = cm_top[:, 0:hk]
        cmax_out[:, hk:tq] = jnp.maximum(cm_top[:, hk:tq], colmax(bot))

    def softmax_stage(buf, stats):
        s_in, cmax_in, p_out, al_out = buf
        m_scr = stats[0]
        m_old = m_scr[...]
        m_new = jnp.maximum(m_old, cmax_in[...])
        al_out[...] = jnp.exp2(m_old - m_new)
        m_scr[...] = m_new
        for c in range(0, tq, MXU_DIM):
            m_c = m_new[:, c:c + MXU_DIM][None]
            for r in range(0, tk, EXP_ROWS):
                sc = s_in[r:r + EXP_ROWS, c:c + MXU_DIM].reshape(EXP_ROWS // SUBLANES, SUBLANES, MXU_DIM)
                p_out[r:r + EXP_ROWS, c:c + MXU_DIM] = (
                    jnp.exp2(sc - m_c).reshape(EXP_ROWS, MXU_DIM).astype(BF16))

    def pv_stage(kv, buf, stats):
        p_in, al_in = buf[2], buf[3]
        acc_scr = stats[1]
        start = pl.multiple_of(jnp.maximum(kv, 0) * tk, tk)
        vt_aug = jnp.concatenate([vt_ref[:, pl.ds(start, tk)], ones_rows], axis=0)
        pv = jnp.dot(vt_aug, p_in[...], preferred_element_type=F32)
        acc3 = acc_scr[...].reshape(nacc, SUBLANES, tq) * al_in[...][None]
        acc_scr[...] = acc3.reshape(ACC_ROWS, tq) + pv

    def finalize(q_rows, stats):
        acc_scr = stats[1]
        inv_l = 1.0 / acc_scr[V_HEAD_DIM:V_HEAD_DIM + SUBLANES, :]
        o_t = acc_scr[0:V_HEAD_DIM, :].reshape(V_HEAD_DIM // SUBLANES, SUBLANES, tq) * inv_l[None]
        o_ref[q_rows, :] = o_t.reshape(V_HEAD_DIM, tq).T.astype(o_ref.dtype)

    def make_step(qi, q_rows, stats):
        kv_of = lambda pos: jnp.where(pos == 0, qi, pos - 1)

        def step(pos, buf, other):
            qk_stage(kv_of(pos), q_rows, buf)
            pv_stage(kv_of(pos - 2), buf, stats)
            softmax_stage(other, stats)

        return kv_of, step

    def run_pairs(lo, hi, pair):
        def body(pp, carry):
            pair(pp)
            return carry

        lax.fori_loop(lo, hi, body, 0)

    bufs = (buf_a, buf_b)
    stats2 = ((m_a, acc_a), (m_b, acc_b))
    start = [0]
    for j in range(ntile - 1):
        last = start[j] if j % 2 == 0 else 1 - start[j]
        start.append(1 - last)

    def tile_ctx(j):
        qi = ntile * g + j
        rows = slice(j * tq, (j + 1) * tq)
        stats = stats2[j % 2]
        x, y = bufs[start[j]], bufs[1 - start[j]]
        kv_of, step = make_step(qi, rows, stats)
        return qi, rows, stats, x, y, kv_of, step

    def run_steps(j, first_pos, n_pairs):
        _, _, _, x, y, _, step = tile_ctx(j)
        even_buf, odd_buf = x, y

        def pair(pp):
            p0 = first_pos + 2 * pp
            if first_pos % 2 == 0:
                step(p0, even_buf, odd_buf)
                step(p0 + 1, odd_buf, even_buf)
            else:
                step(p0, odd_buf, even_buf)
                step(p0 + 1, even_buf, odd_buf)

        run_pairs(0, n_pairs, pair)

    def transition(j, last_buf, other):
        qi, rows, stats, _, _, kv_of, _ = tile_ctx(j)
        qi_n, rows_n, st_n, x_n, y_n, kv_n, _ = tile_ctx(j + 1)
        reset(st_n)
        qk_stage(qi_n, rows_n, x_n, diag=True)
        if other is not None:
            assert x_n is other
            pv_stage(kv_of(qi - 1), other, stats)
        softmax_stage(last_buf, stats)
        qk_stage(kv_n(1), rows_n, y_n)
        pv_stage(kv_of(qi), last_buf, stats)
        softmax_stage(x_n, st_n)
        finalize(rows, stats)

    qi0, rows0, st0, x0, y0, kv0, step0 = tile_ctx(0)
    reset(st0)

    @pl.when(g == 0)
    def _():
        qk_stage(qi0, rows0, buf_c, diag=True)
        transition(0, buf_c, None)

    @pl.when(g > 0)
    def _():
        qk_stage(kv0(1), rows0, y0)
        softmax_stage(buf_c, st0)
        qk_stage(kv0(2), rows0, x0)
        pv_stage(kv0(0), buf_c, st0)
        softmax_stage(y0, st0)
        run_steps(0, 3, (ntile * g - 2) // 2)
        transition(0, x0, y0)

    for j in range(1, ntile):
        qi, rows, stats, x, y, kv_of, step = tile_ctx(j)
        if j % 2 == 0:
            step(2, x, y)
            run_steps(j, 3, (ntile * g + j - 2) // 2)
        else:
            run_steps(j, 2, (ntile * g + j - 1) // 2)
        last_buf, other = (x, y) if j % 2 == 0 else (y, x)
        if j + 1 < ntile:
            transition(j, last_buf, other)
        else:
            qk_stage(jnp.minimum(qi + 1, n_tiles - 1), slice(0, tq), buf_c, diag=True, q_ref=qn_ref)
            pv_stage(kv_of(qi - 1), other, stats)
            softmax_stage(last_buf, stats)
            pv_stage(kv_of(qi), last_buf, stats)
            finalize(rows, stats)


def _attention(qt, k, vt, *, tq, tk, ntile):
    s = k.shape[0]
    assert ntile % 2 == 0 and s % (ntile * tq) == 0, "sequence length must be a multiple of ntile query tiles"
    stat = pltpu.VMEM((SUBLANES, tq), F32)
    n_tiles = s // tq
    return pl.pallas_call(
        functools.partial(_attn_kernel, tq=tq, tk=tk, ntile=ntile, n_tiles=n_tiles),
        grid=(MLA_HEADS, s // (ntile * tq)),
        in_specs=[pl.BlockSpec((QK_PAD, ntile * tq), lambda h, i: (h, i)),
                  pl.BlockSpec((QK_PAD, tq), lambda h, i: (h, jnp.minimum(ntile * (i + 1), n_tiles - 1))),
                  pl.BlockSpec((s, QK_PAD), lambda h, i: (0, h)),
                  pl.BlockSpec((V_HEAD_DIM, s), lambda h, i: (h, 0))],
        out_specs=pl.BlockSpec((ntile * tq, V_HEAD_DIM), lambda h, i: (i, h)),
        out_shape=jax.ShapeDtypeStruct((s, MLA_HEADS * V_HEAD_DIM), BF16),
        scratch_shapes=[stat, pltpu.VMEM((ACC_ROWS, tq), F32), stat, pltpu.VMEM((ACC_ROWS, tq), F32),
                        pltpu.VMEM((tk, tq), F32), pltpu.VMEM((tk, tq), F32), pltpu.VMEM((tk, tq), F32),
                        stat, stat, stat,
                        pltpu.VMEM((tk, tq), BF16), pltpu.VMEM((tk, tq), BF16), pltpu.VMEM((tk, tq), BF16),
                        stat, stat, stat],
        compiler_params=pltpu.CompilerParams(dimension_semantics=("arbitrary", "arbitrary"),
                                             vmem_limit_bytes=VMEM_LIMIT),
        name="attention",
    )(qt, qt, k, vt)


def _ssd_out_kernel(xs_ref, bm_ref, cm_ref, dt_ref, gz_ref, alog_ref, dskip_ref, g_ref, e2_ref,
                    oa_ref, gza_ref, x_ref, w32_ref, mod_ref, lng_ref, lnb_ref,
                    o_ref, state, y_scr, w_ref, os_scr, *, rows, nheads, alpha, sub):
    i = pl.program_id(0)
    gw = SSM_STATE
    hw = state.shape[2]
    na = oa_ref.shape[1]

    @pl.when(i == 0)
    def _():
        state[...] = jnp.zeros(state.shape, F32)
        for r in range(0, w_ref.shape[0], sub):
            w_ref[r:r + sub, :] = w32_ref[r:r + sub, :].astype(BF16)

    mixes = []
    for r in range(0, rows, sub):
        oa = (oa_ref[r:r + sub, :].astype(F32) * gza_ref[r:r + sub, :].astype(F32)).astype(BF16)
        mixes.append(jnp.dot(oa, w_ref[0:na, :], preferred_element_type=F32))

    a_neg = -jnp.exp(alog_ref[...])
    r_i = lax.broadcasted_iota(jnp.int32, (CHUNK, CHUNK), 0)
    c_i = lax.broadcasted_iota(jnp.int32, (CHUNK, CHUNK), 1)
    tri = c_i <= r_i
    tri_bf = tri.astype(BF16)
    lane = lax.broadcasted_iota(jnp.int32, (CHUNK, LANES), 1)
    head_lo = lane < SSM_HEAD_DIM
    first_copy = lax.broadcasted_iota(jnp.int32, (rows, LANES), 1) < nheads
    e2 = e2_ref[...]

    def expand(v):
        hi = v.astype(BF16)
        lo = (v - hi.astype(F32)).astype(BF16)
        return jnp.dot(jnp.where(first_copy, hi, lo), e2, preferred_element_type=F32)

    dt_all = dt_ref[...]
    da_all = dt_all * (a_neg * LOG2E)
    a_cums = []
    for cidx in range(rows // CHUNK):
        a_cum = jnp.zeros((CHUNK, LANES), F32)
        for part in _split_bf16(da_all[cidx * CHUNK:(cidx + 1) * CHUNK, :], 3):
            a_cum = a_cum + jnp.dot(tri_bf, part, preferred_element_type=F32)
        a_cums.append(a_cum)
    dt_x_all = expand(dt_all)
    ea_x_all = expand(jnp.exp2(jnp.concatenate(a_cums, axis=0)))
    dte_x_all = expand(jnp.exp2(jnp.concatenate([a[CHUNK - 1:CHUNK, :] - a for a in a_cums], axis=0)))

    for cidx in range(rows // CHUNK):
        sl = pl.ds(cidx * CHUNK, CHUNK)
        rs = slice(cidx * CHUNK, (cidx + 1) * CHUNK)
        a_cum = a_cums[cidx]
        a_cum_t = a_cum.T
        ea_x = ea_x_all[rs, :]

        xs = xs_ref[sl, :].astype(F32)
        xd = xs * dt_x_all[rs, :]
        xdd = (xd * dte_x_all[rs, :]).astype(BF16)
        xd_bf = xd.astype(BF16)

        for g in range(SSM_GROUPS):
            bg = bm_ref[sl, g * gw:(g + 1) * gw]
            cg = cm_ref[sl, g * gw:(g + 1) * gw]
            bg_t = bg.astype(F32).T.astype(BF16)
            cb = jnp.dot(cg, bg_t, preferred_element_type=F32)
            prev = state[g]
            y_off = jnp.dot(cg, prev.astype(BF16), preferred_element_type=F32)
            new = jnp.dot(bg_t, xdd[:, g * hw:(g + 1) * hw], preferred_element_type=F32)
            state[g] = prev * ea_x[CHUNK - 1:CHUNK, g * hw:(g + 1) * hw] + new
            y_scr[:, g * hw:(g + 1) * hw] = y_off * ea_x[:, g * hw:(g + 1) * hw]

            hpg = hw // SSM_HEAD_DIM
            for pair in range(hpg // 2):
                h0 = g * hpg + 2 * pair
                c0 = h0 * SSM_HEAD_DIM
                xp = xd_bf[:, c0:c0 + LANES]
                yp = jnp.zeros((CHUNK, LANES), F32)
                for k, keep in ((0, head_lo), (1, ~head_lo)):
                    h = h0 + k
                    seg = a_cum[:, h:h + 1] - a_cum_t[h:h + 1, :]
                    m_h = (cb * jnp.exp2(jnp.where(tri, seg, NEG_BIG))).astype(BF16)
                    yp = yp + jnp.dot(m_h, jnp.where(keep, xp, jnp.zeros_like(xp)),
                                      preferred_element_type=F32)
                y_scr[:, c0:c0 + LANES] = y_scr[:, c0:c0 + LANES] + yp

        y = y_scr[...] + xs * dskip_ref[...]
        hf = y * gz_ref[sl, :].astype(F32)
        for g in range(SSM_GROUPS):
            hg = hf[:, g * hw:(g + 1) * hw]
            ms = jnp.mean(hg * hg, axis=-1, keepdims=True)
            os_scr[sl, g * hw:(g + 1) * hw] = (hg * lax.rsqrt(ms + RMS_EPS)
                                                * g_ref[:, g * hw:(g + 1) * hw]).astype(os_scr.dtype)

    for n, mixed in enumerate(mixes):
        rws = slice(n * sub, (n + 1) * sub)
        mixed = mixed + jnp.dot(os_scr[rws, :], w_ref[na:, :], preferred_element_type=F32)
        y = alpha * x_ref[rws, :] + mod_ref[2:3, :] * mixed
        mu = jnp.mean(y, axis=-1, keepdims=True)
        yc = y - mu
        var = jnp.mean(yc * yc, axis=-1, keepdims=True)
        o_ref[rws, :] = yc * lax.rsqrt(var + LN_EPS) * lng_ref[...] + lnb_ref[...]


def _ssd_out(xs, bm, cm, dt, gzs, alog, dskip_x, norm_g, e2, oa, gza, x2, w_out, mod3, ln_g, ln_b,
             *, rows, sub, nheads, alpha):
    s, nx = xs.shape
    d = x2.shape[1]
    hw = nx // SSM_GROUPS
    row = lambda a: pl.BlockSpec((rows, a.shape[1]), lambda i: (i, 0))
    full = lambda a: pl.BlockSpec(a.shape, lambda i: (0,) * a.ndim)
    return pl.pallas_call(
        functools.partial(_ssd_out_kernel, rows=rows, nheads=nheads, alpha=alpha, sub=sub),
        grid=(s // rows,),
        in_specs=[row(xs), row(bm), row(cm), row(dt), row(gzs),
                  full(alog), full(dskip_x), full(norm_g), full(e2),
                  row(oa), row(gza), row(x2),
                  pl.BlockSpec(w_out.shape, lambda i: (0, 0), pipeline_mode=pl.Buffered(1)),
                  full(mod3), full(ln_g), full(ln_b)],
        out_specs=pl.BlockSpec((rows, d), lambda i: (i, 0)),
        out_shape=jax.ShapeDtypeStruct((s, d), F32),
        scratch_shapes=[pltpu.VMEM((SSM_GROUPS, SSM_STATE, hw), F32),
                        pltpu.VMEM((CHUNK, nx), F32),
                        pltpu.VMEM(w_out.shape, BF16),
                        pltpu.VMEM((rows, nx), BF16)],
        compiler_params=pltpu.CompilerParams(dimension_semantics=("arbitrary",),
                                             vmem_limit_bytes=VMEM_LIMIT),
        name="ssd_out",
    )(xs, bm, cm, dt, gzs, alog, dskip_x, norm_g, e2, oa, gza, x2, w_out, mod3, ln_g, ln_b)


def _pad_cols(a, width):
    return jnp.pad(a, ((0, 0), (0, width - a.shape[1])))


def _regroup_kernel(wt_ref, o_ref, *, segments):
    wt = wt_ref[...]
    parts = [jnp.zeros((b, wt.shape[1]), wt.dtype) if a is None else wt[a:b, :] for a, b in segments]
    o_ref[...] = jnp.concatenate(parts, axis=0).T.astype(o_ref.dtype)


def _regroup_columns(w, segments, width):
    assert all(b % SUBLANES == 0 and (a or 0) % SUBLANES == 0 for a, b in segments)
    rows = w.shape[0]
    br = 256
    return pl.pallas_call(
        functools.partial(_regroup_kernel, segments=segments),
        grid=(rows // br,),
        in_specs=[pl.BlockSpec((w.shape[1], br), lambda i: (0, i))],
        out_specs=pl.BlockSpec((br, width), lambda i: (i, 0)),
        out_shape=jax.ShapeDtypeStruct((rows, width), BF16),
        compiler_params=pltpu.CompilerParams(vmem_limit_bytes=VMEM_LIMIT),
        name="regroup_w_in",
    )(w.T)


def _layer(x2, c, pos_row, w_ada, b_ada, w_in, q_norm_g, w_qb, kv_norm_g, w_kvb,
           conv_w, conv_b, dt_bias, a_log, d_skip, ssm_norm_g, w_out, ln_g, ln_b, *, depth):
    s, d = x2.shape
    q_rank = q_norm_g.shape[0]
    kv_rank = kv_norm_g.shape[0]
    nheads = dt_bias.shape[0]
    nch = conv_w.shape[1]
    nv = MLA_HEADS * V_HEAD_DIM
    nx = nheads * SSM_HEAD_DIM
    half = QK_ROPE_DIM // 2

    mod = _adaln_mod(c.reshape(d, 1), w_ada, b_ada.reshape(1, -1))
    mod3 = mod.reshape(3, d)

    o_q, o_ckv = 0, q_rank
    o_kr = o_ckv + kv_rank
    o_za = o_kr + QK_ROPE_DIM
    o_xbc = o_za + nv
    o_dt = o_xbc + nch
    o_zs = o_dt + nheads
    dt_pad = LANES - 2 * nheads
    groups = [("q", [(o_q, o_ckv)]), ("ckv", [(o_ckv, o_kr)]),
              ("krope", [(o_kr, o_za), (o_kr + half, o_za), (o_kr, o_kr + half)]),
              ("za", [(o_za, o_xbc)]), ("xbc", [(o_xbc, o_dt)]), ("zs", [(o_zs, o_zs + nx)]),
              ("dt", [(o_dt, o_zs), (o_dt, o_zs), (None, dt_pad)])]
    off, cur, segments = {}, 0, []
    for name, segs in groups:
        assert cur % LANES == 0
        off[name] = cur
        for a, b in segs:
            cur += b if a is None else b - a
            if segments and a is not None and segments[-1][0] is not None and segments[-1][1] == a:
                segments[-1] = (segments[-1][0], b)
            else:
                segments.append((a, b))
    assert cur % LANES == 0
    w_in_p = _regroup_columns(w_in, tuple(segments), cur)

    w_qb_p = w_qb.T.astype(BF16)

    inv_freq = 1.0 / (ROPE_THETA ** (jnp.arange(half, dtype=F32) / half))
    rope_tab = jnp.broadcast_to(inv_freq[:, None], (half, IN_SUB))

    dtb = _pad_cols(jnp.concatenate([dt_bias, dt_bias]).reshape(1, -1), LANES)
    q_scale = (QK_NOPE_DIM + QK_ROPE_DIM) ** -0.5 * LOG2E

    wkv3 = w_kvb.reshape(kv_rank, MLA_HEADS, QK_NOPE_DIM + V_HEAD_DIM)
    w_k = wkv3[:, :, :QK_NOPE_DIM].reshape(kv_rank, MLA_HEADS * QK_NOPE_DIM).astype(BF16)
    w_vt = wkv3[:, :, QK_NOPE_DIM:].reshape(kv_rank, nv).T.astype(BF16)

    q, k, vt, gza, xs, bm, cm, dt, gzs = _in_proj(
        x2, mod3, pos_row, rope_tab, w_in_p, q_norm_g.reshape(1, -1), w_qb_p,
        kv_norm_g.reshape(1, -1), w_k, w_vt, conv_w, conv_b.reshape(1, -1), dtb,
        off=off, tm=IN_ROWS, sub=IN_SUB, q_scale=q_scale)

    o_attn = _attention(q, k, vt, tq=ATTN_TILE, tk=ATTN_TILE, ntile=ATTN_TILES_PER_STEP)

    alog = _pad_cols(jnp.concatenate([a_log, a_log]).reshape(1, -1), LANES)
    e_head = jnp.repeat(jnp.eye(nheads, dtype=BF16), SSM_HEAD_DIM, axis=1)
    e2 = jnp.pad(jnp.concatenate([e_head, e_head], axis=0), ((0, LANES - 2 * nheads), (0, 0)))
    dskip_x = jnp.repeat(d_skip, SSM_HEAD_DIM).reshape(1, -1)
    alpha = (2.0 * depth) ** 0.25
    return _ssd_out(xs, bm, cm, dt, gzs, alog, dskip_x, ssm_norm_g.reshape(1, -1), e2,
                    o_attn, gza, x2, w_out, mod3, ln_g.reshape(1, -1), ln_b.reshape(1, -1),
                    rows=SSD_ROWS, sub=OUT_SUB, nheads=nheads, alpha=alpha)


def kernel(x, c, positions, w_ada, b_ada, w_in, q_norm_g, w_qb, kv_norm_g, w_kvb, conv_w, conv_b,
           dt_bias, a_log, d_skip, ssm_norm_g, w_out, ln_g, ln_b):
    b, s, d = x.shape
    depth = w_in.shape[0]
    assert b == 1, "one sequence per call"
    h = x.reshape(s, d)
    pos_row = positions.reshape(1, s)
    for l in range(depth):
        h = _layer(h, c, pos_row, w_ada[l], b_ada[l], w_in[l], q_norm_g[l], w_qb[l], kv_norm_g[l],
                   w_kvb[l], conv_w[l], conv_b[l], dt_bias[l], a_log[l], d_skip[l], ssm_norm_g[l],
                   w_out[l], ln_g[l], ln_b[l], depth=depth)
    return h.reshape(b, s, d)
```

```python
import functools
import math

import jax
import jax.numpy as jnp
from jax import lax
from jax.experimental import pallas as pl
from jax.experimental.pallas import tpu as pltpu

F32 = jnp.float32
BF16 = jnp.bfloat16

MLA_HEADS = 8
QK_NOPE_DIM = 128
QK_ROPE_DIM = 64
V_HEAD_DIM = 128
ROPE_THETA = 10000.0
SSM_HEAD_DIM = 64
SSM_GROUPS = 2
SSM_STATE = 128
CHUNK = 128
RMS_EPS = 1e-6
LN_EPS = 1e-5

LANES = 128
SUBLANES = 8
MXU_DIM = 256
QK_PAD = MXU_DIM
VMEM_LIMIT = 56 * 1024 * 1024

IN_ROWS = 512
IN_SUB = 256
OUT_SUB = 256
ATTN_TILE = 1024
ATTN_TILES_PER_STEP = 2
SSD_ROWS = 512

LOG2E = 1.4426950408889634
NEG_BIG = -1e30


def _silu(z):
    h = 0.5 * z
    return h * jnp.tanh(h) + h


def _softplus(z):
    return jnp.maximum(z, 0.0) + jnp.log1p(jnp.exp(-jnp.abs(z)))


def _split_bf16(x, parts):
    out, rem = [], x
    for _ in range(parts):
        hi = rem.astype(BF16)
        out.append(hi)
        rem = rem - hi.astype(F32)
    return out


def _mod_kernel(c_ref, w_ref, b_ref, o_ref):
    o_ref[...] = jnp.sum(w_ref[...] * c_ref[...], axis=0, keepdims=True) + b_ref[...]


def _adaln_mod(c_col, w_ada, b_ada):
    d, n = w_ada.shape
    bn = 1024
    return pl.pallas_call(
        _mod_kernel,
        grid=(n // bn,),
        in_specs=[pl.BlockSpec((d, 1), lambda j: (0, 0)),
                  pl.BlockSpec((d, bn), lambda j: (0, j)),
                  pl.BlockSpec((1, bn), lambda j: (0, j))],
        out_specs=pl.BlockSpec((1, bn), lambda j: (0, j)),
        out_shape=jax.ShapeDtypeStruct((1, n), F32),
        name="adaln_mod",
    )(c_col, w_ada, b_ada)


def _in_kernel(x_ref, mod_ref, pos_ref, rope_ref, w_in_ref, qg_ref, w_qb_ref, kvg_ref, w_k_ref, w_vt_ref,
               conv_w_ref, conv_b_ref, dtb_ref,
               q_out, k_out, vt_out, gza_out, xs_out, bm_out, cm_out, dt_out, gzs_out,
               xbuf, *, tm, sub, off, q_scale):
    i = pl.program_id(0)
    shift = mod_ref[0:1, :]
    scale1 = 1.0 + mod_ref[1:2, :]
    lane = lax.broadcasted_iota(jnp.int32, (sub, LANES), 1)
    low_half = lane < QK_ROPE_DIM
    nconv, nch = conv_w_ref.shape
    nx = xs_out.shape[1]
    nb = bm_out.shape[1]

    @pl.when(i == 0)
    def _():
        xbuf[0:SUBLANES, :] = jnp.zeros((SUBLANES, nch), F32)

    css, cos_ts, sin_ts = [], [], []
    for h in range(tm // sub):
        ang_t = pos_ref[:, h * sub:(h + 1) * sub].astype(F32) * rope_ref[...]
        cos_t, sin_t = jnp.cos(ang_t), jnp.sin(ang_t)
        cos_ts.append(cos_t)
        sin_ts.append(sin_t)
        css.append(jnp.concatenate([cos_t, cos_t, -sin_t, sin_t], axis=0).T)

    projs = []
    for h in range(tm // sub):
        u = x_ref[h * sub:(h + 1) * sub, :] * scale1 + shift
        projs.append(jnp.dot(u.astype(BF16), w_in_ref[...], preferred_element_type=F32))

    for h, (proj, cs) in enumerate(zip(projs, css)):
        rows = slice(h * sub, (h + 1) * sub)

        def rope(rr, cs=cs):
            t = rr * cs
            return jnp.where(low_half, t + pltpu.roll(t, QK_ROPE_DIM, axis=1), 0.0)

        q_lat = proj[:, off["q"]:off["q"] + qg_ref.shape[1]]
        qn = q_lat * lax.rsqrt(jnp.mean(q_lat * q_lat, axis=-1, keepdims=True) + RMS_EPS) * qg_ref[...]
        qft = lax.dot_general(w_qb_ref[...], qn.astype(BF16), (((1,), (1,)), ((), ())),
                              preferred_element_type=F32) * q_scale
        half = QK_ROPE_DIM // 2
        hd_rows = QK_NOPE_DIM + QK_ROPE_DIM
        cos_t, sin_t = cos_ts[h], sin_ts[h]
        for hd in range(MLA_HEADS):
            r0, o0 = hd * hd_rows, hd * QK_PAD
            x1 = qft[r0 + QK_NOPE_DIM:r0 + QK_NOPE_DIM + half, :]
            x2 = qft[r0 + QK_NOPE_DIM + half:r0 + hd_rows, :]
            q_out[o0:o0 + QK_NOPE_DIM, rows] = qft[r0:r0 + QK_NOPE_DIM, :].astype(BF16)
            q_out[o0 + QK_NOPE_DIM:o0 + QK_NOPE_DIM + half, rows] = (x1 * cos_t - x2 * sin_t).astype(BF16)
            q_out[o0 + QK_NOPE_DIM + half:o0 + hd_rows, rows] = (x2 * cos_t + x1 * sin_t).astype(BF16)
            q_out[o0 + hd_rows:o0 + QK_PAD, rows] = jnp.zeros((QK_PAD - hd_rows, sub), BF16)

        c_kv = proj[:, off["ckv"]:off["ckv"] + kvg_ref.shape[1]]
        ckvn = c_kv * lax.rsqrt(jnp.mean(c_kv * c_kv, axis=-1, keepdims=True) + RMS_EPS) * kvg_ref[...]
        ckvn_bf = ckvn.astype(BF16)
        kf = jnp.dot(ckvn_bf, w_k_ref[...], preferred_element_type=F32)
        vt_out[:, rows] = lax.dot_general(w_vt_ref[...], ckvn_bf, (((1,), (1,)), ((), ())),
                                          preferred_element_type=F32).astype(BF16)
        k_rope = rope(proj[:, off["krope"]:off["krope"] + LANES]).astype(BF16)
        k_out[rows, 0:kf.shape[1]] = kf.astype(BF16)
        k_out[rows, kf.shape[1]:kf.shape[1] + LANES] = k_rope

        gza_out[rows, :] = _silu(proj[:, off["za"]:off["za"] + gza_out.shape[1]]).astype(BF16)
        gzs_out[rows, :] = _silu(proj[:, off["zs"]:off["zs"] + gzs_out.shape[1]]).astype(BF16)
        dt_out[rows, :] = _softplus(proj[:, off["dt"]:off["dt"] + LANES] + dtb_ref[...])

        r1 = SUBLANES + h * sub
        xbuf[r1:r1 + sub, :] = proj[:, off["xbc"]:off["xbc"] + nch]
        xfull = xbuf[r1 - SUBLANES:r1 + sub, :]
        acc = conv_b_ref[...] + conv_w_ref[nconv - 1:nconv, :] * xfull[SUBLANES:, :]
        for k in range(1, nconv):
            shifted = pltpu.roll(xfull, k, axis=0)[SUBLANES:, :]
            acc = acc + conv_w_ref[nconv - 1 - k:nconv - k, :] * shifted
        xc = _silu(acc)
        xs_out[rows, :] = xc[:, :nx].astype(BF16)
        bm_out[rows, :] = xc[:, nx:nx + nb].astype(BF16)
        cm_out[rows, :] = xc[:, nx + nb:nx + 2 * nb].astype(BF16)

    xbuf[0:SUBLANES, :] = xbuf[tm:tm + SUBLANES, :]


def _in_proj(x2, mod3, pos_row, rope_tab, w_in_p, qg, w_qb_p, kvg, w_k, w_vt, conv_w, conv_b, dtb,
             *, off, tm, sub, q_scale):
    s, d = x2.shape
    nq = MLA_HEADS * QK_PAD
    nv = MLA_HEADS * V_HEAD_DIM
    nch = conv_w.shape[1]
    nbc = SSM_GROUPS * SSM_STATE
    nx = nch - 2 * nbc
    row = lambda w: pl.BlockSpec((tm, w), lambda i: (i, 0))
    full = lambda a: pl.BlockSpec(a.shape, lambda i: (0,) * a.ndim, pipeline_mode=pl.Buffered(1))
    outs = [
        jax.ShapeDtypeStruct((nq, s), BF16),
        jax.ShapeDtypeStruct((s, nv + LANES), BF16),
        jax.ShapeDtypeStruct((nv, s), BF16),
        jax.ShapeDtypeStruct((s, nv), BF16),
        jax.ShapeDtypeStruct((s, nx), BF16),
        jax.ShapeDtypeStruct((s, nbc), BF16),
        jax.ShapeDtypeStruct((s, nbc), BF16),
        jax.ShapeDtypeStruct((s, LANES), F32),
        jax.ShapeDtypeStruct((s, nx), BF16),
    ]
    return pl.pallas_call(
        functools.partial(_in_kernel, tm=tm, sub=sub, off=off, q_scale=q_scale),
        grid=(s // tm,),
        in_specs=[row(d), full(mod3), pl.BlockSpec((1, tm), lambda i: (0, i)), full(rope_tab),
                  full(w_in_p), full(qg), full(w_qb_p), full(kvg), full(w_k), full(w_vt),
                  full(conv_w), full(conv_b), full(dtb)],
        out_specs=[pl.BlockSpec((o.shape[0], tm), lambda i: (0, i)) if n in (0, 2) else row(o.shape[1])
                   for n, o in enumerate(outs)],
        out_shape=outs,
        scratch_shapes=[pltpu.VMEM((tm + 2 * SUBLANES, nch), F32)],
        compiler_params=pltpu.CompilerParams(dimension_semantics=("arbitrary",),
                                             vmem_limit_bytes=VMEM_LIMIT),
        name="in_proj",
    )(x2, mod3, pos_row, rope_tab, w_in_p, qg, w_qb_p, kvg, w_k, w_vt, conv_w, conv_b, dtb)


def _sublane_allmax(x):
    shift = SUBLANES // 2
    while shift:
        x = jnp.maximum(x, pltpu.roll(x, shift, axis=0))
        shift //= 2
    return x


def _sublane_allsum(x):
    shift = SUBLANES // 2
    while shift:
        x = x + pltpu.roll(x, shift, axis=0)
        shift //= 2
    return x


BF16_ROWS = 16
ACC_ROWS = V_HEAD_DIM + BF16_ROWS
EXP_ROWS = 32


def _attn_kernel(q_ref, qn_ref, kn_ref, kr_ref, vt_ref, o_ref, m_a, acc_a, m_b, acc_b,
                 s_a, s_b, s_c, cmax_a, cmax_b, cmax_c, p_a, p_b, p_c, al_a, al_b, al_c,
                 *, tq, tk, ntile, n_tiles):
    assert tq == tk
    g = pl.program_id(1)
    nsub = tk // SUBLANES
    nacc = ACC_ROWS // SUBLANES
    buf_a = (s_a, cmax_a, p_a, al_a)
    buf_b = (s_b, cmax_b, p_b, al_b)
    buf_c = (s_c, cmax_c, p_c, al_c)
    ones_rows = jnp.ones((ACC_ROWS - V_HEAD_DIM, tk), BF16)

    def k_rows(r0, n):
        return jnp.concatenate([kn_ref[pl.ds(r0, n), :], kr_ref[pl.ds(r0, n), :]], axis=1)

    def reset(stats):
        m_scr, acc_scr = stats
        m_scr[...] = jnp.full(m_scr.shape, NEG_BIG, F32)
        acc_scr[...] = jnp.zeros(acc_scr.shape, F32)

    def qk_stage(kv, q_rows, buf, diag=False, q_ref=q_ref):
        s_out, cmax_out = buf[0], buf[1]
        start = pl.multiple_of(kv * tk, tk)
        colmax = lambda x: _sublane_allmax(jnp.max(x.reshape(x.shape[0] // SUBLANES, SUBLANES, x.shape[1]), axis=0))
        if not diag:
            st = jnp.dot(k_rows(start, tk), q_ref[:, q_rows],
                         preferred_element_type=F32)
            s_out[...] = st
            cmax_out[...] = colmax(st)
            return
        hk = tk // 2
        q_lo = q_rows.start
        mask = (lax.broadcasted_iota(jnp.int32, (hk, tq), 0) <= lax.broadcasted_iota(jnp.int32, (hk, tq), 1))
        top = jnp.dot(k_rows(start, hk), q_ref[:, q_rows], preferred_element_type=F32)
        top = jnp.where(mask, top, NEG_BIG)
        bot = jnp.dot(k_rows(start + hk, hk), q_ref[:, q_lo + hk:q_lo + tq],
                      preferred_element_type=F32)
        bot = jnp.where(mask[:, :tq - hk], bot, NEG_BIG)
        s_out[0:hk, :] = top
        s_out[hk:tk, 0:hk] = jnp.full((tk - hk, hk), NEG_BIG, F32)
        s_out[hk:tk, hk:tq] = bot
        cm_top = colmax(top)
        cmax_out[:, 0:hk] = cm_top[:, 0:hk]
        cmax_out[:, hk:tq] = jnp.maximum(cm_top[:, hk:tq], colmax(bot))

    def softmax_stage(buf, stats):
        s_in, cmax_in, p_out, al_out = buf
        m_scr = stats[0]
        m_old = m_scr[...]
        m_new = jnp.maximum(m_old, cmax_in[...])
        al_out[...] = jnp.exp2(m_old - m_new)
        m_scr[...] = m_new
        for c in range(0, tq, MXU_DIM):
            m_c = m_new[:, c:c + MXU_DIM][None]
            for r in range(0, tk, EXP_ROWS):
                sc = s_in[r:r + EXP_ROWS, c:c + MXU_DIM].reshape(EXP_ROWS // SUBLANES, SUBLANES, MXU_DIM)
                p_out[r:r + EXP_ROWS, c:c + MXU_DIM] = (
                    jnp.exp2(sc - m_c).reshape(EXP_ROWS, MXU_DIM).astype(BF16))

    def pv_stage(kv, buf, stats):
        p_in, al_in = buf[2], buf[3]
        acc_scr = stats[1]
        start = pl.multiple_of(jnp.maximum(kv, 0) * tk, tk)
        vt_aug = jnp.concatenate([vt_ref[:, pl.ds(start, tk)], ones_rows], axis=0)
        pv = jnp.dot(vt_aug, p_in[...], preferred_element_type=F32)
        acc3 = acc_scr[...].reshape(nacc, SUBLANES, tq) * al_in[...][None]
        acc_scr[...] = acc3.reshape(ACC_ROWS, tq) + pv

    def finalize(q_rows, stats):
        acc_scr = stats[1]
        inv_l = 1.0 / acc_scr[V_HEAD_DIM:V_HEAD_DIM + SUBLANES, :]
        o_t = acc_scr[0:V_HEAD_DIM, :].reshape(V_HEAD_DIM // SUBLANES, SUBLANES, tq) * inv_l[None]
        o_ref[q_rows, :] = o_t.reshape(V_HEAD_DIM, tq).T.astype(o_ref.dtype)

    def make_step(qi, q_rows, stats):
        kv_of = lambda pos: jnp.where(pos == 0, qi, pos - 1)

        def step(pos, buf, other):
            qk_stage(kv_of(pos), q_rows, buf)
            pv_stage(kv_of(pos - 2), buf, stats)
            softmax_stage(other, stats)

        return kv_of, step

    def run_pairs(lo, hi, pair):
        def body(pp, carry):
            pair(pp)
            return carry

        lax.fori_loop(lo, hi, body, 0)

    bufs = (buf_a, buf_b)
    stats2 = ((m_a, acc_a), (m_b, acc_b))
    start = [0]
    for j in range(ntile - 1):
        last = start[j] if j % 2 == 0 else 1 - start[j]
        start.append(1 - last)

    def tile_ctx(j):
        qi = ntile * g + j
        rows = slice(j * tq, (j + 1) * tq)
        stats = stats2[j % 2]
        x, y = bufs[start[j]], bufs[1 - start[j]]
        kv_of, step = make_step(qi, rows, stats)
        return qi, rows, stats, x, y, kv_of, step

    def run_steps(j, first_pos, n_pairs):
        _, _, _, x, y, _, step = tile_ctx(j)
        even_buf, odd_buf = x, y

        def pair(pp):
            p0 = first_pos + 2 * pp
            if first_pos % 2 == 0:
                step(p0, even_buf, odd_buf)
                step(p0 + 1, odd_buf, even_buf)
            else:
                step(p0, odd_buf, even_buf)
                step(p0 + 1, even_buf, odd_buf)

        run_pairs(0, n_pairs, pair)

    def transition(j, last_buf, other):
        qi, rows, stats, _, _, kv_of, _ = tile_ctx(j)
        qi_n, rows_n, st_n, x_n, y_n, kv_n, _ = tile_ctx(j + 1)
        reset(st_n)
        qk_stage(qi_n, rows_n, x_n, diag=True)
        if other is not None:
            assert x_n is other
            pv_stage(kv_of(qi - 1), other, stats)
        softmax_stage(last_buf, stats)
        qk_stage(kv_n(1), rows_n, y_n)
        pv_stage(kv_of(qi), last_buf, stats)
        softmax_stage(x_n, st_n)
        finalize(rows, stats)

    qi0, rows0, st0, x0, y0, kv0, step0 = tile_ctx(0)
    reset(st0)

    @pl.when(g == 0)
    def _():
        qk_stage(qi0, rows0, buf_c, diag=True)
        transition(0, buf_c, None)

    @pl.when(g > 0)
    def _():
        qk_stage(kv0(1), rows0, y0)
        softmax_stage(buf_c, st0)
        qk_stage(kv0(2), rows0, x0)
        pv_stage(kv0(0), buf_c, st0)
        softmax_stage(y0, st0)
        run_steps(0, 3, (ntile * g - 2) // 2)
        transition(0, x0, y0)

    for j in range(1, ntile):
        qi, rows, stats, x, y, kv_of, step = tile_ctx(j)
        if j % 2 == 0:
            step(2, x, y)
            run_steps(j, 3, (ntile * g + j - 2) // 2)
        else:
            run_steps(j, 2, (ntile * g + j - 1) // 2)
        last_buf, other = (x, y) if j % 2 == 0 else (y, x)
        if j + 1 < ntile:
            transition(j, last_buf, other)
        else:
            qk_stage(jnp.minimum(qi + 1, n_tiles - 1), slice(0, tq), buf_c, diag=True, q_ref=qn_ref)
            pv_stage(kv_of(qi - 1), other, stats)
            softmax_stage(last_buf, stats)
            pv_stage(kv_of(qi), last_buf, stats)
            finalize(rows, stats)


def _attention(qt, k, vt, *, tq, tk, ntile):
    s = k.shape[0]
    assert ntile % 2 == 0 and s % (ntile * tq) == 0, "sequence length must be a multiple of ntile query tiles"
    stat = pltpu.VMEM((SUBLANES, tq), F32)
    n_tiles = s // tq
    return pl.pallas_call(
        functools.partial(_attn_kernel, tq=tq, tk=tk, ntile=ntile, n_tiles=n_tiles),
        grid=(MLA_HEADS, s // (ntile * tq)),
        in_specs=[pl.BlockSpec((QK_PAD, ntile * tq), lambda h, i: (h, i)),
                  pl.BlockSpec((QK_PAD, tq), lambda h, i: (h, jnp.minimum(ntile * (i + 1), n_tiles - 1))),
                  pl.BlockSpec((s, QK_NOPE_DIM), lambda h, i: (0, h)),
                  pl.BlockSpec((s, LANES), lambda h, i: (0, MLA_HEADS)),
                  pl.BlockSpec((V_HEAD_DIM, s), lambda h, i: (h, 0))],
        out_specs=pl.BlockSpec((ntile * tq, V_HEAD_DIM), lambda h, i: (i, h)),
        out_shape=jax.ShapeDtypeStruct((s, MLA_HEADS * V_HEAD_DIM), BF16),
        scratch_shapes=[stat, pltpu.VMEM((ACC_ROWS, tq), F32), stat, pltpu.VMEM((ACC_ROWS, tq), F32),
                        pltpu.VMEM((tk, tq), F32), pltpu.VMEM((tk, tq), F32), pltpu.VMEM((tk, tq), F32),
                        stat, stat, stat,
                        pltpu.VMEM((tk, tq), BF16), pltpu.VMEM((tk, tq), BF16), pltpu.VMEM((tk, tq), BF16),
                        stat, stat, stat],
        compiler_params=pltpu.CompilerParams(dimension_semantics=("arbitrary", "arbitrary"),
                                             vmem_limit_bytes=VMEM_LIMIT),
        name="attention",
    )(qt, qt, k, k, vt)


def _ssd_out_kernel(xs_ref, bm_ref, cm_ref, dt_ref, gz_ref, alog_ref, dskip_ref, g_ref, e2_ref,
                    oa_ref, gza_ref, x_ref, w32_ref, mod_ref, lng_ref, lnb_ref,
                    o_ref, state, y_scr, w_ref, os_scr, *, rows, nheads, alpha, sub):
    i = pl.program_id(0)
    gw = SSM_STATE
    hw = state.shape[2]
    na = oa_ref.shape[1]

    @pl.when(i == 0)
    def _():
        state[...] = jnp.zeros(state.shape, F32)
        for r in range(0, w_ref.shape[0], sub):
            w_ref[r:r + sub, :] = w32_ref[r:r + sub, :].astype(BF16)

    mixes = []
    for r in range(0, rows, sub):
        oa = (oa_ref[r:r + sub, :].astype(F32) * gza_ref[r:r + sub, :].astype(F32)).astype(BF16)
        mixes.append(jnp.dot(oa, w_ref[0:na, :], preferred_element_type=F32))

    a_neg = -jnp.exp(alog_ref[...])
    r_i = lax.broadcasted_iota(jnp.int32, (CHUNK, CHUNK), 0)
    c_i = lax.broadcasted_iota(jnp.int32, (CHUNK, CHUNK), 1)
    tri = c_i <= r_i
    tri_bf = tri.astype(BF16)
    lane = lax.broadcasted_iota(jnp.int32, (CHUNK, LANES), 1)
    head_lo = lane < SSM_HEAD_DIM
    first_copy = lax.broadcasted_iota(jnp.int32, (rows, LANES), 1) < nheads
    e2 = e2_ref[...]

    def expand(v):
        hi = v.astype(BF16)
        lo = (v - hi.astype(F32)).astype(BF16)
        return jnp.dot(jnp.where(first_copy, hi, lo), e2, preferred_element_type=F32)

    dt_all = dt_ref[...]
    da_all = dt_all * (a_neg * LOG2E)
    a_cums = []
    for cidx in range(rows // CHUNK):
        a_cum = jnp.zeros((CHUNK, LANES), F32)
        for part in _split_bf16(da_all[cidx * CHUNK:(cidx + 1) * CHUNK, :], 3):
            a_cum = a_cum + jnp.dot(tri_bf, part, preferred_element_type=F32)
        a_cums.append(a_cum)
    dt_x_all = expand(dt_all)
    ea_x_all = expand(jnp.exp2(jnp.concatenate(a_cums, axis=0)))
    dte_x_all = expand(jnp.exp2(jnp.concatenate([a[CHUNK - 1:CHUNK, :] - a for a in a_cums], axis=0)))

    for cidx in range(rows // CHUNK):
        sl = pl.ds(cidx * CHUNK, CHUNK)
        rs = slice(cidx * CHUNK, (cidx + 1) * CHUNK)
        a_cum = a_cums[cidx]
        a_cum_t = a_cum.T
        ea_x = ea_x_all[rs, :]

        xs = xs_ref[sl, :].astype(F32)
        xd = xs * dt_x_all[rs, :]
        xdd = (xd * dte_x_all[rs, :]).astype(BF16)
        xd_bf = xd.astype(BF16)

        for g in range(SSM_GROUPS):
            bg = bm_ref[sl, g * gw:(g + 1) * gw]
            cg = cm_ref[sl, g * gw:(g + 1) * gw]
            bg_t = bg.astype(F32).T.astype(BF16)
            cb = jnp.dot(cg, bg_t, preferred_element_type=F32)
            prev = state[g]
            y_off = jnp.dot(cg, prev.astype(BF16), preferred_element_type=F32)
            new = jnp.dot(bg_t, xdd[:, g * hw:(g + 1) * hw], preferred_element_type=F32)
            state[g] = prev * ea_x[CHUNK - 1:CHUNK, g * hw:(g + 1) * hw] + new
            y_scr[:, g * hw:(g + 1) * hw] = y_off * ea_x[:, g * hw:(g + 1) * hw]

            hpg = hw // SSM_HEAD_DIM
            for pair in range(hpg // 2):
                h0 = g * hpg + 2 * pair
                c0 = h0 * SSM_HEAD_DIM
                xp = xd_bf[:, c0:c0 + LANES]
                yp = jnp.zeros((CHUNK, LANES), F32)
                for k, keep in ((0, head_lo), (1, ~head_lo)):
                    h = h0 + k
                    seg = a_cum[:, h:h + 1] - a_cum_t[h:h + 1, :]
                    m_h = (cb * jnp.exp2(jnp.where(tri, seg, NEG_BIG))).astype(BF16)
                    yp = yp + jnp.dot(m_h, jnp.where(keep, xp, jnp.zeros_like(xp)),
                                      preferred_element_type=F32)
                y_scr[:, c0:c0 + LANES] = y_scr[:, c0:c0 + LANES] + yp

        y = y_scr[...] + xs * dskip_ref[...]
        hf = y * gz_ref[sl, :].astype(F32)
        for g in range(SSM_GROUPS):
            hg = hf[:, g * hw:(g + 1) * hw]
            ms = jnp.mean(hg * hg, axis=-1, keepdims=True)
            os_scr[sl, g * hw:(g + 1) * hw] = (hg * lax.rsqrt(ms + RMS_EPS)
                                                * g_ref[:, g * hw:(g + 1) * hw]).astype(os_scr.dtype)

    for n, mixed in enumerate(mixes):
        rws = slice(n * sub, (n + 1) * sub)
        mixed = mixed + jnp.dot(os_scr[rws, :], w_ref[na:, :], preferred_element_type=F32)
        y = alpha * x_ref[rws, :] + mod_ref[2:3, :] * mixed
        mu = jnp.mean(y, axis=-1, keepdims=True)
        yc = y - mu
        var = jnp.mean(yc * yc, axis=-1, keepdims=True)
        o_ref[rws, :] = yc * lax.rsqrt(var + LN_EPS) * lng_ref[...] + lnb_ref[...]


def _ssd_out(xs, bm, cm, dt, gzs, alog, dskip_x, norm_g, e2, oa, gza, x2, w_out, mod3, ln_g, ln_b,
             *, rows, sub, nheads, alpha):
    s, nx = xs.shape
    d = x2.shape[1]
    hw = nx // SSM_GROUPS
    row = lambda a: pl.BlockSpec((rows, a.shape[1]), lambda i: (i, 0))
    full = lambda a: pl.BlockSpec(a.shape, lambda i: (0,) * a.ndim)
    return pl.pallas_call(
        functools.partial(_ssd_out_kernel, rows=rows, nheads=nheads, alpha=alpha, sub=sub),
        grid=(s // rows,),
        in_specs=[row(xs), row(bm), row(cm), row(dt), row(gzs),
                  full(alog), full(dskip_x), full(norm_g), full(e2),
                  row(oa), row(gza), row(x2),
                  pl.BlockSpec(w_out.shape, lambda i: (0, 0), pipeline_mode=pl.Buffered(1)),
                  full(mod3), full(ln_g), full(ln_b)],
        out_specs=pl.BlockSpec((rows, d), lambda i: (i, 0)),
        out_shape=jax.ShapeDtypeStruct((s, d), F32),
        scratch_shapes=[pltpu.VMEM((SSM_GROUPS, SSM_STATE, hw), F32),
                        pltpu.VMEM((CHUNK, nx), F32),
                        pltpu.VMEM(w_out.shape, BF16),
                        pltpu.VMEM((rows, nx), BF16)],
        compiler_params=pltpu.CompilerParams(dimension_semantics=("arbitrary",),
                                             vmem_limit_bytes=VMEM_LIMIT),
        name="ssd_out",
    )(xs, bm, cm, dt, gzs, alog, dskip_x, norm_g, e2, oa, gza, x2, w_out, mod3, ln_g, ln_b)


def _pad_cols(a, width):
    return jnp.pad(a, ((0, 0), (0, width - a.shape[1])))


def _regroup_kernel(wt_ref, o_ref, *, segments):
    wt = wt_ref[...]
    parts = [jnp.zeros((b, wt.shape[1]), wt.dtype) if a is None else wt[a:b, :] for a, b in segments]
    o_ref[...] = jnp.concatenate(parts, axis=0).T.astype(o_ref.dtype)


def _regroup_columns(w, segments, width):
    assert all(b % SUBLANES == 0 and (a or 0) % SUBLANES == 0 for a, b in segments)
    rows = w.shape[0]
    br = 256
    return pl.pallas_call(
        functools.partial(_regroup_kernel, segments=segments),
        grid=(rows // br,),
        in_specs=[pl.BlockSpec((w.shape[1], br), lambda i: (0, i))],
        out_specs=pl.BlockSpec((br, width), lambda i: (i, 0)),
        out_shape=jax.ShapeDtypeStruct((rows, width), BF16),
        compiler_params=pltpu.CompilerParams(vmem_limit_bytes=VMEM_LIMIT),
        name="regroup_w_in",
    )(w.T)


def _layer(x2, c, pos_row, w_ada, b_ada, w_in, q_norm_g, w_qb, kv_norm_g, w_kvb,
           conv_w, conv_b, dt_bias, a_log, d_skip, ssm_norm_g, w_out, ln_g, ln_b, *, depth):
    s, d = x2.shape
    q_rank = q_norm_g.shape[0]
    kv_rank = kv_norm_g.shape[0]
    nheads = dt_bias.shape[0]
    nch = conv_w.shape[1]
    nv = MLA_HEADS * V_HEAD_DIM
    nx = nheads * SSM_HEAD_DIM
    half = QK_ROPE_DIM // 2

    mod = _adaln_mod(c.reshape(d, 1), w_ada, b_ada.reshape(1, -1))
    mod3 = mod.reshape(3, d)

    o_q, o_ckv = 0, q_rank
    o_kr = o_ckv + kv_rank
    o_za = o_kr + QK_ROPE_DIM
    o_xbc = o_za + nv
    o_dt = o_xbc + nch
    o_zs = o_dt + nheads
    dt_pad = LANES - 2 * nheads
    groups = [("q", [(o_q, o_ckv)]), ("ckv", [(o_ckv, o_kr)]),
              ("krope", [(o_kr, o_za), (o_kr + half, o_za), (o_kr, o_kr + half)]),
              ("za", [(o_za, o_xbc)]), ("xbc", [(o_xbc, o_dt)]), ("zs", [(o_zs, o_zs + nx)]),
              ("dt", [(o_dt, o_zs), (o_dt, o_zs), (None, dt_pad)])]
    off, cur, segments = {}, 0, []
    for name, segs in groups:
        assert cur % LANES == 0
        off[name] = cur
        for a, b in segs:
            cur += b if a is None else b - a
            if segments and a is not None and segments[-1][0] is not None and segments[-1][1] == a:
                segments[-1] = (segments[-1][0], b)
            else:
                segments.append((a, b))
    assert cur % LANES == 0
    w_in_p = _regroup_columns(w_in, tuple(segments), cur)

    w_qb_p = w_qb.T.astype(BF16)

    inv_freq = 1.0 / (ROPE_THETA ** (jnp.arange(half, dtype=F32) / half))
    rope_tab = jnp.broadcast_to(inv_freq[:, None], (half, IN_SUB))

    dtb = _pad_cols(jnp.concatenate([dt_bias, dt_bias]).reshape(1, -1), LANES)
    q_scale = (QK_NOPE_DIM + QK_ROPE_DIM) ** -0.5 * LOG2E

    wkv3 = w_kvb.reshape(kv_rank, MLA_HEADS, QK_NOPE_DIM + V_HEAD_DIM)
    w_k = wkv3[:, :, :QK_NOPE_DIM].reshape(kv_rank, MLA_HEADS * QK_NOPE_DIM).astype(BF16)
    w_vt = wkv3[:, :, QK_NOPE_DIM:].reshape(kv_rank, nv).T.astype(BF16)

    q, k, vt, gza, xs, bm, cm, dt, gzs = _in_proj(
        x2, mod3, pos_row, rope_tab, w_in_p, q_norm_g.reshape(1, -1), w_qb_p,
        kv_norm_g.reshape(1, -1), w_k, w_vt, conv_w, conv_b.reshape(1, -1), dtb,
        off=off, tm=IN_ROWS, sub=IN_SUB, q_scale=q_scale)

    o_attn = _attention(q, k, vt, tq=ATTN_TILE, tk=ATTN_TILE, ntile=ATTN_TILES_PER_STEP)

    alog = _pad_cols(jnp.concatenate([a_log, a_log]).reshape(1, -1), LANES)
    e_head = jnp.repeat(jnp.eye(nheads, dtype=BF16), SSM_HEAD_DIM, axis=1)
    e2 = jnp.pad(jnp.concatenate([e_head, e_head], axis=0), ((0, LANES - 2 * nheads), (0, 0)))
    dskip_x = jnp.repeat(d_skip, SSM_HEAD_DIM).reshape(1, -1)
    alpha = (2.0 * depth) ** 0.25
    return _ssd_out(xs, bm, cm, dt, gzs, alog, dskip_x, ssm_norm_g.reshape(1, -1), e2,
                    o_attn, gza, x2, w_out, mod3, ln_g.reshape(1, -1), ln_b.reshape(1, -1),
                    rows=SSD_ROWS, sub=OUT_SUB, nheads=nheads, alpha=alpha)


def kernel(x, c, positions, w_ada, b_ada, w_in, q_norm_g, w_qb, kv_norm_g, w_kvb, conv_w, conv_b,
           dt_bias, a_log, d_skip, ssm_norm_g, w_out, ln_g, ln_b):
    b, s, d = x.shape
    depth = w_in.shape[0]
    assert b == 1, "one sequence per call"
    h = x.reshape(s, d)
    pos_row = positions.reshape(1, s)
    for l in range(depth):
        h = _layer(h, c, pos_row, w_ada[l], b_ada[l], w_in[l], q_norm_g[l], w_qb[l], kv_norm_g[l],
                   w_kvb[l], conv_w[l], conv_b[l], dt_bias[l], a_log[l], d_skip[l], ssm_norm_g[l],
                   w_out[l], ln_g[l], ln_b[l], depth=depth)
    return h.reshape(b, s, d)
```
